```python
import math
import jax, jax.numpy as jnp
from jax import lax
import numpy as np

D_MODEL = 1024
BATCH = 4
SEQ = 4096
DEPTH = 1

HEAD_DIM = 64
RWKV_DIM = D_MODEL // 2
RWKV_HEADS = RWKV_DIM // HEAD_DIM
ATT_DIM = D_MODEL - RWKV_DIM
ATT_HEADS = ATT_DIM // HEAD_DIM
KV_HEADS = 2
KV_DIM = KV_HEADS * HEAD_DIM
MIX_DIM = RWKV_DIM + ATT_DIM
DECAY_LORA = 64
ICLR_LORA = 64
GATE_LORA = 128
SHIFT_DIM = 3 * RWKV_DIM + DECAY_LORA + ICLR_LORA + GATE_LORA
PROJ_DIM = SHIFT_DIM + ATT_DIM + 2 * KV_DIM
WINDOW = 128
BLOCK = 128
D_FF = 2816
CONV_WIDTH = 3
NORM_EPS = 1e-6
LNX_EPS = 64e-5
L2_EPS = 1e-12
MASK_VALUE = -1e30

kernel_name = "hymba_rwkv7_swa_alibi_convffn_encoder"


def rms_norm(x, g):
    xf = x.astype(jnp.float32)
    y = xf * lax.rsqrt(jnp.mean(xf * xf, axis=-1, keepdims=True) + NORM_EPS)
    return (y * g.astype(jnp.float32)).astype(x.dtype)


def split_cols(p, sizes):
    offsets = [int(o) for o in np.cumsum(sizes)[:-1]]
    return jnp.split(p, offsets, axis=-1)


def token_shift(p, mu_prev, mu_next):
    zero = jnp.zeros_like(p[:, :1])
    prev = jnp.concatenate([zero, p[:, :-1]], axis=1)
    nxt = jnp.concatenate([p[:, 1:], zero], axis=1)
    return p + (prev - p) * mu_prev + (nxt - p) * mu_next


def alibi_slopes(n_heads):
    return jnp.asarray(2.0 ** (-8.0 * np.arange(1, n_heads + 1, dtype=np.float32) / n_heads), jnp.float32)


def rwkv7_bidirectional(r, k, v, wd, ad, gd, decay_w0, decay_w2, iclr_a0, iclr_a2,
                        gate_g2, k_k, k_a, r_k, lnx_g, lnx_b):
    B, T, C = r.shape
    H, N = RWKV_HEADS, HEAD_DIM
    f32 = jnp.float32
    w_log = -jax.nn.softplus(-(decay_w0[:, None, None, :]
                               + jnp.einsum('btr,zrc->zbtc', jnp.tanh(wd), decay_w2))) - 0.5
    decay = jnp.exp(-jnp.exp(w_log.astype(f32))).reshape(2, B, T, H, N)
    a = jax.nn.sigmoid(iclr_a0 + ad @ iclr_a2)
    kk = (k * k_k).reshape(B, T, H, N).astype(f32)
    kk = kk / jnp.maximum(jnp.sqrt(jnp.sum(kk * kk, axis=-1, keepdims=True)), L2_EPS)
    k = k * (1.0 + (a - 1.0) * k_a)
    rh, kh, vh, ah = (t.reshape(B, T, H, N) for t in (r, k, v, a))

    def both_dirs(t):
        t = t.astype(f32)
        return jnp.stack([t, t[:, ::-1]], axis=0).transpose(2, 0, 1, 3, 4)

    dec = jnp.stack([decay[0], decay[1][:, ::-1]], axis=0).transpose(2, 0, 1, 3, 4)
    xs = (both_dirs(rh), dec, both_dirs(kh), both_dirs(vh),
          both_dirs(-kk), both_dirs(kk * ah.astype(f32)))

    def step(S, inp):
        rt, wt, kt, vt, at, bt = inp
        Sa = jnp.einsum('zbhij,zbhj->zbhi', S, at)
        S = S * wt[..., None, :] + Sa[..., :, None] * bt[..., None, :] + vt[..., :, None] * kt[..., None, :]
        return S, jnp.einsum('zbhij,zbhj->zbhi', S, rt)

    S0 = jnp.zeros((2, B, H, N, N), f32)
    _, ys = lax.scan(step, S0, xs)
    y = (ys[:, 0] + ys[::-1, 1]).transpose(1, 0, 2, 3)
    mu = jnp.mean(y, axis=-1, keepdims=True)
    var = jnp.mean(jnp.square(y - mu), axis=-1, keepdims=True)
    y = ((y - mu) * lax.rsqrt(var + LNX_EPS)).reshape(B, T, C)
    y = (y * lnx_g.astype(f32) + lnx_b.astype(f32)).astype(r.dtype)
    bonus = jnp.sum(rh * kh * r_k, axis=-1, keepdims=True) * vh
    g = jax.nn.sigmoid(gd) @ gate_g2
    return (y + bonus.reshape(B, T, C)) * g


def windowed_gqa_alibi(q, k, v, sink):
    B, T, H, D = q.shape
    Hkv = k.shape[2]
    G = H // Hkv
    nb = T // BLOCK
    f32 = jnp.float32
    qb = q.reshape(B, nb, BLOCK, Hkv, G, D)

    def band(t):
        tp = jnp.pad(t, ((0, 0), (WINDOW, WINDOW), (0, 0), (0, 0)))
        return jnp.concatenate([tp[:, o:o + T].reshape(B, nb, BLOCK, Hkv, D)
                                for o in (0, BLOCK, 2 * BLOCK)], axis=2)

    kb, vb = band(k), band(v)
    s = jnp.einsum('bnqhgd,bnkhd->bnhgqk', qb, kb).astype(f32) * (D ** -0.5)
    q_off = jnp.arange(BLOCK)
    k_off = jnp.arange(3 * BLOCK) - BLOCK
    dist = jnp.abs(k_off[None, :] - q_off[:, None])
    key_pos = jnp.arange(nb)[:, None] * BLOCK + k_off[None, :]
    valid = (dist <= WINDOW)[None] & ((key_pos >= 0) & (key_pos < T))[:, None, :]
    slopes = alibi_slopes(H).reshape(Hkv, G)
    s = s - slopes[:, :, None, None] * dist.astype(f32)
    s = jnp.where(valid[None, :, None, None], s, MASK_VALUE)
    sink_col = jnp.broadcast_to(sink.astype(f32).reshape(Hkv, G)[:, :, None, None], s.shape[:-1] + (1,))
    p = jax.nn.softmax(jnp.concatenate([s, sink_col], axis=-1), axis=-1)[..., :-1]
    o = jnp.einsum('bnhgqk,bnkhd->bnqhgd', p.astype(v.dtype), vb)
    return o.reshape(B, T, H * D)


def depthwise_conv_centred(h, w, b):
    C = h.shape[-1]
    y = lax.conv_general_dilated(h, w[:, None, :].astype(h.dtype), window_strides=(1,),
                                 padding=((CONV_WIDTH // 2, CONV_WIDTH // 2),),
                                 dimension_numbers=('NWC', 'WIO', 'NWC'),
                                 feature_group_count=C)
    return y + b


def setup_inputs(seed: int = 0) -> dict:
    key = jax.random.key(seed)
    ks = jax.random.split(key, 24)
    L, D, F = DEPTH, D_MODEL, D_FF
    nrm = lambda k, shape, s: jax.random.normal(k, shape, jnp.float32) * s
    return {
        "x": nrm(ks[0], (BATCH, SEQ, D), 1.0),
        "ln1_g": 1.0 + nrm(ks[1], (L, D), 0.02),
        "w_in": nrm(ks[2], (L, D, PROJ_DIM), D ** -0.5),
        "shift_mu_prev": jax.random.uniform(ks[3], (L, SHIFT_DIM), jnp.float32, 0.0, 0.5),
        "shift_mu_next": jax.random.uniform(ks[4], (L, SHIFT_DIM), jnp.float32, 0.0, 0.5),
        "decay_w0": jax.random.uniform(ks[5], (L, 2, RWKV_DIM), jnp.float32, -6.0, -1.0),
        "decay_w2": nrm(ks[6], (L, 2, DECAY_LORA, RWKV_DIM), 0.5 * DECAY_LORA ** -0.5),
        "iclr_a0": nrm(ks[7], (L, RWKV_DIM), 0.1),
        "iclr_a2": nrm(ks[8], (L, ICLR_LORA, RWKV_DIM), 0.5 * ICLR_LORA ** -0.5),
        "gate_g2": nrm(ks[9], (L, GATE_LORA, RWKV_DIM), GATE_LORA ** -0.5),
        "k_k": 0.85 + nrm(ks[10], (L, RWKV_DIM), 0.05),
        "k_a": 1.0 + nrm(ks[11], (L, RWKV_DIM), 0.05),
        "r_k": nrm(ks[12], (L, RWKV_HEADS, HEAD_DIM), 0.1),
        "lnx_g": 1.0 + nrm(ks[13], (L, RWKV_DIM), 0.02),
        "lnx_b": nrm(ks[14], (L, RWKV_DIM), 0.02),
        "attn_sink": nrm(ks[15], (L, ATT_HEADS), 1.0),
        "w_out": nrm(ks[16], (L, MIX_DIM, D), MIX_DIM ** -0.5),
        "ln2_g": 1.0 + nrm(ks[17], (L, D), 0.02),
        "ffn_w_gate": nrm(ks[18], (L, D, F), D ** -0.5),
        "ffn_w_up": nrm(ks[19], (L, D, F), D ** -0.5),
        "ffn_conv_w": nrm(ks[20], (L, CONV_WIDTH, F), CONV_WIDTH ** -0.5),
        "ffn_conv_b": nrm(ks[21], (L, F), 0.02),
        "ffn_w_down": nrm(ks[22], (L, F, D), F ** -0.5),
        "lnf_g": 1.0 + nrm(ks[23], (D,), 0.02),
    }


def reference(x, ln1_g, w_in, shift_mu_prev, shift_mu_next, decay_w0, decay_w2, iclr_a0,
              iclr_a2, gate_g2, k_k, k_a, r_k, lnx_g, lnx_b, attn_sink, w_out, ln2_g,
              ffn_w_gate, ffn_w_up, ffn_conv_w, ffn_conv_b, ffn_w_down, lnf_g):
    B, T, _ = x.shape
    for l in range(DEPTH):
        h = rms_norm(x, ln1_g[l])
        p = h @ w_in[l]
        p_rwkv = token_shift(p[..., :SHIFT_DIM], shift_mu_prev[l], shift_mu_next[l])
        r, k, v, wd, ad, gd = split_cols(p_rwkv, [RWKV_DIM, RWKV_DIM, RWKV_DIM,
                                                  DECAY_LORA, ICLR_LORA, GATE_LORA])
        q, ka, va = split_cols(p[..., SHIFT_DIM:], [ATT_DIM, KV_DIM, KV_DIM])
        o_rwkv = rwkv7_bidirectional(r, k, v, wd, ad, gd, decay_w0[l], decay_w2[l],
                                     iclr_a0[l], iclr_a2[l], gate_g2[l], k_k[l], k_a[l],
                                     r_k[l], lnx_g[l], lnx_b[l])
        o_att = windowed_gqa_alibi(q.reshape(B, T, ATT_HEADS, HEAD_DIM),
                                   ka.reshape(B, T, KV_HEADS, HEAD_DIM),
                                   va.reshape(B, T, KV_HEADS, HEAD_DIM), attn_sink[l])
        x = x + jnp.concatenate([o_rwkv, o_att], axis=-1) @ w_out[l]
        h = rms_norm(x, ln2_g[l])
        gate = depthwise_conv_centred(h @ ffn_w_gate[l], ffn_conv_w[l], ffn_conv_b[l])
        x = x + (jax.nn.gelu(gate, approximate=False) * (h @ ffn_w_up[l])) @ ffn_w_down[l]
    return rms_norm(x, lnf_g)
```

```python
import functools

import numpy as np
import jax
import jax.numpy as jnp
from jax import lax
from jax.experimental import pallas as pl
from jax.experimental.pallas import tpu as pltpu

F32 = jnp.float32
BF16 = jnp.bfloat16

D_MODEL = 1024
HEAD_DIM = 64
RWKV_DIM = 512
ATT_DIM = 512
ATT_HEADS = 8
KV_DIM = 128
LORA_DIM = 256
SHIFT_DIM = 3 * RWKV_DIM + LORA_DIM
PROJ_DIM = SHIFT_DIM + ATT_DIM + 2 * KV_DIM
WINDOW = 128
D_FF = 2816
CONV_WIDTH = 3
NORM_EPS = 1e-6
LNX_EPS = 64e-5
L2_EPS = 1e-12
MASK_VALUE = -1e30

LANES = 128
SUBLANES = 8
CHUNK = 64
SUB = 16
VMEM_LIMIT = 56 * 1024 * 1024

TM_IN = 256
TM_OUT = 512
TM_FFN = 512
FF_CHUNK = 256


def _dot(a, b):
    return jnp.dot(a.astype(BF16), b.astype(BF16), preferred_element_type=F32)


def _dot_nt(a, b):
    return lax.dot_general(a.astype(BF16), b.astype(BF16), (((1,), (1,)), ((), ())),
                           preferred_element_type=F32)


def _dot_tn(a, b):
    return lax.dot_general(a.astype(BF16), b.astype(BF16), (((0,), (0,)), ((), ())),
                           preferred_element_type=F32)


def _split2(x):
    hi = x.astype(BF16)
    lo = (x - hi.astype(F32)).astype(BF16)
    return hi, lo


def _split3(x):
    hi = x.astype(BF16)
    r1 = x - hi.astype(F32)
    mid = r1.astype(BF16)
    lo = (r1 - mid.astype(F32)).astype(BF16)
    return hi, mid, lo


def _rms_norm(x, g):
    return x * lax.rsqrt(jnp.mean(x * x, axis=-1, keepdims=True) + NORM_EPS) * g


def _in_proj_kernel(x_ref, xp_ref, xn_ref, ln1_ref, w_ref, mup_ref, mun_ref, w0_ref,
                    w2hi_ref, w2lo_ref, a0_ref, a2_ref, g2_ref, kk_ref, ka_ref, rk_ref,
                    ones_ref, tri_ref, sel_ref,
                    at0_ref, rt0_ref, bt0_ref, kt0_ref, at1_ref, rt1_ref, bt1_ref, kt1_ref,
                    v_ref, pl0_ref, pl1_ref, g_ref, bonus_ref, q_ref, ka_o_ref, va_o_ref):
    i = pl.program_id(1)
    n_tiles = pl.num_programs(1)
    tm = x_ref.shape[1]
    ln1 = ln1_ref[...]
    h = _rms_norm(x_ref[0], ln1).astype(BF16)
    halo = jnp.concatenate([xp_ref[0], xn_ref[0]], axis=0)
    hh = _rms_norm(halo, ln1).astype(BF16)
    has_prev = i > 0
    has_next = i < n_tiles - 1
    row = lax.broadcasted_iota(jnp.int32, (tm, 1), 0)

    def shifted(c0, c1):
        w = w_ref[:, c0:c1]
        p = _dot(h, w)
        ph = _dot(hh, w)
        prev_row = jnp.where(has_prev, ph[SUBLANES - 1:SUBLANES], 0.0)
        next_row = jnp.where(has_next, ph[SUBLANES:SUBLANES + 1], 0.0)
        prev = jnp.where(row == 0, prev_row, pltpu.roll(p, 1, axis=0))
        nxt = jnp.where(row == tm - 1, next_row, pltpu.roll(p, tm - 1, axis=0))
        return p + (prev - p) * mup_ref[:, c0:c1] + (nxt - p) * mun_ref[:, c0:c1]

    def seg_sum(x):
        hi, lo = _split2(x)
        ones = ones_ref[...]
        return (jnp.dot(hi, ones, preferred_element_type=F32)
                + jnp.dot(lo, ones, preferred_element_type=F32))

    codes = shifted(3 * RWKV_DIM, SHIFT_DIM)
    c_di = codes[:, :LANES]
    c_g = codes[:, LANES:]
    th_hi, th_lo = _split2(jnp.tanh(c_di))
    a_vec = jax.nn.sigmoid(a0_ref[...] + _dot(c_di, a2_ref[...]))
    g_ref[0] = _dot(jax.nn.sigmoid(c_g), g2_ref[...])

    r = shifted(0, RWKV_DIM)
    k = shifted(RWKV_DIM, 2 * RWKV_DIM)
    v = shifted(2 * RWKV_DIM, 3 * RWKV_DIM)
    v_ref[0] = v.astype(BF16)

    kkr = k * kk_ref[...]
    n2 = seg_sum(kkr * kkr)
    kk = kkr * lax.rsqrt(jnp.maximum(n2, L2_EPS * L2_EPS))
    k2 = k * (1.0 + (a_vec - 1.0) * ka_ref[...])
    b_vec = kk * a_vec
    bonus_ref[0] = seg_sum(r * k2 * rk_ref[...]) * v

    sel = sel_ref[...]
    dir_outs = ((at0_ref, rt0_ref, bt0_ref, kt0_ref, pl0_ref),
                (at1_ref, rt1_ref, bt1_ref, kt1_ref, pl1_ref))
    for d, (at_ref, rt_ref, bt_ref, kt_ref, pl_ref) in enumerate(dir_outs):
        w2hi = w2hi_ref[d]
        w2lo = w2lo_ref[d]
        z = (w0_ref[d:d + 1, :]
             + jnp.dot(th_hi, w2hi, preferred_element_type=F32)
             + jnp.dot(th_lo, w2hi, preferred_element_type=F32)
             + jnp.dot(th_hi, w2lo, preferred_element_type=F32))
        softplus = jnp.maximum(-z, 0.0) + jnp.log(1.0 + jnp.exp(-jnp.abs(z)))
        lw = -jnp.exp(-softplus - 0.5)
        cat = jnp.concatenate(_split3(lw), axis=1)
        tri = tri_ref[d]
        parts = []
        for j in range(tm // CHUNK):
            o = jnp.dot(tri, cat[j * CHUNK:(j + 1) * CHUNK], preferred_element_type=F32)
            parts.append(o[:, :RWKV_DIM] + o[:, RWKV_DIM:2 * RWKV_DIM] + o[:, 2 * RWKV_DIM:])
        ci = jnp.concatenate(parts, axis=0)
        tot = jnp.dot(sel, cat, preferred_element_type=F32)
        pl_ref[0, 0] = jnp.exp(tot[:, :RWKV_DIM] + tot[:, RWKV_DIM:2 * RWKV_DIM] + tot[:, 2 * RWKV_DIM:])
        e_inc = jnp.exp(ci)
        e_exc = jnp.exp(ci - lw)
        e_inv = jnp.exp(-ci)
        at_ref[0] = (-kk * e_exc).astype(BF16)
        rt_ref[0] = (r * e_inc).astype(BF16)
        bt_ref[0] = (b_vec * e_inv).astype(BF16)
        kt_ref[0] = (k2 * e_inv).astype(BF16)

    att = _dot(h, w_ref[:, SHIFT_DIM:PROJ_DIM])
    q_ref[0] = att[:, :ATT_DIM].astype(BF16)
    ka_o_ref[0] = att[:, ATT_DIM:ATT_DIM + KV_DIM].astype(BF16)
    va_o_ref[0] = att[:, ATT_DIM + KV_DIM:].astype(BF16)


def _in_proj(x, ln1_g, w_in, mu_prev, mu_next, decay_w0, w2hi, w2lo, iclr_a0, a2_pad, gate_g2,
             k_k, k_a, r_k, ones_blk, tri, sel):
    B, T, D = x.shape
    tm = TM_IN
    nt = T // tm
    rows8 = tm // SUBLANES
    const = lambda shape: pl.BlockSpec(shape, lambda b, i: (0,) * len(shape))
    tok = lambda width: pl.BlockSpec((1, tm, width), lambda b, i: (b, i, 0))
    in_specs = [
        tok(D),
        pl.BlockSpec((1, SUBLANES, D), lambda b, i: (b, jnp.maximum(i * rows8 - 1, 0), 0)),
        pl.BlockSpec((1, SUBLANES, D), lambda b, i: (b, jnp.minimum((i + 1) * rows8, T // SUBLANES - 1), 0)),
        const((1, D)), const((D, PROJ_DIM)), const((1, SHIFT_DIM)), const((1, SHIFT_DIM)),
        const((2, RWKV_DIM)), const((2, LANES, RWKV_DIM)), const((2, LANES, RWKV_DIM)),
        const((1, RWKV_DIM)), const((LANES, RWKV_DIM)), const((LANES, RWKV_DIM)),
        const((1, RWKV_DIM)), const((1, RWKV_DIM)), const((1, RWKV_DIM)),
        const((RWKV_DIM, RWKV_DIM)), const((2, CHUNK, CHUNK)), const((SUBLANES, tm)),
    ]
    tok_bf = jax.ShapeDtypeStruct((B, T, RWKV_DIM), BF16)
    tok_f32 = jax.ShapeDtypeStruct((B, T, RWKV_DIM), F32)
    pl_shape = jax.ShapeDtypeStruct((B, nt, SUBLANES, RWKV_DIM), F32)
    pl_spec = pl.BlockSpec((1, 1, SUBLANES, RWKV_DIM), lambda b, i: (b, i, 0, 0))
    out_shape = [tok_bf] * 9 + [pl_shape, pl_shape, tok_f32, tok_f32,
                                jax.ShapeDtypeStruct((B, T, ATT_DIM), BF16),
                                jax.ShapeDtypeStruct((B, T, KV_DIM), BF16),
                                jax.ShapeDtypeStruct((B, T, KV_DIM), BF16)]
    out_specs = [tok(RWKV_DIM)] * 9 + [pl_spec, pl_spec, tok(RWKV_DIM), tok(RWKV_DIM),
                                       tok(ATT_DIM), tok(KV_DIM), tok(KV_DIM)]
    return pl.pallas_call(
        _in_proj_kernel, grid=(B, nt), in_specs=in_specs, out_specs=out_specs, out_shape=out_shape,
        compiler_params=pltpu.CompilerParams(dimension_semantics=("parallel", "parallel"),
                                             vmem_limit_bytes=VMEM_LIMIT),
        name="in_proj",
    )(x, x, x, ln1_g, w_in, mu_prev, mu_next, decay_w0, w2hi, w2lo, iclr_a0, a2_pad, gate_g2,
      k_k, k_a, r_k, ones_blk, tri, sel)


def _pair_chunk(at, rt, bt, kt, v, p_last, s_prev, masks):
    strict, incl, sub_blk, eye, lane0, bd_mask = masks

    def bd(x):
        x = x.astype(BF16)
        zero = jnp.zeros_like(x)
        return jnp.concatenate([jnp.where(lane0, x, zero), jnp.where(lane0, zero, x)], axis=0)

    def pmm(x, y):
        return _dot(x, bd(y))

    lhs = jnp.concatenate([at, rt], axis=0)
    rhs_t = jnp.concatenate([bd(bt), bd(kt)], axis=0)
    sc = _dot_nt(lhs, rhs_t)
    a_ab = jnp.where(strict, sc[:CHUNK, :LANES], 0.0)
    a_ak = jnp.where(strict, sc[:CHUNK, LANES:], 0.0)
    a_rb = jnp.where(incl, sc[CHUNK:, :LANES], 0.0)
    a_rk = jnp.where(incl, sc[CHUNK:, LANES:], 0.0)

    a_d = jnp.where(sub_blk, a_ab, 0.0)
    a_o = a_ab - a_d
    a2 = pmm(a_d, a_d)
    a4 = pmm(a2, a2)
    a8 = pmm(a4, a4)
    t1 = eye + a_d
    t1 = t1 + pmm(t1, a2)
    t1 = t1 + pmm(t1, a4)
    t_d = t1 + pmm(t1, a8)
    m1 = pmm(t_d, a_o)
    m2 = pmm(m1, m1)
    t_inv = pmm(eye + m1 + m2 + pmm(m1, m2), t_d)

    x1 = pmm(a_ak, v)
    wu = _dot(t_inv, jnp.concatenate([bd(at), bd(x1)], axis=1))
    w = wu[:, :LANES]
    u0 = wu[:, LANES:]
    hs = _dot_nt(jnp.concatenate([w, rt], axis=0), s_prev)
    u = hs[:CHUNK] + u0
    y = hs[CHUNK:] + pmm(a_rb, u) + pmm(a_rk, v)
    upd = _dot_tn(jnp.concatenate([u.astype(BF16), v], axis=0),
                  jnp.concatenate([bt, kt], axis=0))
    s_new = (s_prev + jnp.where(bd_mask, upd, 0.0)) * p_last
    return y, s_new


def _scan_kernel(at0_ref, rt0_ref, bt0_ref, kt0_ref, v0_ref, pl0_ref,
                 at1_ref, rt1_ref, bt1_ref, kt1_ref, v1_ref, pl1_ref,
                 yf_ref, yb_ref, s_ref):
    c = pl.program_id(1)

    @pl.when(c == 0)
    def _():
        s_ref[...] = jnp.zeros_like(s_ref)

    ri = lax.broadcasted_iota(jnp.int32, (CHUNK, LANES), 0)
    ci = lax.broadcasted_iota(jnp.int32, (CHUNK, LANES), 1)
    cj = jnp.where(ci >= CHUNK, ci - CHUNK, ci)
    lane0 = ci < CHUNK
    eye = jnp.where(ri == cj, 1.0, 0.0).astype(F32)
    sub_blk = (ri // SUB) == (cj // SUB)
    r2 = lax.broadcasted_iota(jnp.int32, (LANES, LANES), 0)
    c2 = lax.broadcasted_iota(jnp.int32, (LANES, LANES), 1)
    bd_mask = (r2 >= CHUNK) == (c2 >= CHUNK)
    fwd_masks = (ri > cj, ri >= cj, sub_blk, eye, lane0, bd_mask)
    bwd_masks = (ri < cj, ri <= cj, sub_blk, eye, lane0, bd_mask)

    dirs = ((at0_ref, rt0_ref, bt0_ref, kt0_ref, v0_ref, pl0_ref, yf_ref, fwd_masks),
            (at1_ref, rt1_ref, bt1_ref, kt1_ref, v1_ref, pl1_ref, yb_ref, bwd_masks))
    for d, (at_ref, rt_ref, bt_ref, kt_ref, v_ref, pl_ref, y_ref, masks) in enumerate(dirs):
        for p in range(RWKV_DIM // LANES):
            sl = slice(p * LANES, (p + 1) * LANES)
            y, s_new = _pair_chunk(at_ref[0, :, sl], rt_ref[0, :, sl], bt_ref[0, :, sl],
                                   kt_ref[0, :, sl], v_ref[0, :, sl], pl_ref[0, 0, :, sl],
                                   s_ref[d, p], masks)
            y_ref[0, :, sl] = y
            s_ref[d, p] = s_new


def _scan(at0, rt0, bt0, kt0, at1, rt1, bt1, kt1, v, pl0, pl1):
    B, T, C = v.shape
    nc = T // CHUNK
    fwd = pl.BlockSpec((1, CHUNK, C), lambda b, c: (b, c, 0))
    bwd = pl.BlockSpec((1, CHUNK, C), lambda b, c: (b, nc - 1 - c, 0))
    pl_f = pl.BlockSpec((1, 1, 1, C), lambda b, c: (b, c, 0, 0))
    pl_b = pl.BlockSpec((1, 1, 1, C), lambda b, c: (b, nc - 1 - c, 0, 0))
    y_shape = jax.ShapeDtypeStruct((B, T, C), F32)
    return pl.pallas_call(
        _scan_kernel, grid=(B, nc),
        in_specs=[fwd, fwd, fwd, fwd, fwd, pl_f, bwd, bwd, bwd, bwd, bwd, pl_b],
        out_specs=[fwd, bwd], out_shape=[y_shape, y_shape],
        scratch_shapes=[pltpu.VMEM((2, C // LANES, LANES, LANES), F32)],
        compiler_params=pltpu.CompilerParams(dimension_semantics=("parallel", "arbitrary"),
                                             vmem_limit_bytes=VMEM_LIMIT),
        name="rwkv_scan",
    )(at0, rt0, bt0, kt0, v, pl0, at1, rt1, bt1, kt1, v, pl1)


def _attn_kernel(sink_ref, q_ref, kp_ref, kc_ref, kn_ref, vp_ref, vc_ref, vn_ref, o_ref):
    n = pl.program_id(1)
    nb = pl.num_programs(1)
    blk = q_ref.shape[1]
    nk = 3 * blk
    k_all = jnp.concatenate([kp_ref[0], kc_ref[0], kn_ref[0]], axis=0)
    v_all = jnp.concatenate([vp_ref[0], vc_ref[0], vn_ref[0]], axis=0)

    def swap_halves(x):
        return jnp.concatenate([x[:, HEAD_DIM:], x[:, :HEAD_DIM]], axis=1)

    lane0 = lax.broadcasted_iota(jnp.int32, (nk, LANES), 1) < HEAD_DIM

    def variants(x):
        xs = swap_halves(x)
        zero = jnp.zeros_like(x)
        return ((jnp.where(lane0, x, zero), jnp.where(lane0, zero, xs)),
                (jnp.where(lane0, xs, zero), jnp.where(lane0, zero, x)))

    k_var = variants(k_all)
    v_var = variants(v_all)

    qi = lax.broadcasted_iota(jnp.int32, (blk, nk), 0)
    kj = lax.broadcasted_iota(jnp.int32, (blk, nk), 1)
    dist = jnp.abs(kj - blk - qi)
    valid = (dist <= WINDOW) & ((n > 0) | (kj >= blk)) & ((n < nb - 1) | (kj < 2 * blk))
    dist_f = dist.astype(F32)
    group = ATT_HEADS // (KV_DIM // HEAD_DIM)

    for j in range(ATT_DIM // LANES):
        q_pair = q_ref[0, :, j * LANES:(j + 1) * LANES] * jnp.asarray(HEAD_DIM ** -0.5, BF16)
        acc = jnp.zeros((blk, LANES), F32)
        for e in range(2):
            head = 2 * j + e
            kv = head // group
            slope = float(2.0 ** (-8.0 * (head + 1) / ATT_HEADS))
            s = _dot_nt(q_pair, k_var[kv][e]) - slope * dist_f
            s = jnp.where(valid, s, MASK_VALUE)
            sink = sink_ref[head]
            m = jnp.maximum(jnp.max(s, axis=-1, keepdims=True), sink)
            p = jnp.exp(s - m)
            den = jnp.sum(p, axis=-1, keepdims=True) + jnp.exp(sink - m)
            acc = acc + _dot(p / den, v_var[kv][e])
        o_ref[0, :, j * LANES:(j + 1) * LANES] = acc.astype(o_ref.dtype)


def _attention(q, ka, va, sink):
    B, T, _ = q.shape
    blk = WINDOW
    nb = T // blk
    cur = lambda width: pl.BlockSpec((1, blk, width), lambda b, n: (b, n, 0))
    prev = pl.BlockSpec((1, blk, KV_DIM), lambda b, n: (b, jnp.maximum(n - 1, 0), 0))
    nxt = pl.BlockSpec((1, blk, KV_DIM), lambda b, n: (b, jnp.minimum(n + 1, nb - 1), 0))
    return pl.pallas_call(
        _attn_kernel, grid=(B, nb),
        in_specs=[pl.BlockSpec(memory_space=pltpu.SMEM), cur(ATT_DIM),
                  prev, cur(KV_DIM), nxt, prev, cur(KV_DIM), nxt],
        out_specs=cur(ATT_DIM), out_shape=jax.ShapeDtypeStruct((B, T, ATT_DIM), BF16),
        compiler_params=pltpu.CompilerParams(dimension_semantics=("parallel", "parallel"),
                                             vmem_limit_bytes=VMEM_LIMIT),
        name="band_attn",
    )(sink, q, ka, ka, ka, va, va, va)


def _out_proj_kernel(x_ref, yf_ref, yb_ref, bonus_ref, g_ref, oatt_ref, lg_ref, lb_ref,
                     mean_ref, w_ref, o_ref):
    mean_blk = mean_ref[...]

    def seg_mean(t):
        hi, lo = _split2(t)
        return (jnp.dot(hi, mean_blk, preferred_element_type=F32)
                + jnp.dot(lo, mean_blk, preferred_element_type=F32))

    y = yf_ref[0] + yb_ref[0]
    d = y - seg_mean(y)
    var = seg_mean(d * d)
    yn = d * lax.rsqrt(var + LNX_EPS) * lg_ref[...] + lb_ref[...]
    o_rwkv = (yn + bonus_ref[0]) * g_ref[0]
    mix = _dot(o_rwkv, w_ref[:RWKV_DIM, :]) + _dot(oatt_ref[0], w_ref[RWKV_DIM:, :])
    o_ref[0] = x_ref[0] + mix


def _out_proj(x, yf, yb, bonus, g, o_att, lnx_g, lnx_b, mean_blk, w_out):
    B, T, D = x.shape
    tm = TM_OUT
    tok = lambda width: pl.BlockSpec((1, tm, width), lambda b, i: (b, i, 0))
    const = lambda shape: pl.BlockSpec(shape, lambda b, i: (0,) * len(shape))
    return pl.pallas_call(
        _out_proj_kernel, grid=(B, T // tm),
        in_specs=[tok(D), tok(RWKV_DIM), tok(RWKV_DIM), tok(RWKV_DIM), tok(RWKV_DIM), tok(ATT_DIM),
                  const((1, RWKV_DIM)), const((1, RWKV_DIM)), const((RWKV_DIM, RWKV_DIM)),
                  const((D, D))],
        out_specs=tok(D), out_shape=jax.ShapeDtypeStruct((B, T, D), F32),
        compiler_params=pltpu.CompilerParams(dimension_semantics=("parallel", "parallel"),
                                             vmem_limit_bytes=VMEM_LIMIT),
        name="out_proj",
    )(x, yf, yb, bonus, g, o_att, lnx_g, lnx_b, mean_blk, w_out)


def _ffn_kernel(x_ref, xp_ref, xn_ref, ln2_ref, wg_ref, wu_ref, cw_ref, cb_ref, wd_ref, lnf_ref,
                o_ref):
    i = pl.program_id(1)
    n_tiles = pl.num_programs(1)
    tm = x_ref.shape[1]
    x = x_ref[0]
    ln2 = ln2_ref[...]
    h = _rms_norm(x, ln2).astype(BF16)
    hh = _rms_norm(jnp.concatenate([xp_ref[0], xn_ref[0]], axis=0), ln2).astype(BF16)
    has_prev = i > 0
    has_next = i < n_tiles - 1
    row = lax.broadcasted_iota(jnp.int32, (tm, 1), 0)
    acc = jnp.zeros((tm, D_MODEL), F32)
    for c in range(D_FF // FF_CHUNK):
        cs = slice(c * FF_CHUNK, (c + 1) * FF_CHUNK)
        wg = wg_ref[:, cs]
        gp = _dot(h, wg)
        gh = _dot(hh, wg)
        prev_row = jnp.where(has_prev, gh[SUBLANES - 1:SUBLANES], 0.0)
        next_row = jnp.where(has_next, gh[SUBLANES:SUBLANES + 1], 0.0)
        prev = jnp.where(row == 0, prev_row, pltpu.roll(gp, 1, axis=0))
        nxt = jnp.where(row == tm - 1, next_row, pltpu.roll(gp, tm - 1, axis=0))
        gate = (prev * cw_ref[0:1, cs] + gp * cw_ref[1:2, cs] + nxt * cw_ref[2:3, cs]
                + cb_ref[:, cs])
        act = 0.5 * gate * (1.0 + lax.erf(gate * float(1.0 / np.sqrt(2.0))))
        up = _dot(h, wu_ref[:, cs])
        acc = acc + _dot(act * up, wd_ref[cs, :])
    o_ref[0] = _rms_norm(x + acc, lnf_ref[...])


def _ffn(x1, ln2_g, wg, wu, conv_w, conv_b, wd, lnf_g):
    B, T, D = x1.shape
    tm = TM_FFN
    rows8 = tm // SUBLANES
    tok = pl.BlockSpec((1, tm, D), lambda b, i: (b, i, 0))
    const = lambda shape: pl.BlockSpec(shape, lambda b, i: (0,) * len(shape))
    resident = lambda shape: pl.BlockSpec(shape, lambda b, i: (0,) * len(shape),
                                          pipeline_mode=pl.Buffered(1))
    return pl.pallas_call(
        _ffn_kernel, grid=(B, T // tm),
        in_specs=[tok,
                  pl.BlockSpec((1, SUBLANES, D), lambda b, i: (b, jnp.maximum(i * rows8 - 1, 0), 0)),
                  pl.BlockSpec((1, SUBLANES, D),
                               lambda b, i: (b, jnp.minimum((i + 1) * rows8, T // SUBLANES - 1), 0)),
                  const((1, D)), resident((D, D_FF)), resident((D, D_FF)),
                  const((CONV_WIDTH, D_FF)), const((1, D_FF)), resident((D_FF, D)), const((1, D))],
        out_specs=tok, out_shape=jax.ShapeDtypeStruct((B, T, D), F32),
        compiler_params=pltpu.CompilerParams(dimension_semantics=("parallel", "parallel"),
                                             vmem_limit_bytes=VMEM_LIMIT),
        name="conv_ffn",
    )(x1, x1, x1, ln2_g, wg, wu, conv_w, conv_b, wd, lnf_g)


def _constants():
    idx = np.arange(RWKV_DIM)
    same_head = (idx[:, None] // HEAD_DIM) == (idx[None, :] // HEAD_DIM)
    t = np.arange(CHUNK)
    tri = np.stack([t[:, None] >= t[None, :], t[:, None] <= t[None, :]]).astype(np.float32)
    rows = np.arange(TM_IN)
    sel = (rows[None, :] // CHUNK == np.arange(SUBLANES)[:, None]).astype(np.float32)
    return (jnp.asarray(same_head.astype(np.float32), BF16),
            jnp.asarray(same_head.astype(np.float32) / HEAD_DIM, BF16),
            jnp.asarray(tri, BF16), jnp.asarray(sel, BF16))


def kernel(x, ln1_g, w_in, shift_mu_prev, shift_mu_next, decay_w0, decay_w2, iclr_a0, iclr_a2,
           gate_g2, k_k, k_a, r_k, lnx_g, lnx_b, attn_sink, w_out, ln2_g, ffn_w_gate, ffn_w_up,
           ffn_conv_w, ffn_conv_b, ffn_w_down, lnf_g):
    B, T, _ = x.shape
    assert w_in.shape[0] == 1, "single-layer block"
    l = 0
    ones_blk, mean_blk, tri, sel = _constants()
    row = lambda a: a.reshape(1, -1)
    w2 = decay_w2[l]
    w2_pad = jnp.concatenate([w2, jnp.zeros_like(w2)], axis=1)
    w2hi = w2_pad.astype(BF16)
    w2lo = (w2_pad - w2hi.astype(F32)).astype(BF16)
    a2_pad = jnp.concatenate([jnp.zeros_like(iclr_a2[l]), iclr_a2[l]], axis=0).astype(BF16)
    (at0, rt0, bt0, kt0, at1, rt1, bt1, kt1, v, pl0, pl1, g, bonus, q, ka, va) = _in_proj(
        x, row(ln1_g[l]), w_in[l].astype(BF16), row(shift_mu_prev[l]), row(shift_mu_next[l]),
        decay_w0[l], w2hi, w2lo, row(iclr_a0[l]), a2_pad, gate_g2[l].astype(BF16),
        row(k_k[l]), row(k_a[l]), row(r_k[l]), ones_blk, tri, sel)
    cpt = TM_IN // CHUNK
    pl0 = pl0[:, :, :cpt].reshape(B, T // CHUNK, 1, RWKV_DIM)
    pl1 = pl1[:, :, :cpt].reshape(B, T // CHUNK, 1, RWKV_DIM)
    yf, yb = _scan(at0, rt0, bt0, kt0, at1, rt1, bt1, kt1, v, pl0, pl1)
    o_att = _attention(q, ka, va, attn_sink[l])
    x1 = _out_proj(x, yf, yb, bonus, g, o_att, row(lnx_g[l]), row(lnx_b[l]), mean_blk,
                   w_out[l].astype(BF16))
    return _ffn(x1, row(ln2_g[l]), ffn_w_gate[l].astype(BF16), ffn_w_up[l].astype(BF16),
                ffn_conv_w[l], row(ffn_conv_b[l]), ffn_w_down[l].astype(BF16), row(lnf_g))
```

```python
import functools

import numpy as np
import jax
import jax.numpy as jnp
from jax import lax
from jax.experimental import pallas as pl
from jax.experimental.pallas import tpu as pltpu

F32 = jnp.float32
BF16 = jnp.bfloat16

D_MODEL = 1024
HEAD_DIM = 64
RWKV_DIM = 512
ATT_DIM = 512
ATT_HEADS = 8
KV_DIM = 128
LORA_DIM = 256
SHIFT_DIM = 3 * RWKV_DIM + LORA_DIM
PROJ_DIM = SHIFT_DIM + ATT_DIM + 2 * KV_DIM
WINDOW = 128
D_FF = 2816
CONV_WIDTH = 3
NORM_EPS = 1e-6
LNX_EPS = 64e-5
L2_EPS = 1e-12
MASK_VALUE = -1e30

LANES = 128
SUBLANES = 8
CHUNK = 64
SUB = 16
VMEM_LIMIT = 56 * 1024 * 1024

TM_IN = 256
TM_OUT = 512
TM_FFN = 512
FF_CHUNK = 256


def _dot(a, b):
    return jnp.dot(a.astype(BF16), b.astype(BF16), preferred_element_type=F32)


def _dot_nt(a, b):
    return lax.dot_general(a.astype(BF16), b.astype(BF16), (((1,), (1,)), ((), ())),
                           preferred_element_type=F32)


def _dot_tn(a, b):
    return lax.dot_general(a.astype(BF16), b.astype(BF16), (((0,), (0,)), ((), ())),
                           preferred_element_type=F32)


def _split2(x):
    hi = x.astype(BF16)
    lo = (x - hi.astype(F32)).astype(BF16)
    return hi, lo


def _split3(x):
    hi = x.astype(BF16)
    r1 = x - hi.astype(F32)
    mid = r1.astype(BF16)
    lo = (r1 - mid.astype(F32)).astype(BF16)
    return hi, mid, lo


def _rms_norm(x, g):
    return x * lax.rsqrt(jnp.mean(x * x, axis=-1, keepdims=True) + NORM_EPS) * g


def _in_proj_kernel(x_ref, xp_ref, xn_ref, ln1_ref, w_ref, mup_ref, mun_ref, w0_ref,
                    w2hi_ref, w2lo_ref, a0_ref, a2_ref, g2_ref, kk_ref, ka_ref, rk_ref,
                    ones_ref, tri_ref, sel_ref,
                    at0_ref, rt0_ref, bt0_ref, kt0_ref, at1_ref, rt1_ref, bt1_ref, kt1_ref,
                    v_ref, pl0_ref, pl1_ref, g_ref, bonus_ref, q_ref, ka_o_ref, va_o_ref):
    i = pl.program_id(1)
    n_tiles = pl.num_programs(1)
    tm = x_ref.shape[1]
    ln1 = ln1_ref[...]
    h = _rms_norm(x_ref[0], ln1).astype(BF16)
    halo = jnp.concatenate([xp_ref[0], xn_ref[0]], axis=0)
    hh = _rms_norm(halo, ln1).astype(BF16)
    has_prev = i > 0
    has_next = i < n_tiles - 1
    row = lax.broadcasted_iota(jnp.int32, (tm, 1), 0)

    def shifted(c0, c1):
        w = w_ref[:, c0:c1]
        p = _dot(h, w)
        ph = _dot(hh, w)
        prev_row = jnp.where(has_prev, ph[SUBLANES - 1:SUBLANES], 0.0)
        next_row = jnp.where(has_next, ph[SUBLANES:SUBLANES + 1], 0.0)
        prev = jnp.where(row == 0, prev_row, pltpu.roll(p, 1, axis=0))
        nxt = jnp.where(row == tm - 1, next_row, pltpu.roll(p, tm - 1, axis=0))
        return p + (prev - p) * mup_ref[:, c0:c1] + (nxt - p) * mun_ref[:, c0:c1]

    def seg_sum(x):
        hi, lo = _split2(x)
        ones = ones_ref[...]
        return (jnp.dot(hi, ones, preferred_element_type=F32)
                + jnp.dot(lo, ones, preferred_element_type=F32))

    codes = shifted(3 * RWKV_DIM, SHIFT_DIM)
    c_di = codes[:, :LANES]
    c_g = codes[:, LANES:]
    th_hi, th_lo = _split2(jnp.tanh(c_di))
    a_vec = jax.nn.sigmoid(a0_ref[...] + _dot(c_di, a2_ref[...]))
    g_ref[0] = _dot(jax.nn.sigmoid(c_g), g2_ref[...])

    r = shifted(0, RWKV_DIM)
    k = shifted(RWKV_DIM, 2 * RWKV_DIM)
    v = shifted(2 * RWKV_DIM, 3 * RWKV_DIM)
    v_ref[0] = v.astype(BF16)

    kkr = k * kk_ref[...]
    n2 = seg_sum(kkr * kkr)
    kk = kkr * lax.rsqrt(jnp.maximum(n2, L2_EPS * L2_EPS))
    k2 = k * (1.0 + (a_vec - 1.0) * ka_ref[...])
    b_vec = kk * a_vec
    bonus_ref[0] = seg_sum(r * k2 * rk_ref[...]) * v

    sel = sel_ref[...]
    dir_outs = ((at0_ref, rt0_ref, bt0_ref, kt0_ref, pl0_ref),
                (at1_ref, rt1_ref, bt1_ref, kt1_ref, pl1_ref))
    for d, (at_ref, rt_ref, bt_ref, kt_ref, pl_ref) in enumerate(dir_outs):
        w2hi = w2hi_ref[d]
        w2lo = w2lo_ref[d]
        z = (w0_ref[d:d + 1, :]
             + jnp.dot(th_hi, w2hi, preferred_element_type=F32)
             + jnp.dot(th_lo, w2hi, preferred_element_type=F32)
             + jnp.dot(th_hi, w2lo, preferred_element_type=F32))
        softplus = jnp.maximum(-z, 0.0) + jnp.log(1.0 + jnp.exp(-jnp.abs(z)))
        lw = -jnp.exp(-softplus - 0.5)
        cat = jnp.concatenate(_split3(lw), axis=1)
        tri = tri_ref[d]
        parts = []
        for j in range(tm // CHUNK):
            o = jnp.dot(tri, cat[j * CHUNK:(j + 1) * CHUNK], preferred_element_type=F32)
            parts.append(o[:, :RWKV_DIM] + o[:, RWKV_DIM:2 * RWKV_DIM] + o[:, 2 * RWKV_DIM:])
        ci = jnp.concatenate(parts, axis=0)
        tot = jnp.dot(sel, cat, preferred_element_type=F32)
        pl_ref[0, 0] = jnp.exp(tot[:, :RWKV_DIM] + tot[:, RWKV_DIM:2 * RWKV_DIM] + tot[:, 2 * RWKV_DIM:])
        e_inc = jnp.exp(ci)
        e_exc = jnp.exp(ci - lw)
        e_inv = jnp.exp(-ci)
        at_ref[0] = (-kk * e_exc).astype(BF16)
        rt_ref[0] = (r * e_inc).astype(BF16)
        bt_ref[0] = (b_vec * e_inv).astype(BF16)
        kt_ref[0] = (k2 * e_inv).astype(BF16)

    att = _dot(h, w_ref[:, SHIFT_DIM:PROJ_DIM])
    q_ref[0] = att[:, :ATT_DIM].astype(BF16)
    ka_o_ref[0] = att[:, ATT_DIM:ATT_DIM + KV_DIM].astype(BF16)
    va_o_ref[0] = att[:, ATT_DIM + KV_DIM:].astype(BF16)


def _in_proj(x, ln1_g, w_in, mu_prev, mu_next, decay_w0, w2hi, w2lo, iclr_a0, a2_pad, gate_g2,
             k_k, k_a, r_k, ones_blk, tri, sel):
    B, T, D = x.shape
    tm = TM_IN
    nt = T // tm
    rows8 = tm // SUBLANES
    const = lambda shape: pl.BlockSpec(shape, lambda b, i: (0,) * len(shape))
    tok = lambda width: pl.BlockSpec((1, tm, width), lambda b, i: (b, i, 0))
    in_specs = [
        tok(D),
        pl.BlockSpec((1, SUBLANES, D), lambda b, i: (b, jnp.maximum(i * rows8 - 1, 0), 0)),
        pl.BlockSpec((1, SUBLANES, D), lambda b, i: (b, jnp.minimum((i + 1) * rows8, T // SUBLANES - 1), 0)),
        const((1, D)), const((D, PROJ_DIM)), const((1, SHIFT_DIM)), const((1, SHIFT_DIM)),
        const((2, RWKV_DIM)), const((2, LANES, RWKV_DIM)), const((2, LANES, RWKV_DIM)),
        const((1, RWKV_DIM)), const((LANES, RWKV_DIM)), const((LANES, RWKV_DIM)),
        const((1, RWKV_DIM)), const((1, RWKV_DIM)), const((1, RWKV_DIM)),
        const((RWKV_DIM, RWKV_DIM)), const((2, CHUNK, CHUNK)), const((SUBLANES, tm)),
    ]
    tok_bf = jax.ShapeDtypeStruct((B, T, RWKV_DIM), BF16)
    tok_f32 = jax.ShapeDtypeStruct((B, T, RWKV_DIM), F32)
    pl_shape = jax.ShapeDtypeStruct((B, nt, SUBLANES, RWKV_DIM), F32)
    pl_spec = pl.BlockSpec((1, 1, SUBLANES, RWKV_DIM), lambda b, i: (b, i, 0, 0))
    out_shape = [tok_bf] * 9 + [pl_shape, pl_shape, tok_f32, tok_f32,
                                jax.ShapeDtypeStruct((B, T, ATT_DIM), BF16),
                                jax.ShapeDtypeStruct((B, T, KV_DIM), BF16),
                                jax.ShapeDtypeStruct((B, T, KV_DIM), BF16)]
    out_specs = [tok(RWKV_DIM)] * 9 + [pl_spec, pl_spec, tok(RWKV_DIM), tok(RWKV_DIM),
                                       tok(ATT_DIM), tok(KV_DIM), tok(KV_DIM)]
    return pl.pallas_call(
        _in_proj_kernel, grid=(B, nt), in_specs=in_specs, out_specs=out_specs, out_shape=out_shape,
        compiler_params=pltpu.CompilerParams(dimension_semantics=("parallel", "parallel"),
                                             vmem_limit_bytes=VMEM_LIMIT),
        name="in_proj",
    )(x, x, x, ln1_g, w_in, mu_prev, mu_next, decay_w0, w2hi, w2lo, iclr_a0, a2_pad, gate_g2,
      k_k, k_a, r_k, ones_blk, tri, sel)


def _pair_chunks(items, sub_blk, eye, lane0, bd_mask):
    n = range(len(items))
    at, rt, bt, kt, v, p_last, s_prev, strict, incl = zip(*items)

    def bd(x):
        x = x.astype(BF16)
        zero = jnp.zeros_like(x)
        return jnp.concatenate([jnp.where(lane0, x, zero), jnp.where(lane0, zero, x)], axis=0)

    def pmm(x, y):
        return _dot(x, bd(y))

    sc = [_dot_nt(jnp.concatenate([at[i], rt[i]], axis=0),
                  jnp.concatenate([bd(bt[i]), bd(kt[i])], axis=0)) for i in n]
    a_ab = [jnp.where(strict[i], sc[i][:CHUNK, :LANES], 0.0) for i in n]
    a_ak = [jnp.where(strict[i], sc[i][:CHUNK, LANES:], 0.0) for i in n]
    a_rb = [jnp.where(incl[i], sc[i][CHUNK:, :LANES], 0.0) for i in n]
    a_rk = [jnp.where(incl[i], sc[i][CHUNK:, LANES:], 0.0) for i in n]

    a_d = [jnp.where(sub_blk, a_ab[i], 0.0) for i in n]
    a_o = [a_ab[i] - a_d[i] for i in n]
    a2 = [pmm(a_d[i], a_d[i]) for i in n]
    x1 = [pmm(a_ak[i], v[i]) for i in n]
    yk = [pmm(a_rk[i], v[i]) for i in n]
    t1 = [eye + a_d[i] for i in n]
    a4 = [pmm(a2[i], a2[i]) for i in n]
    t1 = [t1[i] + pmm(t1[i], a2[i]) for i in n]
    a8 = [pmm(a4[i], a4[i]) for i in n]
    t1 = [t1[i] + pmm(t1[i], a4[i]) for i in n]
    t_d = [t1[i] + pmm(t1[i], a8[i]) for i in n]
    m1 = [pmm(t_d[i], a_o[i]) for i in n]
    xp = [_dot(t_d[i], jnp.concatenate([bd(at[i]), bd(x1[i])], axis=1)) for i in n]
    m2 = [pmm(m1[i], m1[i]) for i in n]
    q = [eye + m1[i] + m2[i] + pmm(m1[i], m2[i]) for i in n]
    wu = [_dot(q[i], jnp.concatenate([bd(xp[i][:, :LANES]), bd(xp[i][:, LANES:])], axis=1))
          for i in n]
    hs = [_dot_nt(jnp.concatenate([wu[i][:, :LANES].astype(BF16), rt[i]], axis=0), s_prev[i])
          for i in n]
    u = [hs[i][:CHUNK] + wu[i][:, LANES:] for i in n]
    y = [hs[i][CHUNK:] + pmm(a_rb[i], u[i]) + yk[i] for i in n]
    upd = [_dot_tn(jnp.concatenate([u[i].astype(BF16), v[i]], axis=0),
                   jnp.concatenate([bt[i], kt[i]], axis=0)) for i in n]
    s_new = [(s_prev[i] + jnp.where(bd_mask, upd[i], 0.0)) * p_last[i] for i in n]
    return list(zip(y, s_new))


def _scan_kernel(at0_ref, rt0_ref, bt0_ref, kt0_ref, v0_ref, pl0_ref,
                 at1_ref, rt1_ref, bt1_ref, kt1_ref, v1_ref, pl1_ref,
                 yf_ref, yb_ref, s_ref):
    c = pl.program_id(1)

    @pl.when(c == 0)
    def _():
        s_ref[...] = jnp.zeros_like(s_ref)

    ri = lax.broadcasted_iota(jnp.int32, (CHUNK, LANES), 0)
    ci = lax.broadcasted_iota(jnp.int32, (CHUNK, LANES), 1)
    cj = jnp.where(ci >= CHUNK, ci - CHUNK, ci)
    lane0 = ci < CHUNK
    eye = jnp.where(ri == cj, 1.0, 0.0).astype(F32)
    sub_blk = (ri // SUB) == (cj // SUB)
    r2 = lax.broadcasted_iota(jnp.int32, (LANES, LANES), 0)
    c2 = lax.broadcasted_iota(jnp.int32, (LANES, LANES), 1)
    bd_mask = (r2 >= CHUNK) == (c2 >= CHUNK)
    dirs = ((at0_ref, rt0_ref, bt0_ref, kt0_ref, v0_ref, pl0_ref, ri > cj, ri >= cj),
            (at1_ref, rt1_ref, bt1_ref, kt1_ref, v1_ref, pl1_ref, ri < cj, ri <= cj))
    n_pairs = RWKV_DIM // LANES
    items = []
    for d, (at_ref, rt_ref, bt_ref, kt_ref, v_ref, pl_ref, strict, incl) in enumerate(dirs):
        for p in range(n_pairs):
            sl = slice(p * LANES, (p + 1) * LANES)
            items.append((at_ref[0, :, sl], rt_ref[0, :, sl], bt_ref[0, :, sl], kt_ref[0, :, sl],
                          v_ref[0, :, sl], pl_ref[0, 0, :, sl], s_ref[d, p], strict, incl))
    outs = _pair_chunks(items, sub_blk, eye, lane0, bd_mask)
    for d, y_ref in enumerate((yf_ref, yb_ref)):
        y_ref[0] = jnp.concatenate([outs[d * n_pairs + p][0] for p in range(n_pairs)], axis=1)
        for p in range(n_pairs):
            s_ref[d, p] = outs[d * n_pairs + p][1]


def _scan(at0, rt0, bt0, kt0, at1, rt1, bt1, kt1, v, pl0, pl1):
    B, T, C = v.shape
    nc = T // CHUNK
    fwd = pl.BlockSpec((1, CHUNK, C), lambda b, c: (b, c, 0))
    bwd = pl.BlockSpec((1, CHUNK, C), lambda b, c: (b, nc - 1 - c, 0))
    pl_f = pl.BlockSpec((1, 1, 1, C), lambda b, c: (b, c, 0, 0))
    pl_b = pl.BlockSpec((1, 1, 1, C), lambda b, c: (b, nc - 1 - c, 0, 0))
    y_shape = jax.ShapeDtypeStruct((B, T, C), F32)
    return pl.pallas_call(
        _scan_kernel, grid=(B, nc),
        in_specs=[fwd, fwd, fwd, fwd, fwd, pl_f, bwd, bwd, bwd, bwd, bwd, pl_b],
        out_specs=[fwd, bwd], out_shape=[y_shape, y_shape],
        scratch_shapes=[pltpu.VMEM((2, C // LANES, LANES, LANES), F32)],
        compiler_params=pltpu.CompilerParams(dimension_semantics=("parallel", "arbitrary"),
                                             vmem_limit_bytes=VMEM_LIMIT),
        name="rwkv_scan",
    )(at0, rt0, bt0, kt0, v, pl0, at1, rt1, bt1, kt1, v, pl1)


def _attn_kernel(sink_ref, q_ref, kp_ref, kc_ref, kn_ref, vp_ref, vc_ref, vn_ref, o_ref):
    n = pl.program_id(1)
    nb = pl.num_programs(1)
    blk = q_ref.shape[1]
    nk = 3 * blk
    k_all = jnp.concatenate([kp_ref[0], kc_ref[0], kn_ref[0]], axis=0)
    v_all = jnp.concatenate([vp_ref[0], vc_ref[0], vn_ref[0]], axis=0)

    def swap_halves(x):
        return jnp.concatenate([x[:, HEAD_DIM:], x[:, :HEAD_DIM]], axis=1)

    lane0 = lax.broadcasted_iota(jnp.int32, (nk, LANES), 1) < HEAD_DIM

    def variants(x):
        xs = swap_halves(x)
        zero = jnp.zeros_like(x)
        return ((jnp.where(lane0, x, zero), jnp.where(lane0, zero, xs)),
                (jnp.where(lane0, xs, zero), jnp.where(lane0, zero, x)))

    k_var = variants(k_all)
    v_var = variants(v_all)

    qi = lax.broadcasted_iota(jnp.int32, (blk, nk), 0)
    kj = lax.broadcasted_iota(jnp.int32, (blk, nk), 1)
    dist = jnp.abs(kj - blk - qi)
    valid = (dist <= WINDOW) & ((n > 0) | (kj >= blk)) & ((n < nb - 1) | (kj < 2 * blk))
    dist_f = dist.astype(F32)
    group = ATT_HEADS // (KV_DIM // HEAD_DIM)

    for j in range(ATT_DIM // LANES):
        q_pair = q_ref[0, :, j * LANES:(j + 1) * LANES] * jnp.asarray(HEAD_DIM ** -0.5, BF16)
        acc = jnp.zeros((blk, LANES), F32)
        for e in range(2):
            head = 2 * j + e
            kv = head // group
            slope = float(2.0 ** (-8.0 * (head + 1) / ATT_HEADS))
            s = _dot_nt(q_pair, k_var[kv][e]) - slope * dist_f
            s = jnp.where(valid, s, MASK_VALUE)
            sink = sink_ref[head]
            m = jnp.maximum(jnp.max(s, axis=-1, keepdims=True), sink)
            p = jnp.exp(s - m)
            den = jnp.sum(p, axis=-1, keepdims=True) + jnp.exp(sink - m)
            acc = acc + _dot(p / den, v_var[kv][e])
        o_ref[0, :, j * LANES:(j + 1) * LANES] = acc.astype(o_ref.dtype)


def _attention(q, ka, va, sink):
    B, T, _ = q.shape
    blk = WINDOW
    nb = T // blk
    cur = lambda width: pl.BlockSpec((1, blk, width), lambda b, n: (b, n, 0))
    prev = pl.BlockSpec((1, blk, KV_DIM), lambda b, n: (b, jnp.maximum(n - 1, 0), 0))
    nxt = pl.BlockSpec((1, blk, KV_DIM), lambda b, n: (b, jnp.minimum(n + 1, nb - 1), 0))
    return pl.pallas_call(
        _attn_kernel, grid=(B, nb),
        in_specs=[pl.BlockSpec(memory_space=pltpu.SMEM), cur(ATT_DIM),
                  prev, cur(KV_DIM), nxt, prev, cur(KV_DIM), nxt],
        out_specs=cur(ATT_DIM), out_shape=jax.ShapeDtypeStruct((B, T, ATT_DIM), BF16),
        compiler_params=pltpu.CompilerParams(dimension_semantics=("parallel", "parallel"),
                                             vmem_limit_bytes=VMEM_LIMIT),
        name="band_attn",
    )(sink, q, ka, ka, ka, va, va, va)


def _out_proj_kernel(x_ref, yf_ref, yb_ref, bonus_ref, g_ref, oatt_ref, lg_ref, lb_ref,
                     mean_ref, w_ref, o_ref):
    mean_blk = mean_ref[...]

    def seg_mean(t):
        hi, lo = _split2(t)
        return (jnp.dot(hi, mean_blk, preferred_element_type=F32)
                + jnp.dot(lo, mean_blk, preferred_element_type=F32))

    y = yf_ref[0] + yb_ref[0]
    d = y - seg_mean(y)
    var = seg_mean(d * d)
    yn = d * lax.rsqrt(var + LNX_EPS) * lg_ref[...] + lb_ref[...]
    o_rwkv = (yn + bonus_ref[0]) * g_ref[0]
    mix = _dot(o_rwkv, w_ref[:RWKV_DIM, :]) + _dot(oatt_ref[0], w_ref[RWKV_DIM:, :])
    o_ref[0] = x_ref[0] + mix


def _out_proj(x, yf, yb, bonus, g, o_att, lnx_g, lnx_b, mean_blk, w_out):
    B, T, D = x.shape
    tm = TM_OUT
    tok = lambda width: pl.BlockSpec((1, tm, width), lambda b, i: (b, i, 0))
    const = lambda shape: pl.BlockSpec(shape, lambda b, i: (0,) * len(shape))
    return pl.pallas_call(
        _out_proj_kernel, grid=(B, T // tm),
        in_specs=[tok(D), tok(RWKV_DIM), tok(RWKV_DIM), tok(RWKV_DIM), tok(RWKV_DIM), tok(ATT_DIM),
                  const((1, RWKV_DIM)), const((1, RWKV_DIM)), const((RWKV_DIM, RWKV_DIM)),
                  const((D, D))],
        out_specs=tok(D), out_shape=jax.ShapeDtypeStruct((B, T, D), F32),
        compiler_params=pltpu.CompilerParams(dimension_semantics=("parallel", "parallel"),
                                             vmem_limit_bytes=VMEM_LIMIT),
        name="out_proj",
    )(x, yf, yb, bonus, g, o_att, lnx_g, lnx_b, mean_blk, w_out)


def _ffn_kernel(x_ref, xp_ref, xn_ref, ln2_ref, wg_ref, wu_ref, cw_ref, cb_ref, wd_ref, lnf_ref,
                o_ref):
    i = pl.program_id(1)
    n_tiles = pl.num_programs(1)
    tm = x_ref.shape[1]
    x = x_ref[0]
    ln2 = ln2_ref[...]
    h = _rms_norm(x, ln2).astype(BF16)
    hh = _rms_norm(jnp.concatenate([xp_ref[0], xn_ref[0]], axis=0), ln2).astype(BF16)
    has_prev = i > 0
    has_next = i < n_tiles - 1
    row = lax.broadcasted_iota(jnp.int32, (tm, 1), 0)
    acc = jnp.zeros((tm, D_MODEL), F32)
    for c in range(D_FF // FF_CHUNK):
        cs = slice(c * FF_CHUNK, (c + 1) * FF_CHUNK)
        wg = wg_ref[:, cs]
        gp = _dot(h, wg)
        gh = _dot(hh, wg)
        prev_row = jnp.where(has_prev, gh[SUBLANES - 1:SUBLANES], 0.0)
        next_row = jnp.where(has_next, gh[SUBLANES:SUBLANES + 1], 0.0)
        prev = jnp.where(row == 0, prev_row, pltpu.roll(gp, 1, axis=0))
        nxt = jnp.where(row == tm - 1, next_row, pltpu.roll(gp, tm - 1, axis=0))
        gate = (prev * cw_ref[0:1, cs] + gp * cw_ref[1:2, cs] + nxt * cw_ref[2:3, cs]
                + cb_ref[:, cs])
        act = 0.5 * gate * (1.0 + lax.erf(gate * float(1.0 / np.sqrt(2.0))))
        up = _dot(h, wu_ref[:, cs])
        acc = acc + _dot(act * up, wd_ref[cs, :])
    o_ref[0] = _rms_norm(x + acc, lnf_ref[...])


def _ffn(x1, ln2_g, wg, wu, conv_w, conv_b, wd, lnf_g):
    B, T, D = x1.shape
    tm = TM_FFN
    rows8 = tm // SUBLANES
    tok = pl.BlockSpec((1, tm, D), lambda b, i: (b, i, 0))
    const = lambda shape: pl.BlockSpec(shape, lambda b, i: (0,) * len(shape))
    resident = lambda shape: pl.BlockSpec(shape, lambda b, i: (0,) * len(shape),
                                          pipeline_mode=pl.Buffered(1))
    return pl.pallas_call(
        _ffn_kernel, grid=(B, T // tm),
        in_specs=[tok,
                  pl.BlockSpec((1, SUBLANES, D), lambda b, i: (b, jnp.maximum(i * rows8 - 1, 0), 0)),
                  pl.BlockSpec((1, SUBLANES, D),
                               lambda b, i: (b, jnp.minimum((i + 1) * rows8, T // SUBLANES - 1), 0)),
                  const((1, D)), resident((D, D_FF)), resident((D, D_FF)),
                  const((CONV_WIDTH, D_FF)), const((1, D_FF)), resident((D_FF, D)), const((1, D))],
        out_specs=tok, out_shape=jax.ShapeDtypeStruct((B, T, D), F32),
        compiler_params=pltpu.CompilerParams(dimension_semantics=("parallel", "parallel"),
                                             vmem_limit_bytes=VMEM_LIMIT),
        name="conv_ffn",
    )(x1, x1, x1, ln2_g, wg, wu, conv_w, conv_b, wd, lnf_g)


def _constants():
    idx = np.arange(RWKV_DIM)
    same_head = (idx[:, None] // HEAD_DIM) == (idx[None, :] // HEAD_DIM)
    t = np.arange(CHUNK)
    tri = np.stack([t[:, None] >= t[None, :], t[:, None] <= t[None, :]]).astype(np.float32)
    rows = np.arange(TM_IN)
    sel = (rows[None, :] // CHUNK == np.arange(SUBLANES)[:, None]).astype(np.float32)
    return (jnp.asarray(same_head.astype(np.float32), BF16),
            jnp.asarray(same_head.astype(np.float32) / HEAD_DIM, BF16),
            jnp.asarray(tri, BF16), jnp.asarray(sel, BF16))


def kernel(x, ln1_g, w_in, shift_mu_prev, shift_mu_next, decay_w0, decay_w2, iclr_a0, iclr_a2,
           gate_g2, k_k, k_a, r_k, lnx_g, lnx_b, attn_sink, w_out, ln2_g, ffn_w_gate, ffn_w_up,
           ffn_conv_w, ffn_conv_b, ffn_w_down, lnf_g):
    B, T, _ = x.shape
    assert w_in.shape[0] == 1, "single-layer block"
    l = 0
    ones_blk, mean_blk, tri, sel = _constants()
    row = lambda a: a.reshape(1, -1)
    w2 = decay_w2[l]
    w2_pad = jnp.concatenate([w2, jnp.zeros_like(w2)], axis=1)
    w2hi = w2_pad.astype(BF16)
    w2lo = (w2_pad - w2hi.astype(F32)).astype(BF16)
    a2_pad = jnp.concatenate([jnp.zeros_like(iclr_a2[l]), iclr_a2[l]], axis=0).astype(BF16)
    (at0, rt0, bt0, kt0, at1, rt1, bt1, kt1, v, pl0, pl1, g, bonus, q, ka, va) = _in_proj(
        x, row(ln1_g[l]), w_in[l].astype(BF16), row(shift_mu_prev[l]), row(shift_mu_next[l]),
        decay_w0[l], w2hi, w2lo, row(iclr_a0[l]), a2_pad, gate_g2[l].astype(BF16),
        row(k_k[l]), row(k_a[l]), row(r_k[l]), ones_blk, tri, sel)
    cpt = TM_IN // CHUNK
    pl0 = pl0[:, :, :cpt].reshape(B, T // CHUNK, 1, RWKV_DIM)
    pl1 = pl1[:, :, :cpt].reshape(B, T // CHUNK, 1, RWKV_DIM)
    yf, yb = _scan(at0, rt0, bt0, kt0, at1, rt1, bt1, kt1, v, pl0, pl1)
    o_att = _attention(q, ka, va, attn_sink[l])
    x1 = _out_proj(x, yf, yb, bonus, g, o_att, row(lnx_g[l]), row(lnx_b[l]), mean_blk,
                   w_out[l].astype(BF16))
    return _ffn(x1, row(ln2_g[l]), ffn_w_gate[l].astype(BF16), ffn_w_up[l].astype(BF16),
                ffn_conv_w[l], row(ffn_conv_b[l]), ffn_w_down[l].astype(BF16), row(lnf_g))
```

```python
import functools

import numpy as np
import jax
import jax.numpy as jnp
from jax import lax
from jax.experimental import pallas as pl
from jax.experimental.pallas import tpu as pltpu

F32 = jnp.float32
BF16 = jnp.bfloat16

D_MODEL = 1024
HEAD_DIM = 64
RWKV_DIM = 512
ATT_DIM = 512
ATT_HEADS = 8
KV_DIM = 128
LORA_DIM = 256
SHIFT_DIM = 3 * RWKV_DIM + LORA_DIM
PROJ_DIM = SHIFT_DIM + ATT_DIM + 2 * KV_DIM
WINDOW = 128
D_FF = 2816
CONV_WIDTH = 3
NORM_EPS = 1e-6
LNX_EPS = 64e-5
L2_EPS = 1e-12
MASK_VALUE = -1e30

LANES = 128
SUBLANES = 8
CHUNK = 64
SUB = 16
VMEM_LIMIT = 56 * 1024 * 1024

TM_IN = 256
TM_OUT = 512
TM_FFN = 512
FF_SPLITS = (0, 1536, D_FF)


def _dot(a, b):
    return jnp.dot(a.astype(BF16), b.astype(BF16), preferred_element_type=F32)


def _dot_nt(a, b):
    return lax.dot_general(a.astype(BF16), b.astype(BF16), (((1,), (1,)), ((), ())),
                           preferred_element_type=F32)


def _dot_tn(a, b):
    return lax.dot_general(a.astype(BF16), b.astype(BF16), (((0,), (0,)), ((), ())),
                           preferred_element_type=F32)


def _split2(x):
    hi = x.astype(BF16)
    lo = (x - hi.astype(F32)).astype(BF16)
    return hi, lo


def _split3(x):
    hi = x.astype(BF16)
    r1 = x - hi.astype(F32)
    mid = r1.astype(BF16)
    lo = (r1 - mid.astype(F32)).astype(BF16)
    return hi, mid, lo


def _rms_norm(x, g):
    return x * lax.rsqrt(jnp.mean(x * x, axis=-1, keepdims=True) + NORM_EPS) * g


def _in_proj_kernel(x_ref, xp_ref, xn_ref, ln1_ref, w_ref, mup_ref, mun_ref, w0_ref,
                    w2hi_ref, w2lo_ref, a0_ref, a2_ref, g2_ref, kk_ref, ka_ref, rk_ref,
                    ones_ref, tri_ref, sel_ref,
                    at0_ref, rt0_ref, bt0_ref, kt0_ref, at1_ref, rt1_ref, bt1_ref, kt1_ref,
                    v_ref, pl0_ref, pl1_ref, g_ref, bonus_ref, q_ref, ka_o_ref, va_o_ref):
    i = pl.program_id(1)
    n_tiles = pl.num_programs(1)
    tm = x_ref.shape[1]
    ln1 = ln1_ref[...]
    h = _rms_norm(x_ref[0], ln1).astype(BF16)
    halo = jnp.concatenate([xp_ref[0], xn_ref[0]], axis=0)
    hh = _rms_norm(halo, ln1).astype(BF16)
    has_prev = i > 0
    has_next = i < n_tiles - 1
    row = lax.broadcasted_iota(jnp.int32, (tm, 1), 0)

    def shifted(c0, c1):
        w = w_ref[:, c0:c1]
        p = _dot(h, w)
        ph = _dot(hh, w)
        prev_row = jnp.where(has_prev, ph[SUBLANES - 1:SUBLANES], 0.0)
        next_row = jnp.where(has_next, ph[SUBLANES:SUBLANES + 1], 0.0)
        prev = jnp.where(row == 0, prev_row, pltpu.roll(p, 1, axis=0))
        nxt = jnp.where(row == tm - 1, next_row, pltpu.roll(p, tm - 1, axis=0))
        return p + (prev - p) * mup_ref[:, c0:c1] + (nxt - p) * mun_ref[:, c0:c1]

    def seg_sum(x):
        hi, lo = _split2(x)
        ones = ones_ref[...]
        return (jnp.dot(hi, ones, preferred_element_type=F32)
                + jnp.dot(lo, ones, preferred_element_type=F32))

    codes = shifted(3 * RWKV_DIM, SHIFT_DIM)
    c_di = codes[:, :LANES]
    c_g = codes[:, LANES:]
    th_hi, th_lo = _split2(jnp.tanh(c_di))
    a_vec = jax.nn.sigmoid(a0_ref[...] + _dot(c_di, a2_ref[...]))
    g_ref[0] = _dot(jax.nn.sigmoid(c_g), g2_ref[...])

    r = shifted(0, RWKV_DIM)
    k = shifted(RWKV_DIM, 2 * RWKV_DIM)
    v = shifted(2 * RWKV_DIM, 3 * RWKV_DIM)
    v_ref[0] = v.astype(BF16)

    kkr = k * kk_ref[...]
    n2 = seg_sum(kkr * kkr)
    kk = kkr * lax.rsqrt(jnp.maximum(n2, L2_EPS * L2_EPS))
    k2 = k * (1.0 + (a_vec - 1.0) * ka_ref[...])
    b_vec = kk * a_vec
    bonus_ref[0] = seg_sum(r * k2 * rk_ref[...]) * v

    sel = sel_ref[...]
    dir_outs = ((at0_ref, rt0_ref, bt0_ref, kt0_ref, pl0_ref),
                (at1_ref, rt1_ref, bt1_ref, kt1_ref, pl1_ref))
    for d, (at_ref, rt_ref, bt_ref, kt_ref, pl_ref) in enumerate(dir_outs):
        w2hi = w2hi_ref[d]
        w2lo = w2lo_ref[d]
        z = (w0_ref[d:d + 1, :]
             + jnp.dot(th_hi, w2hi, preferred_element_type=F32)
             + jnp.dot(th_lo, w2hi, preferred_element_type=F32)
             + jnp.dot(th_hi, w2lo, preferred_element_type=F32))
        softplus = jnp.maximum(-z, 0.0) + jnp.log(1.0 + jnp.exp(-jnp.abs(z)))
        lw = -jnp.exp(-softplus - 0.5)
        cat = jnp.concatenate(_split3(lw), axis=1)
        tri = tri_ref[d]
        parts = []
        for j in range(tm // CHUNK):
            o = jnp.dot(tri, cat[j * CHUNK:(j + 1) * CHUNK], preferred_element_type=F32)
            parts.append(o[:, :RWKV_DIM] + o[:, RWKV_DIM:2 * RWKV_DIM] + o[:, 2 * RWKV_DIM:])
        ci = jnp.concatenate(parts, axis=0)
        tot = jnp.dot(sel, cat, preferred_element_type=F32)
        pl_ref[0, 0] = jnp.exp(tot[:, :RWKV_DIM] + tot[:, RWKV_DIM:2 * RWKV_DIM] + tot[:, 2 * RWKV_DIM:])
        e_inc = jnp.exp(ci)
        e_exc = jnp.exp(ci - lw)
        e_inv = jnp.exp(-ci)
        at_ref[0] = (-kk * e_exc).astype(BF16)
        rt_ref[0] = (r * e_inc).astype(BF16)
        bt_ref[0] = (b_vec * e_inv).astype(BF16)
        kt_ref[0] = (k2 * e_inv).astype(BF16)

    att = _dot(h, w_ref[:, SHIFT_DIM:PROJ_DIM])
    q_ref[0] = att[:, :ATT_DIM].astype(BF16)
    ka_o_ref[0] = att[:, ATT_DIM:ATT_DIM + KV_DIM].astype(BF16)
    va_o_ref[0] = att[:, ATT_DIM + KV_DIM:].astype(BF16)


def _in_proj(x, ln1_g, w_in, mu_prev, mu_next, decay_w0, w2hi, w2lo, iclr_a0, a2_pad, gate_g2,
             k_k, k_a, r_k, ones_blk, tri, sel):
    B, T, D = x.shape
    tm = TM_IN
    nt = T // tm
    rows8 = tm // SUBLANES
    const = lambda shape: pl.BlockSpec(shape, lambda b, i: (0,) * len(shape))
    tok = lambda width: pl.BlockSpec((1, tm, width), lambda b, i: (b, i, 0))
    in_specs = [
        tok(D),
        pl.BlockSpec((1, SUBLANES, D), lambda b, i: (b, jnp.maximum(i * rows8 - 1, 0), 0)),
        pl.BlockSpec((1, SUBLANES, D), lambda b, i: (b, jnp.minimum((i + 1) * rows8, T // SUBLANES - 1), 0)),
        const((1, D)), const((D, PROJ_DIM)), const((1, SHIFT_DIM)), const((1, SHIFT_DIM)),
        const((2, RWKV_DIM)), const((2, LANES, RWKV_DIM)), const((2, LANES, RWKV_DIM)),
        const((1, RWKV_DIM)), const((LANES, RWKV_DIM)), const((LANES, RWKV_DIM)),
        const((1, RWKV_DIM)), const((1, RWKV_DIM)), const((1, RWKV_DIM)),
        const((RWKV_DIM, RWKV_DIM)), const((2, CHUNK, CHUNK)), const((SUBLANES, tm)),
    ]
    tok_bf = jax.ShapeDtypeStruct((B, T, RWKV_DIM), BF16)
    tok_f32 = jax.ShapeDtypeStruct((B, T, RWKV_DIM), F32)
    pl_shape = jax.ShapeDtypeStruct((B, nt, SUBLANES, RWKV_DIM), F32)
    pl_spec = pl.BlockSpec((1, 1, SUBLANES, RWKV_DIM), lambda b, i: (b, i, 0, 0))
    out_shape = [tok_bf] * 9 + [pl_shape, pl_shape, tok_f32, tok_f32,
                                jax.ShapeDtypeStruct((B, T, ATT_DIM), BF16),
                                jax.ShapeDtypeStruct((B, T, KV_DIM), BF16),
                                jax.ShapeDtypeStruct((B, T, KV_DIM), BF16)]
    out_specs = [tok(RWKV_DIM)] * 9 + [pl_spec, pl_spec, tok(RWKV_DIM), tok(RWKV_DIM),
                                       tok(ATT_DIM), tok(KV_DIM), tok(KV_DIM)]
    return pl.pallas_call(
        _in_proj_kernel, grid=(B, nt), in_specs=in_specs, out_specs=out_specs, out_shape=out_shape,
        compiler_params=pltpu.CompilerParams(dimension_semantics=("parallel", "parallel"),
                                             vmem_limit_bytes=VMEM_LIMIT),
        name="in_proj",
    )(x, x, x, ln1_g, w_in, mu_prev, mu_next, decay_w0, w2hi, w2lo, iclr_a0, a2_pad, gate_g2,
      k_k, k_a, r_k, ones_blk, tri, sel)


def _pair_chunks(items, sub_blk, eye, lane0, bd_mask):
    n = range(len(items))
    at, rt, bt, kt, v, p_last, s_prev, strict, incl = zip(*items)

    def bd(x):
        x = x.astype(BF16)
        zero = jnp.zeros_like(x)
        return jnp.concatenate([jnp.where(lane0, x, zero), jnp.where(lane0, zero, x)], axis=0)

    def pmm(x, y):
        return _dot(x, bd(y))

    sc = [_dot_nt(jnp.concatenate([at[i], rt[i]], axis=0),
                  jnp.concatenate([bd(bt[i]), bd(kt[i])], axis=0)) for i in n]
    a_ab = [jnp.where(strict[i], sc[i][:CHUNK, :LANES], 0.0) for i in n]
    a_ak = [jnp.where(strict[i], sc[i][:CHUNK, LANES:], 0.0) for i in n]
    a_rb = [jnp.where(incl[i], sc[i][CHUNK:, :LANES], 0.0) for i in n]
    a_rk = [jnp.where(incl[i], sc[i][CHUNK:, LANES:], 0.0) for i in n]

    a_d = [jnp.where(sub_blk, a_ab[i], 0.0) for i in n]
    a_o = [a_ab[i] - a_d[i] for i in n]
    a2 = [pmm(a_d[i], a_d[i]) for i in n]
    xy = [pmm(jnp.concatenate([a_ak[i], a_rk[i]], axis=0), v[i]) for i in n]
    x1 = [xy[i][:CHUNK] for i in n]
    yk = [xy[i][CHUNK:] for i in n]
    t1 = [eye + a_d[i] for i in n]
    st = [pmm(jnp.concatenate([a2[i], t1[i]], axis=0), a2[i]) for i in n]
    a4 = [st[i][:CHUNK] for i in n]
    t1 = [t1[i] + st[i][CHUNK:] for i in n]
    st = [pmm(jnp.concatenate([a4[i], t1[i]], axis=0), a4[i]) for i in n]
    a8 = [st[i][:CHUNK] for i in n]
    t1 = [t1[i] + st[i][CHUNK:] for i in n]
    t_d = [t1[i] + pmm(t1[i], a8[i]) for i in n]
    mx = [_dot(t_d[i], jnp.concatenate([bd(a_o[i]), bd(at[i]), bd(x1[i])], axis=1)) for i in n]
    m1 = [mx[i][:, :LANES] for i in n]
    xp = [mx[i][:, LANES:] for i in n]
    m2 = [pmm(m1[i], m1[i]) for i in n]
    q = [eye + m1[i] + m2[i] + pmm(m1[i], m2[i]) for i in n]
    wu = [_dot(q[i], jnp.concatenate([bd(xp[i][:, :LANES]), bd(xp[i][:, LANES:])], axis=1))
          for i in n]
    hs = [_dot_nt(jnp.concatenate([wu[i][:, :LANES].astype(BF16), rt[i]], axis=0), s_prev[i])
          for i in n]
    u = [hs[i][:CHUNK] + wu[i][:, LANES:] for i in n]
    y = [hs[i][CHUNK:] + pmm(a_rb[i], u[i]) + yk[i] for i in n]
    upd = [_dot_tn(jnp.concatenate([u[i].astype(BF16), v[i]], axis=0),
                   jnp.concatenate([bt[i], kt[i]], axis=0)) for i in n]
    s_new = [(s_prev[i] + jnp.where(bd_mask, upd[i], 0.0)) * p_last[i] for i in n]
    return list(zip(y, s_new))


def _scan_kernel(at0_ref, rt0_ref, bt0_ref, kt0_ref, v0_ref, pl0_ref,
                 at1_ref, rt1_ref, bt1_ref, kt1_ref, v1_ref, pl1_ref,
                 yf_ref, yb_ref, s_ref):
    c = pl.program_id(0)

    @pl.when(c == 0)
    def _():
        s_ref[...] = jnp.zeros_like(s_ref)

    n_batch = v0_ref.shape[0]
    ri = lax.broadcasted_iota(jnp.int32, (CHUNK, LANES), 0)
    ci = lax.broadcasted_iota(jnp.int32, (CHUNK, LANES), 1)
    cj = jnp.where(ci >= CHUNK, ci - CHUNK, ci)
    lane0 = ci < CHUNK
    eye = jnp.where(ri == cj, 1.0, 0.0).astype(F32)
    sub_blk = (ri // SUB) == (cj // SUB)
    r2 = lax.broadcasted_iota(jnp.int32, (LANES, LANES), 0)
    c2 = lax.broadcasted_iota(jnp.int32, (LANES, LANES), 1)
    bd_mask = (r2 >= CHUNK) == (c2 >= CHUNK)
    dirs = ((at0_ref, rt0_ref, bt0_ref, kt0_ref, v0_ref, pl0_ref, ri > cj, ri >= cj),
            (at1_ref, rt1_ref, bt1_ref, kt1_ref, v1_ref, pl1_ref, ri < cj, ri <= cj))
    n_pairs = RWKV_DIM // LANES
    items = []
    for b in range(n_batch):
        for d, (at_ref, rt_ref, bt_ref, kt_ref, v_ref, pl_ref, strict, incl) in enumerate(dirs):
            for p in range(n_pairs):
                sl = slice(p * LANES, (p + 1) * LANES)
                items.append((at_ref[b, :, sl], rt_ref[b, :, sl], bt_ref[b, :, sl], kt_ref[b, :, sl],
                              v_ref[b, :, sl], pl_ref[b, 0, :, sl], s_ref[b, d, p], strict, incl))
    outs = _pair_chunks(items, sub_blk, eye, lane0, bd_mask)
    for b in range(n_batch):
        for d, y_ref in enumerate((yf_ref, yb_ref)):
            base = (b * 2 + d) * n_pairs
            y_ref[b] = jnp.concatenate([outs[base + p][0] for p in range(n_pairs)], axis=1)
            for p in range(n_pairs):
                s_ref[b, d, p] = outs[base + p][1]


def _scan(at0, rt0, bt0, kt0, at1, rt1, bt1, kt1, v, pl0, pl1):
    B, T, C = v.shape
    nc = T // CHUNK
    fwd = pl.BlockSpec((B, CHUNK, C), lambda c: (0, c, 0))
    bwd = pl.BlockSpec((B, CHUNK, C), lambda c: (0, nc - 1 - c, 0))
    pl_f = pl.BlockSpec((B, 1, 1, C), lambda c: (0, c, 0, 0))
    pl_b = pl.BlockSpec((B, 1, 1, C), lambda c: (0, nc - 1 - c, 0, 0))
    y_shape = jax.ShapeDtypeStruct((B, T, C), F32)
    return pl.pallas_call(
        _scan_kernel, grid=(nc,),
        in_specs=[fwd, fwd, fwd, fwd, fwd, pl_f, bwd, bwd, bwd, bwd, bwd, pl_b],
        out_specs=[fwd, bwd], out_shape=[y_shape, y_shape],
        scratch_shapes=[pltpu.VMEM((B, 2, C // LANES, LANES, LANES), F32)],
        compiler_params=pltpu.CompilerParams(dimension_semantics=("arbitrary",),
                                             vmem_limit_bytes=VMEM_LIMIT),
        name="rwkv_scan",
    )(at0, rt0, bt0, kt0, v, pl0, at1, rt1, bt1, kt1, v, pl1)


def _attn_kernel(sink_ref, bias_ref, q_ref, kp_ref, kc_ref, kn_ref, vp_ref, vc_ref, vn_ref, o_ref):
    blk = q_ref.shape[1]
    nk = 3 * blk
    k_all = jnp.concatenate([kp_ref[0], kc_ref[0], kn_ref[0]], axis=0)
    v_all = jnp.concatenate([vp_ref[0], vc_ref[0], vn_ref[0]], axis=0)

    def swap_halves(x):
        return jnp.concatenate([x[:, HEAD_DIM:], x[:, :HEAD_DIM]], axis=1)

    lane0 = lax.broadcasted_iota(jnp.int32, (nk, LANES), 1) < HEAD_DIM

    def variants(x):
        xs = swap_halves(x)
        zero = jnp.zeros_like(x)
        return ((jnp.where(lane0, x, zero), jnp.where(lane0, zero, xs)),
                (jnp.where(lane0, xs, zero), jnp.where(lane0, zero, x)))

    k_var = variants(k_all)
    v_var = variants(v_all)

    group = ATT_HEADS // (KV_DIM // HEAD_DIM)

    heads = range(ATT_HEADS)
    q_pairs = [q_ref[0, :, j * LANES:(j + 1) * LANES] * jnp.asarray(HEAD_DIM ** -0.5, BF16)
               for j in range(ATT_DIM // LANES)]
    s = [_dot_nt(q_pairs[h // 2], k_var[h // group][h % 2]) + bias_ref[0, h] for h in heads]
    m = [jnp.maximum(jnp.max(s[h], axis=-1, keepdims=True), sink_ref[h]) for h in heads]
    p = [jnp.exp(s[h] - m[h]) for h in heads]
    den = [jnp.sum(p[h], axis=-1, keepdims=True) + jnp.exp(sink_ref[h] - m[h]) for h in heads]
    o = [_dot(p[h], v_var[h // group][h % 2]) * (1.0 / den[h]) for h in heads]
    o_ref[0] = jnp.concatenate([o[2 * j] + o[2 * j + 1] for j in range(ATT_DIM // LANES)],
                               axis=1).astype(o_ref.dtype)


def _attn_bias(blk):
    qi = np.arange(blk)[:, None]
    kj = np.arange(3 * blk)[None, :]
    dist = np.abs(kj - blk - qi)
    slopes = 2.0 ** (-8.0 * np.arange(1, ATT_HEADS + 1, dtype=np.float32) / ATT_HEADS)
    alibi = -slopes[:, None, None].astype(np.float32) * dist[None].astype(np.float32)
    out = []
    for has_prev, has_next in ((False, True), (True, True), (True, False)):
        valid = (dist <= WINDOW) & (has_prev | (kj >= blk)) & (has_next | (kj < 2 * blk))
        out.append(np.where(valid[None], alibi, np.float32(MASK_VALUE)))
    return jnp.asarray(np.stack(out), F32)


def _attention(q, ka, va, sink):
    B, T, _ = q.shape
    blk = WINDOW
    nb = T // blk
    cur = lambda width: pl.BlockSpec((1, blk, width), lambda b, n: (b, n, 0))
    prev = pl.BlockSpec((1, blk, KV_DIM), lambda b, n: (b, jnp.maximum(n - 1, 0), 0))
    nxt = pl.BlockSpec((1, blk, KV_DIM), lambda b, n: (b, jnp.minimum(n + 1, nb - 1), 0))
    bias = pl.BlockSpec((1, ATT_HEADS, blk, 3 * blk),
                        lambda b, n: (jnp.where(n == 0, 0, jnp.where(n == nb - 1, 2, 1)), 0, 0, 0))
    return pl.pallas_call(
        _attn_kernel, grid=(B, nb),
        in_specs=[pl.BlockSpec(memory_space=pltpu.SMEM), bias, cur(ATT_DIM),
                  prev, cur(KV_DIM), nxt, prev, cur(KV_DIM), nxt],
        out_specs=cur(ATT_DIM), out_shape=jax.ShapeDtypeStruct((B, T, ATT_DIM), BF16),
        compiler_params=pltpu.CompilerParams(dimension_semantics=("parallel", "parallel"),
                                             vmem_limit_bytes=VMEM_LIMIT),
        name="band_attn",
    )(sink, _attn_bias(blk), q, ka, ka, ka, va, va, va)


def _out_proj_kernel(x_ref, yf_ref, yb_ref, bonus_ref, g_ref, oatt_ref, lg_ref, lb_ref,
                     mean_ref, w_ref, o_ref):
    mean_blk = mean_ref[...]

    def seg_mean(t):
        hi, lo = _split2(t)
        return (jnp.dot(hi, mean_blk, preferred_element_type=F32)
                + jnp.dot(lo, mean_blk, preferred_element_type=F32))

    y = yf_ref[0] + yb_ref[0]
    d = y - seg_mean(y)
    var = seg_mean(d * d)
    yn = d * lax.rsqrt(var + LNX_EPS) * lg_ref[...] + lb_ref[...]
    o_rwkv = (yn + bonus_ref[0]) * g_ref[0]
    mix = _dot(o_rwkv, w_ref[:RWKV_DIM, :]) + _dot(oatt_ref[0], w_ref[RWKV_DIM:, :])
    o_ref[0] = x_ref[0] + mix


def _out_proj(x, yf, yb, bonus, g, o_att, lnx_g, lnx_b, mean_blk, w_out):
    B, T, D = x.shape
    tm = TM_OUT
    tok = lambda width: pl.BlockSpec((1, tm, width), lambda b, i: (b, i, 0))
    const = lambda shape: pl.BlockSpec(shape, lambda b, i: (0,) * len(shape))
    return pl.pallas_call(
        _out_proj_kernel, grid=(B, T // tm),
        in_specs=[tok(D), tok(RWKV_DIM), tok(RWKV_DIM), tok(RWKV_DIM), tok(RWKV_DIM), tok(ATT_DIM),
                  const((1, RWKV_DIM)), const((1, RWKV_DIM)), const((RWKV_DIM, RWKV_DIM)),
                  const((D, D))],
        out_specs=tok(D), out_shape=jax.ShapeDtypeStruct((B, T, D), F32),
        compiler_params=pltpu.CompilerParams(dimension_semantics=("parallel", "parallel"),
                                             vmem_limit_bytes=VMEM_LIMIT),
        name="out_proj",
    )(x, yf, yb, bonus, g, o_att, lnx_g, lnx_b, mean_blk, w_out)


def _ffn_kernel(x_ref, xp_ref, xn_ref, ln2_ref, wg_ref, wu_ref, cw_ref, cb_ref, wd_ref, lnf_ref,
                o_ref):
    i = pl.program_id(1)
    n_tiles = pl.num_programs(1)
    tm = x_ref.shape[1]
    x = x_ref[0]
    ln2 = ln2_ref[...]
    h = _rms_norm(x, ln2).astype(BF16)
    hh = _rms_norm(jnp.concatenate([xp_ref[0], xn_ref[0]], axis=0), ln2).astype(BF16)
    has_prev = i > 0
    has_next = i < n_tiles - 1
    row = lax.broadcasted_iota(jnp.int32, (tm, 1), 0)
    acc = jnp.zeros((tm, D_MODEL), F32)
    for c0, c1 in zip(FF_SPLITS[:-1], FF_SPLITS[1:]):
        cs = slice(c0, c1)
        wg = wg_ref[:, cs]
        gp = _dot(h, wg)
        gh = _dot(hh, wg)
        prev_row = jnp.where(has_prev, gh[SUBLANES - 1:SUBLANES], 0.0)
        next_row = jnp.where(has_next, gh[SUBLANES:SUBLANES + 1], 0.0)
        prev = jnp.where(row == 0, prev_row, pltpu.roll(gp, 1, axis=0))
        nxt = jnp.where(row == tm - 1, next_row, pltpu.roll(gp, tm - 1, axis=0))
        gate = (prev * cw_ref[0:1, cs] + gp * cw_ref[1:2, cs] + nxt * cw_ref[2:3, cs]
                + cb_ref[:, cs])
        act = 0.5 * gate * (1.0 + lax.erf(gate * float(1.0 / np.sqrt(2.0))))
        up = _dot(h, wu_ref[:, cs])
        acc = acc + _dot(act * up, wd_ref[cs, :])
    o_ref[0] = _rms_norm(x + acc, lnf_ref[...])


def _ffn(x1, ln2_g, wg, wu, conv_w, conv_b, wd, lnf_g):
    B, T, D = x1.shape
    tm = TM_FFN
    rows8 = tm // SUBLANES
    tok = pl.BlockSpec((1, tm, D), lambda b, i: (b, i, 0))
    const = lambda shape: pl.BlockSpec(shape, lambda b, i: (0,) * len(shape))
    resident = lambda shape: pl.BlockSpec(shape, lambda b, i: (0,) * len(shape),
                                          pipeline_mode=pl.Buffered(1))
    return pl.pallas_call(
        _ffn_kernel, grid=(B, T // tm),
        in_specs=[tok,
                  pl.BlockSpec((1, SUBLANES, D), lambda b, i: (b, jnp.maximum(i * rows8 - 1, 0), 0)),
                  pl.BlockSpec((1, SUBLANES, D),
                               lambda b, i: (b, jnp.minimum((i + 1) * rows8, T // SUBLANES - 1), 0)),
                  const((1, D)), resident((D, D_FF)), resident((D, D_FF)),
                  const((CONV_WIDTH, D_FF)), const((1, D_FF)), resident((D_FF, D)), const((1, D))],
        out_specs=tok, out_shape=jax.ShapeDtypeStruct((B, T, D), F32),
        compiler_params=pltpu.CompilerParams(dimension_semantics=("parallel", "parallel"),
                                             vmem_limit_bytes=VMEM_LIMIT),
        name="conv_ffn",
    )(x1, x1, x1, ln2_g, wg, wu, conv_w, conv_b, wd, lnf_g)


def _constants():
    idx = np.arange(RWKV_DIM)
    same_head = (idx[:, None] // HEAD_DIM) == (idx[None, :] // HEAD_DIM)
    t = np.arange(CHUNK)
    tri = np.stack([t[:, None] >= t[None, :], t[:, None] <= t[None, :]]).astype(np.float32)
    rows = np.arange(TM_IN)
    sel = (rows[None, :] // CHUNK == np.arange(SUBLANES)[:, None]).astype(np.float32)
    return (jnp.asarray(same_head.astype(np.float32), BF16),
            jnp.asarray(same_head.astype(np.float32) / HEAD_DIM, BF16),
            jnp.asarray(tri, BF16), jnp.asarray(sel, BF16))


def kernel(x, ln1_g, w_in, shift_mu_prev, shift_mu_next, decay_w0, decay_w2, iclr_a0, iclr_a2,
           gate_g2, k_k, k_a, r_k, lnx_g, lnx_b, attn_sink, w_out, ln2_g, ffn_w_gate, ffn_w_up,
           ffn_conv_w, ffn_conv_b, ffn_w_down, lnf_g):
    B, T, _ = x.shape
    assert w_in.shape[0] == 1, "single-layer block"
    l = 0
    ones_blk, mean_blk, tri, sel = _constants()
    row = lambda a: a.reshape(1, -1)
    w2 = decay_w2[l]
    w2_pad = jnp.concatenate([w2, jnp.zeros_like(w2)], axis=1)
    w2hi = w2_pad.astype(BF16)
    w2lo = (w2_pad - w2hi.astype(F32)).astype(BF16)
    a2_pad = jnp.concatenate([jnp.zeros_like(iclr_a2[l]), iclr_a2[l]], axis=0).astype(BF16)
    (at0, rt0, bt0, kt0, at1, rt1, bt1, kt1, v, pl0, pl1, g, bonus, q, ka, va) = _in_proj(
        x, row(ln1_g[l]), w_in[l].astype(BF16), row(shift_mu_prev[l]), row(shift_mu_next[l]),
        decay_w0[l], w2hi, w2lo, row(iclr_a0[l]), a2_pad, gate_g2[l].astype(BF16),
        row(k_k[l]), row(k_a[l]), row(r_k[l]), ones_blk, tri, sel)
    cpt = TM_IN // CHUNK
    pl0 = pl0[:, :, :cpt].reshape(B, T // CHUNK, 1, RWKV_DIM)
    pl1 = pl1[:, :, :cpt].reshape(B, T // CHUNK, 1, RWKV_DIM)
    yf, yb = _scan(at0, rt0, bt0, kt0, at1, rt1, bt1, kt1, v, pl0, pl1)
    o_att = _attention(q, ka, va, attn_sink[l])
    x1 = _out_proj(x, yf, yb, bonus, g, o_att, row(lnx_g[l]), row(lnx_b[l]), mean_blk,
                   w_out[l].astype(BF16))
    return _ffn(x1, row(ln2_g[l]), ffn_w_gate[l].astype(BF16), ffn_w_up[l].astype(BF16),
                ffn_conv_w[l], row(ffn_conv_b[l]), ffn_w_down[l].astype(BF16), row(lnf_g))
```

```python
import functools

import numpy as np
import jax
import jax.numpy as jnp
from jax import lax
from jax.experimental import pallas as pl
from jax.experimental.pallas import tpu as pltpu

F32 = jnp.float32
BF16 = jnp.bfloat16

D_MODEL = 1024
HEAD_DIM = 64
RWKV_DIM = 512
ATT_DIM = 512
ATT_HEADS = 8
KV_DIM = 128
LORA_DIM = 256
SHIFT_DIM = 3 * RWKV_DIM + LORA_DIM
PROJ_DIM = SHIFT_DIM + ATT_DIM + 2 * KV_DIM
WINDOW = 128
D_FF = 2816
CONV_WIDTH = 3
NORM_EPS = 1e-6
LNX_EPS = 64e-5
L2_EPS = 1e-12
MASK_VALUE = -1e30
LOG2E = float(np.log2(np.e))
NEG_DECAY_SCALE = float(-np.exp(-0.5) * np.log2(np.e))

LANES = 128
SUBLANES = 8
CHUNK = 64
SUB = 16
VMEM_LIMIT = 56 * 1024 * 1024

TM_IN = 512
TM_OUT = 512
TM_FFN = 512
FF_SPLITS = (0, 1536, D_FF)


def _dot(a, b):
    return jnp.dot(a.astype(BF16), b.astype(BF16), preferred_element_type=F32)


def _dot_nt(a, b):
    return lax.dot_general(a.astype(BF16), b.astype(BF16), (((1,), (1,)), ((), ())),
                           preferred_element_type=F32)


def _dot_tn(a, b):
    return lax.dot_general(a.astype(BF16), b.astype(BF16), (((0,), (0,)), ((), ())),
                           preferred_element_type=F32)


def _split2(x):
    hi = x.astype(BF16)
    lo = (x - hi.astype(F32)).astype(BF16)
    return hi, lo


def _rms_norm(x, g):
    return x * lax.rsqrt(jnp.mean(x * x, axis=-1, keepdims=True) + NORM_EPS) * g


def _in_proj_kernel(x_ref, xp_ref, xn_ref, ln1_ref, w_ref, muc_ref, mup_ref, mun_ref, w0_ref,
                    w2_ref, a0_ref, a2_ref, g2_ref, kk_ref, ka_ref, rk_ref, ones_ref, tri_ref,
                    at0_ref, rt0_ref, bt0_ref, kt0_ref, at1_ref, rt1_ref, bt1_ref, kt1_ref,
                    v_ref, pl0_ref, pl1_ref, g_ref, bonus_ref, q_ref, ka_o_ref, va_o_ref):
    i = pl.program_id(1)
    n_tiles = pl.num_programs(1)
    tm = x_ref.shape[1]
    ln1 = ln1_ref[...]
    x_ext = jnp.concatenate([jnp.where(i > 0, xp_ref[0], 0.0), x_ref[0],
                             jnp.where(i < n_tiles - 1, xn_ref[0], 0.0)], axis=0)
    h = _rms_norm(x_ext, ln1).astype(BF16)
    rows = tm + 2 * SUBLANES
    core = slice(SUBLANES, tm + SUBLANES)

    def shifted(c0, c1):
        p_ext = _dot(h, w_ref[:, c0:c1])
        p = p_ext[core]
        prev = pltpu.roll(p_ext, 1, axis=0)[core]
        nxt = pltpu.roll(p_ext, rows - 1, axis=0)[core]
        return p * muc_ref[:, c0:c1] + prev * mup_ref[:, c0:c1] + nxt * mun_ref[:, c0:c1]

    def seg_sum(x):
        return _dot(x, ones_ref[...])

    codes = shifted(3 * RWKV_DIM, SHIFT_DIM)
    c_di = codes[:, :LANES]
    c_g = codes[:, LANES:]
    th_hi, th_lo = _split2(jnp.tanh(c_di))
    th_cat = jnp.concatenate([th_hi, th_lo, th_hi], axis=1)
    a_vec = jax.nn.sigmoid(a0_ref[...] + _dot(c_di, a2_ref[...]))
    g_ref[0] = _dot(jax.nn.sigmoid(c_g), g2_ref[...])

    r = shifted(0, RWKV_DIM)
    k = shifted(RWKV_DIM, 2 * RWKV_DIM)
    v = shifted(2 * RWKV_DIM, 3 * RWKV_DIM)
    v_ref[0] = v.astype(BF16)

    kkr = k * kk_ref[...]
    n2 = seg_sum(kkr * kkr)
    kk = kkr * lax.rsqrt(jnp.maximum(n2, L2_EPS * L2_EPS))
    k2 = k * (1.0 + (a_vec - 1.0) * ka_ref[...])
    b_vec = kk * a_vec
    neg_kk = -kk
    bonus_ref[0] = seg_sum(r * k2 * rk_ref[...]) * v

    dir_outs = ((at0_ref, rt0_ref, bt0_ref, kt0_ref, pl0_ref),
                (at1_ref, rt1_ref, bt1_ref, kt1_ref, pl1_ref))
    for d, (at_ref, rt_ref, bt_ref, kt_ref, pl_ref) in enumerate(dir_outs):
        z = w0_ref[d:d + 1, :] + jnp.dot(th_cat, w2_ref[d], preferred_element_type=F32)
        lw = NEG_DECAY_SCALE / (1.0 + jnp.exp2(z * (-LOG2E)))
        lw_hi, lw_lo = _split2(lw)
        tri = tri_ref[d]
        parts = []
        for j in range(tm // CHUNK):
            cs = slice(j * CHUNK, (j + 1) * CHUNK)
            parts.append(jnp.dot(tri, jnp.concatenate([lw_hi[cs], lw_lo[cs]], axis=0),
                                 preferred_element_type=F32))
        ci = jnp.concatenate(parts, axis=0)
        end = 0 if d else CHUNK - 1
        tot = [parts[j][end:end + 1] for j in range(tm // CHUNK)]
        tot += [jnp.zeros_like(tot[0])] * (SUBLANES - len(tot))
        pl_ref[0, 0] = jnp.exp2(jnp.concatenate(tot, axis=0))
        e_inc = jnp.exp2(ci)
        e_exc = jnp.exp2(ci - lw)
        e_inv = 1.0 / e_inc
        at_ref[0] = (neg_kk * e_exc).astype(BF16)
        rt_ref[0] = (r * e_inc).astype(BF16)
        bt_ref[0] = (b_vec * e_inv).astype(BF16)
        kt_ref[0] = (k2 * e_inv).astype(BF16)

    att = _dot(h, w_ref[:, SHIFT_DIM:PROJ_DIM])[core]
    q_ref[0] = att[:, :ATT_DIM].astype(BF16)
    ka_o_ref[0] = att[:, ATT_DIM:ATT_DIM + KV_DIM].astype(BF16)
    va_o_ref[0] = att[:, ATT_DIM + KV_DIM:].astype(BF16)


def _in_proj(x, ln1_g, w_in, mu_cur, mu_prev, mu_next, decay_w0, w2_cat, iclr_a0, a2_pad, gate_g2,
             k_k, k_a, r_k, ones_blk, tri):
    B, T, D = x.shape
    tm = TM_IN
    nt = T // tm
    rows8 = tm // SUBLANES
    const = lambda shape: pl.BlockSpec(shape, lambda b, i: (0,) * len(shape))
    tok = lambda width: pl.BlockSpec((1, tm, width), lambda b, i: (b, i, 0))
    in_specs = [
        tok(D),
        pl.BlockSpec((1, SUBLANES, D), lambda b, i: (b, jnp.maximum(i * rows8 - 1, 0), 0)),
        pl.BlockSpec((1, SUBLANES, D), lambda b, i: (b, jnp.minimum((i + 1) * rows8, T // SUBLANES - 1), 0)),
        const((1, D)), const((D, PROJ_DIM)),
        const((1, SHIFT_DIM)), const((1, SHIFT_DIM)), const((1, SHIFT_DIM)),
        const((2, RWKV_DIM)), const((2, 3 * LANES, RWKV_DIM)),
        const((1, RWKV_DIM)), const((LANES, RWKV_DIM)), const((LANES, RWKV_DIM)),
        const((1, RWKV_DIM)), const((1, RWKV_DIM)), const((1, RWKV_DIM)),
        const((RWKV_DIM, RWKV_DIM)), const((2, CHUNK, 2 * CHUNK)),
    ]
    tok_bf = jax.ShapeDtypeStruct((B, T, RWKV_DIM), BF16)
    tok_f32 = jax.ShapeDtypeStruct((B, T, RWKV_DIM), F32)
    pl_shape = jax.ShapeDtypeStruct((B, nt, SUBLANES, RWKV_DIM), F32)
    pl_spec = pl.BlockSpec((1, 1, SUBLANES, RWKV_DIM), lambda b, i: (b, i, 0, 0))
    out_shape = [tok_bf] * 9 + [pl_shape, pl_shape, tok_f32, tok_f32,
                                jax.ShapeDtypeStruct((B, T, ATT_DIM), BF16),
                                jax.ShapeDtypeStruct((B, T, KV_DIM), BF16),
                                jax.ShapeDtypeStruct((B, T, KV_DIM), BF16)]
    out_specs = [tok(RWKV_DIM)] * 9 + [pl_spec, pl_spec, tok(RWKV_DIM), tok(RWKV_DIM),
                                       tok(ATT_DIM), tok(KV_DIM), tok(KV_DIM)]
    return pl.pallas_call(
        _in_proj_kernel, grid=(B, nt), in_specs=in_specs, out_specs=out_specs, out_shape=out_shape,
        compiler_params=pltpu.CompilerParams(dimension_semantics=("parallel", "parallel"),
                                             vmem_limit_bytes=VMEM_LIMIT),
        name="in_proj",
    )(x, x, x, ln1_g, w_in, mu_cur, mu_prev, mu_next, decay_w0, w2_cat, iclr_a0, a2_pad, gate_g2,
      k_k, k_a, r_k, ones_blk, tri)


def _pair_chunks(items, sub_blk, eye, lane0, bd_mask):
    n = range(len(items))
    at, rt, bt, kt, v, p_last, s_prev, strict, incl = zip(*items)

    def bd(x):
        x = x.astype(BF16)
        zero = jnp.zeros_like(x)
        return jnp.concatenate([jnp.where(lane0, x, zero), jnp.where(lane0, zero, x)], axis=0)

    def pmm(x, y):
        return _dot(x, bd(y))

    sc = [_dot_nt(jnp.concatenate([at[i], rt[i]], axis=0),
                  jnp.concatenate([bd(bt[i]), bd(kt[i])], axis=0)) for i in n]
    a_ab = [jnp.where(strict[i], sc[i][:CHUNK, :LANES], 0.0) for i in n]
    a_ak = [jnp.where(strict[i], sc[i][:CHUNK, LANES:], 0.0) for i in n]
    a_rb = [jnp.where(incl[i], sc[i][CHUNK:, :LANES], 0.0) for i in n]
    a_rk = [jnp.where(incl[i], sc[i][CHUNK:, LANES:], 0.0) for i in n]

    a_d = [jnp.where(sub_blk, a_ab[i], 0.0) for i in n]
    a_o = [a_ab[i] - a_d[i] for i in n]
    a2 = [pmm(a_d[i], a_d[i]) for i in n]
    xy = [pmm(jnp.concatenate([a_ak[i], a_rk[i]], axis=0), v[i]) for i in n]
    x1 = [xy[i][:CHUNK] for i in n]
    yk = [xy[i][CHUNK:] for i in n]
    t1 = [eye + a_d[i] for i in n]
    st = [pmm(jnp.concatenate([a2[i], t1[i]], axis=0), a2[i]) for i in n]
    a4 = [st[i][:CHUNK] for i in n]
    t1 = [t1[i] + st[i][CHUNK:] for i in n]
    st = [pmm(jnp.concatenate([a4[i], t1[i]], axis=0), a4[i]) for i in n]
    a8 = [st[i][:CHUNK] for i in n]
    t1 = [t1[i] + st[i][CHUNK:] for i in n]
    t_d = [t1[i] + pmm(t1[i], a8[i]) for i in n]
    mx = [_dot(t_d[i], jnp.concatenate([bd(a_o[i]), bd(at[i]), bd(x1[i])], axis=1)) for i in n]
    m1 = [mx[i][:, :LANES] for i in n]
    xp = [mx[i][:, LANES:] for i in n]
    m2 = [pmm(m1[i], m1[i]) for i in n]
    q = [eye + m1[i] + m2[i] + pmm(m1[i], m2[i]) for i in n]
    wu = [_dot(q[i], jnp.concatenate([bd(xp[i][:, :LANES]), bd(xp[i][:, LANES:])], axis=1))
          for i in n]
    hs = [_dot_nt(jnp.concatenate([wu[i][:, :LANES].astype(BF16), rt[i]], axis=0), s_prev[i])
          for i in n]
    u = [hs[i][:CHUNK] + wu[i][:, LANES:] for i in n]
    y = [hs[i][CHUNK:] + pmm(a_rb[i], u[i]) + yk[i] for i in n]
    upd = [_dot_tn(jnp.concatenate([u[i].astype(BF16), v[i]], axis=0),
                   jnp.concatenate([bt[i], kt[i]], axis=0)) for i in n]
    s_new = [(s_prev[i] + jnp.where(bd_mask, upd[i], 0.0)) * p_last[i] for i in n]
    return list(zip(y, s_new))


def _scan_kernel(at0_ref, rt0_ref, bt0_ref, kt0_ref, v0_ref, pl0_ref,
                 at1_ref, rt1_ref, bt1_ref, kt1_ref, v1_ref, pl1_ref,
                 yf_ref, yb_ref, s_ref):
    c = pl.program_id(0)

    @pl.when(c == 0)
    def _():
        s_ref[...] = jnp.zeros_like(s_ref)

    n_batch = v0_ref.shape[0]
    ri = lax.broadcasted_iota(jnp.int32, (CHUNK, LANES), 0)
    ci = lax.broadcasted_iota(jnp.int32, (CHUNK, LANES), 1)
    cj = jnp.where(ci >= CHUNK, ci - CHUNK, ci)
    lane0 = ci < CHUNK
    eye = jnp.where(ri == cj, 1.0, 0.0).astype(F32)
    sub_blk = (ri // SUB) == (cj // SUB)
    r2 = lax.broadcasted_iota(jnp.int32, (LANES, LANES), 0)
    c2 = lax.broadcasted_iota(jnp.int32, (LANES, LANES), 1)
    bd_mask = (r2 >= CHUNK) == (c2 >= CHUNK)
    dirs = ((at0_ref, rt0_ref, bt0_ref, kt0_ref, v0_ref, pl0_ref, ri > cj, ri >= cj),
            (at1_ref, rt1_ref, bt1_ref, kt1_ref, v1_ref, pl1_ref, ri < cj, ri <= cj))
    n_pairs = RWKV_DIM // LANES
    items = []
    for b in range(n_batch):
        for d, (at_ref, rt_ref, bt_ref, kt_ref, v_ref, pl_ref, strict, incl) in enumerate(dirs):
            for p in range(n_pairs):
                sl = slice(p * LANES, (p + 1) * LANES)
                items.append((at_ref[b, :, sl], rt_ref[b, :, sl], bt_ref[b, :, sl], kt_ref[b, :, sl],
                              v_ref[b, :, sl], pl_ref[b, 0, :, sl], s_ref[b, d, p], strict, incl))
    outs = _pair_chunks(items, sub_blk, eye, lane0, bd_mask)
    for b in range(n_batch):
        for d, y_ref in enumerate((yf_ref, yb_ref)):
            base = (b * 2 + d) * n_pairs
            y_ref[b] = jnp.concatenate([outs[base + p][0] for p in range(n_pairs)], axis=1)
            for p in range(n_pairs):
                s_ref[b, d, p] = outs[base + p][1]


def _scan(at0, rt0, bt0, kt0, at1, rt1, bt1, kt1, v, pl0, pl1):
    B, T, C = v.shape
    nc = T // CHUNK
    fwd = pl.BlockSpec((B, CHUNK, C), lambda c: (0, c, 0))
    bwd = pl.BlockSpec((B, CHUNK, C), lambda c: (0, nc - 1 - c, 0))
    pl_f = pl.BlockSpec((B, 1, 1, C), lambda c: (0, c, 0, 0))
    pl_b = pl.BlockSpec((B, 1, 1, C), lambda c: (0, nc - 1 - c, 0, 0))
    y_shape = jax.ShapeDtypeStruct((B, T, C), F32)
    return pl.pallas_call(
        _scan_kernel, grid=(nc,),
        in_specs=[fwd, fwd, fwd, fwd, fwd, pl_f, bwd, bwd, bwd, bwd, bwd, pl_b],
        out_specs=[fwd, bwd], out_shape=[y_shape, y_shape],
        scratch_shapes=[pltpu.VMEM((B, 2, C // LANES, LANES, LANES), F32)],
        compiler_params=pltpu.CompilerParams(dimension_semantics=("arbitrary",),
                                             vmem_limit_bytes=VMEM_LIMIT),
        name="rwkv_scan",
    )(at0, rt0, bt0, kt0, v, pl0, at1, rt1, bt1, kt1, v, pl1)


def _attn_kernel(sink_ref, bias_ref, q_ref, kp_ref, kc_ref, kn_ref, vp_ref, vc_ref, vn_ref, o_ref):
    blk = q_ref.shape[1]
    nk = 3 * blk
    k_all = jnp.concatenate([kp_ref[0], kc_ref[0], kn_ref[0]], axis=0)
    v_all = jnp.concatenate([vp_ref[0], vc_ref[0], vn_ref[0]], axis=0)

    def swap_halves(x):
        return jnp.concatenate([x[:, HEAD_DIM:], x[:, :HEAD_DIM]], axis=1)

    lane0 = lax.broadcasted_iota(jnp.int32, (nk, LANES), 1) < HEAD_DIM

    def variants(x):
        xs = swap_halves(x)
        zero = jnp.zeros_like(x)
        return ((jnp.where(lane0, x, zero), jnp.where(lane0, zero, xs)),
                (jnp.where(lane0, xs, zero), jnp.where(lane0, zero, x)))

    k_var = variants(k_all)
    v_var = variants(v_all)

    group = ATT_HEADS // (KV_DIM // HEAD_DIM)

    heads = range(ATT_HEADS)
    q_pairs = [q_ref[0, :, j * LANES:(j + 1) * LANES] * jnp.asarray(HEAD_DIM ** -0.5, BF16)
               for j in range(ATT_DIM // LANES)]
    s = [_dot_nt(q_pairs[h // 2], k_var[h // group][h % 2]) + bias_ref[0, h] for h in heads]
    m = [jnp.maximum(jnp.max(s[h], axis=-1, keepdims=True), sink_ref[h]) for h in heads]
    p = [jnp.exp(s[h] - m[h]) for h in heads]
    den = [jnp.sum(p[h], axis=-1, keepdims=True) + jnp.exp(sink_ref[h] - m[h]) for h in heads]
    o = [_dot(p[h], v_var[h // group][h % 2]) * (1.0 / den[h]) for h in heads]
    o_ref[0] = jnp.concatenate([o[2 * j] + o[2 * j + 1] for j in range(ATT_DIM // LANES)],
                               axis=1).astype(o_ref.dtype)


def _attn_bias(blk):
    qi = np.arange(blk)[:, None]
    kj = np.arange(3 * blk)[None, :]
    dist = np.abs(kj - blk - qi)
    slopes = 2.0 ** (-8.0 * np.arange(1, ATT_HEADS + 1, dtype=np.float32) / ATT_HEADS)
    alibi = -slopes[:, None, None].astype(np.float32) * dist[None].astype(np.float32)
    out = []
    for has_prev, has_next in ((False, True), (True, True), (True, False)):
        valid = (dist <= WINDOW) & (has_prev | (kj >= blk)) & (has_next | (kj < 2 * blk))
        out.append(np.where(valid[None], alibi, np.float32(MASK_VALUE)))
    return jnp.asarray(np.stack(out), F32)


def _attention(q, ka, va, sink):
    B, T, _ = q.shape
    blk = WINDOW
    nb = T // blk
    cur = lambda width: pl.BlockSpec((1, blk, width), lambda b, n: (b, n, 0))
    prev = pl.BlockSpec((1, blk, KV_DIM), lambda b, n: (b, jnp.maximum(n - 1, 0), 0))
    nxt = pl.BlockSpec((1, blk, KV_DIM), lambda b, n: (b, jnp.minimum(n + 1, nb - 1), 0))
    bias = pl.BlockSpec((1, ATT_HEADS, blk, 3 * blk),
                        lambda b, n: (jnp.where(n == 0, 0, jnp.where(n == nb - 1, 2, 1)), 0, 0, 0))
    return pl.pallas_call(
        _attn_kernel, grid=(B, nb),
        in_specs=[pl.BlockSpec(memory_space=pltpu.SMEM), bias, cur(ATT_DIM),
                  prev, cur(KV_DIM), nxt, prev, cur(KV_DIM), nxt],
        out_specs=cur(ATT_DIM), out_shape=jax.ShapeDtypeStruct((B, T, ATT_DIM), BF16),
        compiler_params=pltpu.CompilerParams(dimension_semantics=("parallel", "parallel"),
                                             vmem_limit_bytes=VMEM_LIMIT),
        name="band_attn",
    )(sink, _attn_bias(blk), q, ka, ka, ka, va, va, va)


def _out_proj_kernel(x_ref, yf_ref, yb_ref, bonus_ref, g_ref, oatt_ref, lg_ref, lb_ref,
                     mean_ref, w_ref, o_ref):
    def seg_mean(t):
        return _dot(t, mean_ref[...])

    y = yf_ref[0] + yb_ref[0]
    d = y - seg_mean(y)
    var = seg_mean(d * d)
    yn = d * lax.rsqrt(var + LNX_EPS) * lg_ref[...] + lb_ref[...]
    o_rwkv = (yn + bonus_ref[0]) * g_ref[0]
    mix = _dot(o_rwkv, w_ref[:RWKV_DIM, :]) + _dot(oatt_ref[0], w_ref[RWKV_DIM:, :])
    o_ref[0] = x_ref[0] + mix


def _out_proj(x, yf, yb, bonus, g, o_att, lnx_g, lnx_b, mean_blk, w_out):
    B, T, D = x.shape
    tm = TM_OUT
    tok = lambda width: pl.BlockSpec((1, tm, width), lambda b, i: (b, i, 0))
    const = lambda shape: pl.BlockSpec(shape, lambda b, i: (0,) * len(shape))
    return pl.pallas_call(
        _out_proj_kernel, grid=(B, T // tm),
        in_specs=[tok(D), tok(RWKV_DIM), tok(RWKV_DIM), tok(RWKV_DIM), tok(RWKV_DIM), tok(ATT_DIM),
                  const((1, RWKV_DIM)), const((1, RWKV_DIM)), const((RWKV_DIM, RWKV_DIM)),
                  const((D, D))],
        out_specs=tok(D), out_shape=jax.ShapeDtypeStruct((B, T, D), F32),
        compiler_params=pltpu.CompilerParams(dimension_semantics=("parallel", "parallel"),
                                             vmem_limit_bytes=VMEM_LIMIT),
        name="out_proj",
    )(x, yf, yb, bonus, g, o_att, lnx_g, lnx_b, mean_blk, w_out)


def _ffn_kernel(x_ref, xp_ref, xn_ref, ln2_ref, wg_ref, wu_ref, cw_ref, cb_ref, wd_ref, lnf_ref,
                o_ref):
    i = pl.program_id(1)
    n_tiles = pl.num_programs(1)
    tm = x_ref.shape[1]
    x = x_ref[0]
    x_ext = jnp.concatenate([jnp.where(i > 0, xp_ref[0], 0.0), x,
                             jnp.where(i < n_tiles - 1, xn_ref[0], 0.0)], axis=0)
    h = _rms_norm(x_ext, ln2_ref[...]).astype(BF16)
    rows = tm + 2 * SUBLANES
    core = slice(SUBLANES, tm + SUBLANES)
    acc = jnp.zeros((tm, D_MODEL), F32)
    for c0, c1 in zip(FF_SPLITS[:-1], FF_SPLITS[1:]):
        cs = slice(c0, c1)
        gp_ext = _dot(h, wg_ref[:, cs])
        prev = pltpu.roll(gp_ext, 1, axis=0)[core]
        nxt = pltpu.roll(gp_ext, rows - 1, axis=0)[core]
        gate = (prev * cw_ref[0:1, cs] + gp_ext[core] * cw_ref[1:2, cs] + nxt * cw_ref[2:3, cs]
                + cb_ref[:, cs])
        act = 0.5 * gate * (1.0 + lax.erf(gate * float(1.0 / np.sqrt(2.0))))
        up = _dot(h, wu_ref[:, cs])[core]
        acc = acc + _dot(act * up, wd_ref[cs, :])
    o_ref[0] = _rms_norm(x + acc, lnf_ref[...])


def _ffn(x1, ln2_g, wg, wu, conv_w, conv_b, wd, lnf_g):
    B, T, D = x1.shape
    tm = TM_FFN
    rows8 = tm // SUBLANES
    tok = pl.BlockSpec((1, tm, D), lambda b, i: (b, i, 0))
    const = lambda shape: pl.BlockSpec(shape, lambda b, i: (0,) * len(shape))
    resident = lambda shape: pl.BlockSpec(shape, lambda b, i: (0,) * len(shape),
                                          pipeline_mode=pl.Buffered(1))
    return pl.pallas_call(
        _ffn_kernel, grid=(B, T // tm),
        in_specs=[tok,
                  pl.BlockSpec((1, SUBLANES, D), lambda b, i: (b, jnp.maximum(i * rows8 - 1, 0), 0)),
                  pl.BlockSpec((1, SUBLANES, D),
                               lambda b, i: (b, jnp.minimum((i + 1) * rows8, T // SUBLANES - 1), 0)),
                  const((1, D)), resident((D, D_FF)), resident((D, D_FF)),
                  const((CONV_WIDTH, D_FF)), const((1, D_FF)), resident((D_FF, D)), const((1, D))],
        out_specs=tok, out_shape=jax.ShapeDtypeStruct((B, T, D), F32),
        compiler_params=pltpu.CompilerParams(dimension_semantics=("parallel", "parallel"),
                                             vmem_limit_bytes=VMEM_LIMIT),
        name="conv_ffn",
    )(x1, x1, x1, ln2_g, wg, wu, conv_w, conv_b, wd, lnf_g)


def _constants():
    idx = np.arange(RWKV_DIM)
    same_head = (idx[:, None] // HEAD_DIM) == (idx[None, :] // HEAD_DIM)
    t = np.arange(CHUNK)
    tri = np.stack([t[:, None] >= t[None, :], t[:, None] <= t[None, :]]).astype(np.float32)
    return (jnp.asarray(same_head.astype(np.float32), BF16),
            jnp.asarray(same_head.astype(np.float32) / HEAD_DIM, BF16),
            jnp.asarray(np.concatenate([tri, tri], axis=2), BF16))


def kernel(x, ln1_g, w_in, shift_mu_prev, shift_mu_next, decay_w0, decay_w2, iclr_a0, iclr_a2,
           gate_g2, k_k, k_a, r_k, lnx_g, lnx_b, attn_sink, w_out, ln2_g, ffn_w_gate, ffn_w_up,
           ffn_conv_w, ffn_conv_b, ffn_w_down, lnf_g):
    B, T, _ = x.shape
    assert w_in.shape[0] == 1, "single-layer block"
    l = 0
    ones_blk, mean_blk, tri = _constants()
    row = lambda a: a.reshape(1, -1)
    w2 = decay_w2[l]
    w2_pad = jnp.concatenate([w2, jnp.zeros_like(w2)], axis=1)
    w2hi = w2_pad.astype(BF16)
    w2lo = (w2_pad - w2hi.astype(F32)).astype(BF16)
    w2_cat = jnp.concatenate([w2hi, w2hi, w2lo], axis=1)
    mu_p, mu_n = shift_mu_prev[l], shift_mu_next[l]
    a2_pad = jnp.concatenate([jnp.zeros_like(iclr_a2[l]), iclr_a2[l]], axis=0).astype(BF16)
    (at0, rt0, bt0, kt0, at1, rt1, bt1, kt1, v, pl0, pl1, g, bonus, q, ka, va) = _in_proj(
        x, row(ln1_g[l]), w_in[l].astype(BF16), row(1.0 - mu_p - mu_n), row(mu_p), row(mu_n),
        decay_w0[l], w2_cat, row(iclr_a0[l]), a2_pad, gate_g2[l].astype(BF16),
        row(k_k[l]), row(k_a[l]), row(r_k[l]), ones_blk, tri)
    cpt = TM_IN // CHUNK
    pl0 = pl0[:, :, :cpt].reshape(B, T // CHUNK, 1, RWKV_DIM)
    pl1 = pl1[:, :, :cpt].reshape(B, T // CHUNK, 1, RWKV_DIM)
    yf, yb = _scan(at0, rt0, bt0, kt0, at1, rt1, bt1, kt1, v, pl0, pl1)
    o_att = _attention(q, ka, va, attn_sink[l])
    x1 = _out_proj(x, yf, yb, bonus, g, o_att, row(lnx_g[l]), row(lnx_b[l]), mean_blk,
                   w_out[l].astype(BF16))
    return _ffn(x1, row(ln2_g[l]), ffn_w_gate[l].astype(BF16), ffn_w_up[l].astype(BF16),
                ffn_conv_w[l], row(ffn_conv_b[l]), ffn_w_down[l].astype(BF16), row(lnf_g))
```

```python
import functools

import numpy as np
import jax
import jax.numpy as jnp
from jax import lax
from jax.experimental import pallas as pl
from jax.experimental.pallas import tpu as pltpu

F32 = jnp.float32
BF16 = jnp.bfloat16

D_MODEL = 1024
HEAD_DIM = 64
RWKV_DIM = 512
ATT_DIM = 512
ATT_HEADS = 8
KV_DIM = 128
LORA_DIM = 256
SHIFT_DIM = 3 * RWKV_DIM + LORA_DIM
PROJ_DIM = SHIFT_DIM + ATT_DIM + 2 * KV_DIM
WINDOW = 128
D_FF = 2816
CONV_WIDTH = 3
NORM_EPS = 1e-6
LNX_EPS = 64e-5
L2_EPS = 1e-12
MASK_VALUE = -1e30
PROJ_GROUPS = ((3 * RWKV_DIM, SHIFT_DIM), (RWKV_DIM, 2 * RWKV_DIM), (0, RWKV_DIM),
               (2 * RWKV_DIM, 3 * RWKV_DIM), (SHIFT_DIM, PROJ_DIM))
LOG2E = float(np.log2(np.e))
NEG_DECAY_SCALE = float(-np.exp(-0.5) * np.log2(np.e))

LANES = 128
SUBLANES = 8
CHUNK = 64
SUB = 16
VMEM_LIMIT = 56 * 1024 * 1024

TM_IN = 512
TM_OUT = 512
TM_FFN = 512
ATT_QBLOCKS = 2
FF_SPLITS = (0, 1536, D_FF)


def _dot(a, b):
    return jnp.dot(a.astype(BF16), b.astype(BF16), preferred_element_type=F32)


def _dot_nt(a, b):
    return lax.dot_general(a.astype(BF16), b.astype(BF16), (((1,), (1,)), ((), ())),
                           preferred_element_type=F32)


def _dot_tn(a, b):
    return lax.dot_general(a.astype(BF16), b.astype(BF16), (((0,), (0,)), ((), ())),
                           preferred_element_type=F32)


def _split2(x):
    hi = x.astype(BF16)
    lo = (x - hi.astype(F32)).astype(BF16)
    return hi, lo


def _rms_norm(x, g):
    return x * lax.rsqrt(jnp.mean(x * x, axis=-1, keepdims=True) + NORM_EPS) * g


def _in_proj_kernel(x_ref, xp_ref, xn_ref, ln1_ref, w_ref, muc_ref, mup_ref, mun_ref, w0_ref,
                    w2_ref, a0_ref, a2_ref, g2_ref, kk_ref, ka_ref, rk_ref, ones_ref, tri_ref,
                    at0_ref, rt0_ref, bt0_ref, kt0_ref, at1_ref, rt1_ref, bt1_ref, kt1_ref,
                    v_ref, pl0_ref, pl1_ref, g_ref, bonus_ref, q_ref, ka_o_ref, va_o_ref,
                    pa_ref, pb_ref, *, tiles_per_seq):
    s = pl.program_id(0)
    n_tiles = pl.num_programs(0) - 1

    @pl.when(s == 0)
    def _():
        pb_ref[...] = jnp.zeros_like(pb_ref)

    def step(dst_ref, src_ref):
        project = _in_proj_project(s, n_tiles, tiles_per_seq, x_ref, xp_ref, xn_ref, ln1_ref, w_ref,
                                   dst_ref)
        prepare = _in_proj_prepare(src_ref, muc_ref, mup_ref, mun_ref, w0_ref, w2_ref, a0_ref, a2_ref,
                                   g2_ref, kk_ref, ka_ref, rk_ref, ones_ref, tri_ref,
                                   at0_ref, rt0_ref, bt0_ref, kt0_ref, at1_ref, rt1_ref, bt1_ref, kt1_ref,
                                   v_ref, pl0_ref, pl1_ref, g_ref, bonus_ref, q_ref, ka_o_ref, va_o_ref)
        live = [project, prepare]
        while live:
            live = [g for g in live if next(g, StopIteration) is not StopIteration]

    @pl.when(s % 2 == 0)
    def _():
        step(pa_ref, pb_ref)

    @pl.when(s % 2 == 1)
    def _():
        step(pb_ref, pa_ref)


def _in_proj_project(s, n_tiles, tiles_per_seq, x_ref, xp_ref, xn_ref, ln1_ref, w_ref, p_ref):
    i = lax.rem(jnp.minimum(s, n_tiles - 1), tiles_per_seq)
    x_ext = jnp.concatenate([jnp.where(i > 0, xp_ref[0], 0.0), x_ref[0],
                             jnp.where(i < tiles_per_seq - 1, xn_ref[0], 0.0)], axis=0)
    h = _rms_norm(x_ext, ln1_ref[...]).astype(BF16)
    for c0, c1 in PROJ_GROUPS:
        p_ref[:, c0:c1] = _dot(h, w_ref[:, c0:c1])
        yield


def _in_proj_prepare(p_ref, muc_ref, mup_ref, mun_ref, w0_ref, w2_ref, a0_ref, a2_ref, g2_ref,
                     kk_ref, ka_ref, rk_ref, ones_ref, tri_ref,
                     at0_ref, rt0_ref, bt0_ref, kt0_ref, at1_ref, rt1_ref, bt1_ref, kt1_ref,
                     v_ref, pl0_ref, pl1_ref, g_ref, bonus_ref, q_ref, ka_o_ref, va_o_ref):
    rows = p_ref.shape[0]
    tm = rows - 2 * SUBLANES
    core = slice(SUBLANES, tm + SUBLANES)

    def shifted(c0, c1):
        p_ext = p_ref[:, c0:c1]
        p = p_ext[core]
        prev = pltpu.roll(p_ext, 1, axis=0)[core]
        nxt = pltpu.roll(p_ext, rows - 1, axis=0)[core]
        return p * muc_ref[:, c0:c1] + prev * mup_ref[:, c0:c1] + nxt * mun_ref[:, c0:c1]

    def seg_sum(x):
        return _dot(x, ones_ref[...])

    codes = shifted(3 * RWKV_DIM, SHIFT_DIM)
    c_di = codes[:, :LANES]
    c_g = codes[:, LANES:]
    th_hi, th_lo = _split2(jnp.tanh(c_di))
    th_cat = jnp.concatenate([th_hi, th_lo, th_hi], axis=1)
    a_vec = jax.nn.sigmoid(a0_ref[...] + _dot(c_di, a2_ref[...]))
    g_ref[0] = _dot(jax.nn.sigmoid(c_g), g2_ref[...]).astype(BF16)
    yield

    k = shifted(RWKV_DIM, 2 * RWKV_DIM)
    kkr = k * kk_ref[...]
    n2 = seg_sum(kkr * kkr)
    kk = kkr * lax.rsqrt(jnp.maximum(n2, L2_EPS * L2_EPS))
    k2 = k * (1.0 + (a_vec - 1.0) * ka_ref[...])
    b_vec = kk * a_vec
    neg_kk = -kk
    yield

    r = shifted(0, RWKV_DIM)
    v = shifted(2 * RWKV_DIM, 3 * RWKV_DIM)
    v_ref[0] = v.astype(BF16)
    bonus_ref[0] = (seg_sum(r * k2 * rk_ref[...]) * v).astype(BF16)
    yield

    dir_outs = ((at0_ref, rt0_ref, bt0_ref, kt0_ref, pl0_ref),
                (at1_ref, rt1_ref, bt1_ref, kt1_ref, pl1_ref))
    for d, (at_ref, rt_ref, bt_ref, kt_ref, pl_ref) in enumerate(dir_outs):
        z = w0_ref[d:d + 1, :] + jnp.dot(th_cat, w2_ref[d], preferred_element_type=F32)
        lw = NEG_DECAY_SCALE / (1.0 + jnp.exp2(z * (-LOG2E)))
        lw_hi, lw_lo = _split2(lw)
        tri = tri_ref[d]
        parts = []
        for j in range(tm // CHUNK):
            cs = slice(j * CHUNK, (j + 1) * CHUNK)
            parts.append(jnp.dot(tri, jnp.concatenate([lw_hi[cs], lw_lo[cs]], axis=0),
                                 preferred_element_type=F32))
        ci = jnp.concatenate(parts, axis=0)
        end = 0 if d else CHUNK - 1
        tot = [parts[j][end:end + 1] for j in range(tm // CHUNK)]
        tot += [jnp.zeros_like(tot[0])] * (SUBLANES - len(tot))
        pl_ref[0, 0] = jnp.exp2(jnp.concatenate(tot, axis=0))
        yield
        e_inc = jnp.exp2(ci)
        e_exc = jnp.exp2(ci - lw)
        e_inv = 1.0 / e_inc
        at_ref[0] = (neg_kk * e_exc).astype(BF16)
        rt_ref[0] = (r * e_inc).astype(BF16)
        bt_ref[0] = (b_vec * e_inv).astype(BF16)
        kt_ref[0] = (k2 * e_inv).astype(BF16)
        yield

    q_ref[0] = p_ref[core, SHIFT_DIM:SHIFT_DIM + ATT_DIM].astype(BF16)
    ka_o_ref[0] = p_ref[core, SHIFT_DIM + ATT_DIM:SHIFT_DIM + ATT_DIM + KV_DIM].astype(BF16)
    va_o_ref[0] = p_ref[core, SHIFT_DIM + ATT_DIM + KV_DIM:].astype(BF16)


def _in_proj(x, ln1_g, w_in, mu_cur, mu_prev, mu_next, decay_w0, w2_cat, iclr_a0, a2_pad, gate_g2,
             k_k, k_a, r_k, ones_blk, tri):
    B, T, D = x.shape
    tm = TM_IN
    nt = T // tm
    n_tiles = B * nt
    rows8 = tm // SUBLANES
    const = lambda shape: pl.BlockSpec(shape, lambda s: (0,) * len(shape))

    def a_tile(s):
        t = jnp.minimum(s, n_tiles - 1)
        return t // nt, lax.rem(t, nt)

    def b_tile(s):
        t = jnp.maximum(s - 1, 0)
        return t // nt, lax.rem(t, nt)

    def halo_prev(s):
        b, i = a_tile(s)
        return b, jnp.maximum(i * rows8 - 1, 0), 0

    def halo_next(s):
        b, i = a_tile(s)
        return b, jnp.minimum((i + 1) * rows8, T // SUBLANES - 1), 0

    tok = lambda width: pl.BlockSpec((1, tm, width), lambda s: (*b_tile(s), 0))
    in_specs = [
        pl.BlockSpec((1, tm, D), lambda s: (*a_tile(s), 0)),
        pl.BlockSpec((1, SUBLANES, D), halo_prev),
        pl.BlockSpec((1, SUBLANES, D), halo_next),
        const((1, D)), const((D, PROJ_DIM)),
        const((1, SHIFT_DIM)), const((1, SHIFT_DIM)), const((1, SHIFT_DIM)),
        const((2, RWKV_DIM)), const((2, 3 * LANES, RWKV_DIM)),
        const((1, RWKV_DIM)), const((LANES, RWKV_DIM)), const((LANES, RWKV_DIM)),
        const((1, RWKV_DIM)), const((1, RWKV_DIM)), const((1, RWKV_DIM)),
        const((RWKV_DIM, RWKV_DIM)), const((2, CHUNK, 2 * CHUNK)),
    ]
    tok_bf = jax.ShapeDtypeStruct((B, T, RWKV_DIM), BF16)
    pl_shape = jax.ShapeDtypeStruct((B, nt, SUBLANES, RWKV_DIM), F32)
    pl_spec = pl.BlockSpec((1, 1, SUBLANES, RWKV_DIM), lambda s: (*b_tile(s), 0, 0))
    out_shape = [tok_bf] * 9 + [pl_shape, pl_shape, tok_bf, tok_bf,
                                jax.ShapeDtypeStruct((B, T, ATT_DIM), BF16),
                                jax.ShapeDtypeStruct((B, T, KV_DIM), BF16),
                                jax.ShapeDtypeStruct((B, T, KV_DIM), BF16)]
    out_specs = [tok(RWKV_DIM)] * 9 + [pl_spec, pl_spec, tok(RWKV_DIM), tok(RWKV_DIM),
                                       tok(ATT_DIM), tok(KV_DIM), tok(KV_DIM)]
    p_buf = pltpu.VMEM((tm + 2 * SUBLANES, PROJ_DIM), F32)
    return pl.pallas_call(
        functools.partial(_in_proj_kernel, tiles_per_seq=nt),
        grid=(n_tiles + 1,), in_specs=in_specs, out_specs=out_specs, out_shape=out_shape,
        scratch_shapes=[p_buf, p_buf],
        compiler_params=pltpu.CompilerParams(dimension_semantics=("arbitrary",),
                                             vmem_limit_bytes=VMEM_LIMIT),
        name="in_proj",
    )(x, x, x, ln1_g, w_in, mu_cur, mu_prev, mu_next, decay_w0, w2_cat, iclr_a0, a2_pad, gate_g2,
      k_k, k_a, r_k, ones_blk, tri)


def _pair_chunks(items, sub_blk, eye, lane0, bd_mask):
    n = range(len(items))
    at, rt, bt, kt, v, p_last, s_prev, strict, incl = zip(*items)

    def bd(x):
        x = x.astype(BF16)
        zero = jnp.zeros_like(x)
        return jnp.concatenate([jnp.where(lane0, x, zero), jnp.where(lane0, zero, x)], axis=0)

    def pmm(x, y):
        return _dot(x, bd(y))

    sc = [_dot_nt(jnp.concatenate([at[i], rt[i]], axis=0),
                  jnp.concatenate([bd(bt[i]), bd(kt[i])], axis=0)) for i in n]
    a_ab = [jnp.where(strict[i], sc[i][:CHUNK, :LANES], 0.0) for i in n]
    a_ak = [jnp.where(strict[i], sc[i][:CHUNK, LANES:], 0.0) for i in n]
    a_rb = [jnp.where(incl[i], sc[i][CHUNK:, :LANES], 0.0) for i in n]
    a_rk = [jnp.where(incl[i], sc[i][CHUNK:, LANES:], 0.0) for i in n]

    a_d = [jnp.where(sub_blk, a_ab[i], 0.0) for i in n]
    a_o = [a_ab[i] - a_d[i] for i in n]
    a2 = [pmm(a_d[i], a_d[i]) for i in n]
    xy = [pmm(jnp.concatenate([a_ak[i], a_rk[i]], axis=0), v[i]) for i in n]
    x1 = [xy[i][:CHUNK] for i in n]
    yk = [xy[i][CHUNK:] for i in n]
    t1 = [eye + a_d[i] for i in n]
    st = [pmm(jnp.concatenate([a2[i], t1[i]], axis=0), a2[i]) for i in n]
    a4 = [st[i][:CHUNK] for i in n]
    t1 = [t1[i] + st[i][CHUNK:] for i in n]
    st = [pmm(jnp.concatenate([a4[i], t1[i]], axis=0), a4[i]) for i in n]
    a8 = [st[i][:CHUNK] for i in n]
    t1 = [t1[i] + st[i][CHUNK:] for i in n]
    t_d = [t1[i] + pmm(t1[i], a8[i]) for i in n]
    mx = [_dot(t_d[i], jnp.concatenate([bd(a_o[i]), bd(at[i]), bd(x1[i])], axis=1)) for i in n]
    m1 = [mx[i][:, :LANES] for i in n]
    xp = [mx[i][:, LANES:] for i in n]
    m2 = [pmm(m1[i], m1[i]) for i in n]
    q = [eye + m1[i] + m2[i] + pmm(m1[i], m2[i]) for i in n]
    wu = [_dot(q[i], jnp.concatenate([bd(xp[i][:, :LANES]), bd(xp[i][:, LANES:])], axis=1))
          for i in n]
    hs = [_dot_nt(jnp.concatenate([wu[i][:, :LANES].astype(BF16), rt[i]], axis=0), s_prev[i])
          for i in n]
    u = [hs[i][:CHUNK] + wu[i][:, LANES:] for i in n]
    y = [hs[i][CHUNK:] + pmm(a_rb[i], u[i]) + yk[i] for i in n]
    upd = [_dot_tn(jnp.concatenate([u[i].astype(BF16), v[i]], axis=0),
                   jnp.concatenate([bt[i], kt[i]], axis=0)) for i in n]
    s_new = [(s_prev[i] + jnp.where(bd_mask, upd[i], 0.0)) * p_last[i] for i in n]
    return list(zip(y, s_new))


def _scan_kernel(at0_ref, rt0_ref, bt0_ref, kt0_ref, v0_ref, pl0_ref,
                 at1_ref, rt1_ref, bt1_ref, kt1_ref, v1_ref, pl1_ref,
                 yf_ref, yb_ref, s_ref):
    c = pl.program_id(0)

    @pl.when(c == 0)
    def _():
        s_ref[...] = jnp.zeros_like(s_ref)

    n_batch = v0_ref.shape[0]
    ri = lax.broadcasted_iota(jnp.int32, (CHUNK, LANES), 0)
    ci = lax.broadcasted_iota(jnp.int32, (CHUNK, LANES), 1)
    cj = jnp.where(ci >= CHUNK, ci - CHUNK, ci)
    lane0 = ci < CHUNK
    eye = jnp.where(ri == cj, 1.0, 0.0).astype(F32)
    sub_blk = (ri // SUB) == (cj // SUB)
    r2 = lax.broadcasted_iota(jnp.int32, (LANES, LANES), 0)
    c2 = lax.broadcasted_iota(jnp.int32, (LANES, LANES), 1)
    bd_mask = (r2 >= CHUNK) == (c2 >= CHUNK)
    dirs = ((at0_ref, rt0_ref, bt0_ref, kt0_ref, v0_ref, pl0_ref, ri > cj, ri >= cj),
            (at1_ref, rt1_ref, bt1_ref, kt1_ref, v1_ref, pl1_ref, ri < cj, ri <= cj))
    n_pairs = RWKV_DIM // LANES
    items = []
    for b in range(n_batch):
        for d, (at_ref, rt_ref, bt_ref, kt_ref, v_ref, pl_ref, strict, incl) in enumerate(dirs):
            for p in range(n_pairs):
                sl = slice(p * LANES, (p + 1) * LANES)
                items.append((at_ref[b, :, sl], rt_ref[b, :, sl], bt_ref[b, :, sl], kt_ref[b, :, sl],
                              v_ref[b, :, sl], pl_ref[b, 0, :, sl], s_ref[b, d, p], strict, incl))
    outs = _pair_chunks(items, sub_blk, eye, lane0, bd_mask)
    for b in range(n_batch):
        for d, y_ref in enumerate((yf_ref, yb_ref)):
            base = (b * 2 + d) * n_pairs
            y_ref[b] = jnp.concatenate([outs[base + p][0] for p in range(n_pairs)],
                                       axis=1).astype(BF16)
            for p in range(n_pairs):
                s_ref[b, d, p] = outs[base + p][1]


def _scan(at0, rt0, bt0, kt0, at1, rt1, bt1, kt1, v, pl0, pl1):
    B, T, C = v.shape
    nc = T // CHUNK
    fwd = pl.BlockSpec((B, CHUNK, C), lambda c: (0, c, 0))
    bwd = pl.BlockSpec((B, CHUNK, C), lambda c: (0, nc - 1 - c, 0))
    pl_f = pl.BlockSpec((B, 1, 1, C), lambda c: (0, c, 0, 0))
    pl_b = pl.BlockSpec((B, 1, 1, C), lambda c: (0, nc - 1 - c, 0, 0))
    y_shape = jax.ShapeDtypeStruct((B, T, C), BF16)
    return pl.pallas_call(
        _scan_kernel, grid=(nc,),
        in_specs=[fwd, fwd, fwd, fwd, fwd, pl_f, bwd, bwd, bwd, bwd, bwd, pl_b],
        out_specs=[fwd, bwd], out_shape=[y_shape, y_shape],
        scratch_shapes=[pltpu.VMEM((B, 2, C // LANES, LANES, LANES), F32)],
        compiler_params=pltpu.CompilerParams(dimension_semantics=("arbitrary",),
                                             vmem_limit_bytes=VMEM_LIMIT),
        name="rwkv_scan",
    )(at0, rt0, bt0, kt0, v, pl0, at1, rt1, bt1, kt1, v, pl1)


def _attn_kernel(sink_ref, *refs):
    n_sub = ATT_QBLOCKS
    bias_refs = refs[:n_sub]
    q_ref, kp_ref, kc_ref, kn_ref, vp_ref, vc_ref, vn_ref, o_ref = refs[n_sub:]
    blk = WINDOW
    k_all = jnp.concatenate([kp_ref[0], kc_ref[0], kn_ref[0]], axis=0)
    v_all = jnp.concatenate([vp_ref[0], vc_ref[0], vn_ref[0]], axis=0)

    def swap_halves(x):
        return jnp.concatenate([x[:, HEAD_DIM:], x[:, :HEAD_DIM]], axis=1)

    lane0 = lax.broadcasted_iota(jnp.int32, k_all.shape, 1) < HEAD_DIM

    def variants(x):
        xs = swap_halves(x)
        zero = jnp.zeros_like(x)
        return ((jnp.where(lane0, x, zero), jnp.where(lane0, zero, xs)),
                (jnp.where(lane0, xs, zero), jnp.where(lane0, zero, x)))

    k_var = variants(k_all)
    v_var = variants(v_all)

    group = ATT_HEADS // (KV_DIM // HEAD_DIM)

    items = [(u, h) for u in range(n_sub) for h in range(ATT_HEADS)]
    keys = lambda var, u, h: var[h // group][h % 2][u * blk:(u + 3) * blk]
    q_pairs = [[q_ref[0, u * blk:(u + 1) * blk, j * LANES:(j + 1) * LANES]
                * jnp.asarray(HEAD_DIM ** -0.5, BF16) for j in range(ATT_DIM // LANES)]
               for u in range(n_sub)]
    s = [_dot_nt(q_pairs[u][h // 2], keys(k_var, u, h)) + bias_refs[u][0, h] for u, h in items]
    m = [jnp.maximum(jnp.max(s[i], axis=-1, keepdims=True), sink_ref[h]) for i, (u, h) in enumerate(items)]
    p = [jnp.exp(s[i] - m[i]) for i in range(len(items))]
    den = [jnp.sum(p[i], axis=-1, keepdims=True) + jnp.exp(sink_ref[h] - m[i])
           for i, (u, h) in enumerate(items)]
    o = [_dot(p[i], keys(v_var, u, h)) * (1.0 / den[i]) for i, (u, h) in enumerate(items)]
    for u in range(n_sub):
        ou = o[u * ATT_HEADS:(u + 1) * ATT_HEADS]
        o_ref[0, u * blk:(u + 1) * blk, :] = jnp.concatenate(
            [ou[2 * j] + ou[2 * j + 1] for j in range(ATT_DIM // LANES)], axis=1).astype(o_ref.dtype)


def _attn_bias(blk):
    qi = np.arange(blk)[:, None]
    kj = np.arange(3 * blk)[None, :]
    dist = np.abs(kj - blk - qi)
    slopes = 2.0 ** (-8.0 * np.arange(1, ATT_HEADS + 1, dtype=np.float32) / ATT_HEADS)
    alibi = -slopes[:, None, None].astype(np.float32) * dist[None].astype(np.float32)
    out = []
    for has_prev, has_next in ((False, True), (True, True), (True, False)):
        valid = (dist <= WINDOW) & (has_prev | (kj >= blk)) & (has_next | (kj < 2 * blk))
        out.append(np.where(valid[None], alibi, np.float32(MASK_VALUE)))
    return jnp.asarray(np.stack(out), F32)


def _attention(q, ka, va, sink):
    B, T, _ = q.shape
    blk = WINDOW
    n_sub = ATT_QBLOCKS
    steps = T // (blk * n_sub)
    nb = T // blk
    cur = lambda width: pl.BlockSpec((1, n_sub * blk, width), lambda b, n: (b, n, 0))
    prev = pl.BlockSpec((1, blk, KV_DIM), lambda b, n: (b, jnp.maximum(n * n_sub - 1, 0), 0))
    nxt = pl.BlockSpec((1, blk, KV_DIM), lambda b, n: (b, jnp.minimum((n + 1) * n_sub, nb - 1), 0))

    def bias_spec(u):
        def index(b, n):
            g = n * n_sub + u
            return (jnp.where(g == 0, 0, jnp.where(g == nb - 1, 2, 1)), 0, 0, 0)
        return pl.BlockSpec((1, ATT_HEADS, blk, 3 * blk), index)

    bias = _attn_bias(blk)
    return pl.pallas_call(
        _attn_kernel, grid=(B, steps),
        in_specs=[pl.BlockSpec(memory_space=pltpu.SMEM)] + [bias_spec(u) for u in range(n_sub)]
                 + [cur(ATT_DIM), prev, cur(KV_DIM), nxt, prev, cur(KV_DIM), nxt],
        out_specs=cur(ATT_DIM), out_shape=jax.ShapeDtypeStruct((B, T, ATT_DIM), BF16),
        compiler_params=pltpu.CompilerParams(dimension_semantics=("parallel", "parallel"),
                                             vmem_limit_bytes=VMEM_LIMIT),
        name="band_attn",
    )(sink, *([bias] * n_sub), q, ka, ka, ka, va, va, va)


def _out_proj_kernel(x_ref, yf_ref, yb_ref, bonus_ref, g_ref, oatt_ref, lg_ref, lb_ref,
                     mean_ref, w_ref, o_ref):
    def seg_mean(t):
        return _dot(t, mean_ref[...])

    y = yf_ref[0].astype(F32) + yb_ref[0].astype(F32)
    d = y - seg_mean(y)
    var = seg_mean(d * d)
    yn = d * lax.rsqrt(var + LNX_EPS) * lg_ref[...] + lb_ref[...]
    o_rwkv = (yn + bonus_ref[0]) * g_ref[0]
    mix = _dot(o_rwkv, w_ref[:RWKV_DIM, :]) + _dot(oatt_ref[0], w_ref[RWKV_DIM:, :])
    o_ref[0] = x_ref[0] + mix


def _out_proj(x, yf, yb, bonus, g, o_att, lnx_g, lnx_b, mean_blk, w_out):
    B, T, D = x.shape
    tm = TM_OUT
    tok = lambda width: pl.BlockSpec((1, tm, width), lambda b, i: (b, i, 0))
    const = lambda shape: pl.BlockSpec(shape, lambda b, i: (0,) * len(shape))
    return pl.pallas_call(
        _out_proj_kernel, grid=(B, T // tm),
        in_specs=[tok(D), tok(RWKV_DIM), tok(RWKV_DIM), tok(RWKV_DIM), tok(RWKV_DIM), tok(ATT_DIM),
                  const((1, RWKV_DIM)), const((1, RWKV_DIM)), const((RWKV_DIM, RWKV_DIM)),
                  const((D, D))],
        out_specs=tok(D), out_shape=jax.ShapeDtypeStruct((B, T, D), F32),
        compiler_params=pltpu.CompilerParams(dimension_semantics=("parallel", "parallel"),
                                             vmem_limit_bytes=VMEM_LIMIT),
        name="out_proj",
    )(x, yf, yb, bonus, g, o_att, lnx_g, lnx_b, mean_blk, w_out)


def _ffn_kernel(x_ref, xp_ref, xn_ref, ln2_ref, wg_ref, wu_ref, cw_ref, cb_ref, wd_ref, lnf_ref,
                o_ref):
    i = pl.program_id(1)
    n_tiles = pl.num_programs(1)
    tm = x_ref.shape[1]
    x = x_ref[0]
    x_ext = jnp.concatenate([jnp.where(i > 0, xp_ref[0], 0.0), x,
                             jnp.where(i < n_tiles - 1, xn_ref[0], 0.0)], axis=0)
    h = _rms_norm(x_ext, ln2_ref[...]).astype(BF16)
    rows = tm + 2 * SUBLANES
    core = slice(SUBLANES, tm + SUBLANES)
    acc = jnp.zeros((tm, D_MODEL), F32)
    for c0, c1 in zip(FF_SPLITS[:-1], FF_SPLITS[1:]):
        cs = slice(c0, c1)
        gp_ext = _dot(h, wg_ref[:, cs])
        prev = pltpu.roll(gp_ext, 1, axis=0)[core]
        nxt = pltpu.roll(gp_ext, rows - 1, axis=0)[core]
        gate = (prev * cw_ref[0:1, cs] + gp_ext[core] * cw_ref[1:2, cs] + nxt * cw_ref[2:3, cs]
                + cb_ref[:, cs])
        act = 0.5 * gate * (1.0 + lax.erf(gate * float(1.0 / np.sqrt(2.0))))
        up = _dot(h, wu_ref[:, cs])[core]
        acc = acc + _dot(act * up, wd_ref[cs, :])
    o_ref[0] = _rms_norm(x + acc, lnf_ref[...])


def _ffn(x1, ln2_g, wg, wu, conv_w, conv_b, wd, lnf_g):
    B, T, D = x1.shape
    tm = TM_FFN
    rows8 = tm // SUBLANES
    tok = pl.BlockSpec((1, tm, D), lambda b, i: (b, i, 0))
    const = lambda shape: pl.BlockSpec(shape, lambda b, i: (0,) * len(shape))
    resident = lambda shape: pl.BlockSpec(shape, lambda b, i: (0,) * len(shape),
                                          pipeline_mode=pl.Buffered(1))
    return pl.pallas_call(
        _ffn_kernel, grid=(B, T // tm),
        in_specs=[tok,
                  pl.BlockSpec((1, SUBLANES, D), lambda b, i: (b, jnp.maximum(i * rows8 - 1, 0), 0)),
                  pl.BlockSpec((1, SUBLANES, D),
                               lambda b, i: (b, jnp.minimum((i + 1) * rows8, T // SUBLANES - 1), 0)),
                  const((1, D)), resident((D, D_FF)), resident((D, D_FF)),
                  const((CONV_WIDTH, D_FF)), const((1, D_FF)), resident((D_FF, D)), const((1, D))],
        out_specs=tok, out_shape=jax.ShapeDtypeStruct((B, T, D), F32),
        compiler_params=pltpu.CompilerParams(dimension_semantics=("parallel", "parallel"),
                                             vmem_limit_bytes=VMEM_LIMIT),
        name="conv_ffn",
    )(x1, x1, x1, ln2_g, wg, wu, conv_w, conv_b, wd, lnf_g)


def _constants():
    idx = np.arange(RWKV_DIM)
    same_head = (idx[:, None] // HEAD_DIM) == (idx[None, :] // HEAD_DIM)
    t = np.arange(CHUNK)
    tri = np.stack([t[:, None] >= t[None, :], t[:, None] <= t[None, :]]).astype(np.float32)
    return (jnp.asarray(same_head.astype(np.float32), BF16),
            jnp.asarray(same_head.astype(np.float32) / HEAD_DIM, BF16),
            jnp.asarray(np.concatenate([tri, tri], axis=2), BF16))


def kernel(x, ln1_g, w_in, shift_mu_prev, shift_mu_next, decay_w0, decay_w2, iclr_a0, iclr_a2,
           gate_g2, k_k, k_a, r_k, lnx_g, lnx_b, attn_sink, w_out, ln2_g, ffn_w_gate, ffn_w_up,
           ffn_conv_w, ffn_conv_b, ffn_w_down, lnf_g):
    B, T, _ = x.shape
    assert w_in.shape[0] == 1, "single-layer block"
    l = 0
    ones_blk, mean_blk, tri = _constants()
    row = lambda a: a.reshape(1, -1)
    w2 = decay_w2[l]
    w2_pad = jnp.concatenate([w2, jnp.zeros_like(w2)], axis=1)
    w2hi = w2_pad.astype(BF16)
    w2lo = (w2_pad - w2hi.astype(F32)).astype(BF16)
    w2_cat = jnp.concatenate([w2hi, w2hi, w2lo], axis=1)
    mu_p, mu_n = shift_mu_prev[l], shift_mu_next[l]
    a2_pad = jnp.concatenate([jnp.zeros_like(iclr_a2[l]), iclr_a2[l]], axis=0).astype(BF16)
    (at0, rt0, bt0, kt0, at1, rt1, bt1, kt1, v, pl0, pl1, g, bonus, q, ka, va) = _in_proj(
        x, row(ln1_g[l]), w_in[l].astype(BF16), row(1.0 - mu_p - mu_n), row(mu_p), row(mu_n),
        decay_w0[l], w2_cat, row(iclr_a0[l]), a2_pad, gate_g2[l].astype(BF16),
        row(k_k[l]), row(k_a[l]), row(r_k[l]), ones_blk, tri)
    cpt = TM_IN // CHUNK
    pl0 = pl0[:, :, :cpt].reshape(B, T // CHUNK, 1, RWKV_DIM)
    pl1 = pl1[:, :, :cpt].reshape(B, T // CHUNK, 1, RWKV_DIM)
    yf, yb = _scan(at0, rt0, bt0, kt0, at1, rt1, bt1, kt1, v, pl0, pl1)
    o_att = _attention(q, ka, va, attn_sink[l])
    x1 = _out_proj(x, yf, yb, bonus, g, o_att, row(lnx_g[l]), row(lnx_b[l]), mean_blk,
                   w_out[l].astype(BF16))
    return _ffn(x1, row(ln2_g[l]), ffn_w_gate[l].astype(BF16), ffn_w_up[l].astype(BF16),
                ffn_conv_w[l], row(ffn_conv_b[l]), ffn_w_down[l].astype(BF16), row(lnf_g))
```

```python
import functools

import numpy as np
import jax
import jax.numpy as jnp
from jax import lax
from jax.experimental import pallas as pl
from jax.experimental.pallas import tpu as pltpu

F32 = jnp.float32
BF16 = jnp.bfloat16

D_MODEL = 1024
HEAD_DIM = 64
RWKV_DIM = 512
ATT_DIM = 512
ATT_HEADS = 8
KV_DIM = 128
LORA_DIM = 256
SHIFT_DIM = 3 * RWKV_DIM + LORA_DIM
PROJ_DIM = SHIFT_DIM + ATT_DIM + 2 * KV_DIM
WINDOW = 128
D_FF = 2816
CONV_WIDTH = 3
NORM_EPS = 1e-6
LNX_EPS = 64e-5
L2_EPS = 1e-12
MASK_VALUE = -1e30
LOG2E = float(np.log2(np.e))
NEG_DECAY_SCALE = float(-np.exp(-0.5) * np.log2(np.e))

LANES = 128
SUBLANES = 8
CHUNK = 64
SUB = 16
VMEM_LIMIT = 56 * 1024 * 1024

TM_IN = 512
TM_FFN = 512
ATT_QBLOCKS = 2
FF_SPLITS = (0, 1536, D_FF)


def _dot(a, b):
    return jnp.dot(a.astype(BF16), b.astype(BF16), preferred_element_type=F32)


def _dot_nt(a, b):
    return lax.dot_general(a.astype(BF16), b.astype(BF16), (((1,), (1,)), ((), ())),
                           preferred_element_type=F32)


def _dot_tn(a, b):
    return lax.dot_general(a.astype(BF16), b.astype(BF16), (((0,), (0,)), ((), ())),
                           preferred_element_type=F32)


def _split2(x):
    hi = x.astype(BF16)
    lo = (x - hi.astype(F32)).astype(BF16)
    return hi, lo


def _rms_norm(x, g):
    return x * lax.rsqrt(jnp.mean(x * x, axis=-1, keepdims=True) + NORM_EPS) * g


def _in_proj_kernel(x_ref, xp_ref, xn_ref, ln1_ref, w_ref, muc_ref, mup_ref, mun_ref, w0_ref,
                    w2_ref, a0_ref, a2_ref, g2_ref, kk_ref, ka_ref, rk_ref, ones_ref, tri_ref,
                    at0_ref, rt0_ref, bt0_ref, kt0_ref, at1_ref, rt1_ref, bt1_ref, kt1_ref,
                    v_ref, pl0_ref, pl1_ref, g_ref, bonus_ref, q_ref, ka_o_ref, va_o_ref):
    i = pl.program_id(1)
    n_tiles = pl.num_programs(1)
    tm = x_ref.shape[1]
    ln1 = ln1_ref[...]
    x_ext = jnp.concatenate([jnp.where(i > 0, xp_ref[0], 0.0), x_ref[0],
                             jnp.where(i < n_tiles - 1, xn_ref[0], 0.0)], axis=0)
    h = _rms_norm(x_ext, ln1).astype(BF16)
    rows = tm + 2 * SUBLANES
    core = slice(SUBLANES, tm + SUBLANES)

    def shifted(c0, c1):
        p_ext = _dot(h, w_ref[:, c0:c1])
        p = p_ext[core]
        prev = pltpu.roll(p_ext, 1, axis=0)[core]
        nxt = pltpu.roll(p_ext, rows - 1, axis=0)[core]
        return p * muc_ref[:, c0:c1] + prev * mup_ref[:, c0:c1] + nxt * mun_ref[:, c0:c1]

    def seg_sum(x):
        return _dot(x, ones_ref[...])

    codes = shifted(3 * RWKV_DIM, SHIFT_DIM)
    c_di = codes[:, :LANES]
    c_g = codes[:, LANES:]
    th_hi, th_lo = _split2(jnp.tanh(c_di))
    th_cat = jnp.concatenate([th_hi, th_lo, th_hi], axis=1)
    a_vec = jax.nn.sigmoid(a0_ref[...] + _dot(c_di, a2_ref[...]))
    g_ref[0] = _dot(jax.nn.sigmoid(c_g), g2_ref[...]).astype(BF16)

    r = shifted(0, RWKV_DIM)
    k = shifted(RWKV_DIM, 2 * RWKV_DIM)
    v = shifted(2 * RWKV_DIM, 3 * RWKV_DIM)
    v_ref[0] = v.astype(BF16)

    kkr = k * kk_ref[...]
    n2 = seg_sum(kkr * kkr)
    kk = kkr * lax.rsqrt(jnp.maximum(n2, L2_EPS * L2_EPS))
    k2 = k * (1.0 + (a_vec - 1.0) * ka_ref[...])
    b_vec = kk * a_vec
    neg_kk = -kk
    bonus_ref[0] = (seg_sum(r * k2 * rk_ref[...]) * v).astype(BF16)

    dir_outs = ((at0_ref, rt0_ref, bt0_ref, kt0_ref, pl0_ref),
                (at1_ref, rt1_ref, bt1_ref, kt1_ref, pl1_ref))
    for d, (at_ref, rt_ref, bt_ref, kt_ref, pl_ref) in enumerate(dir_outs):
        z = w0_ref[d:d + 1, :] + jnp.dot(th_cat, w2_ref[d], preferred_element_type=F32)
        lw = NEG_DECAY_SCALE / (1.0 + jnp.exp2(z * (-LOG2E)))
        lw_hi, lw_lo = _split2(lw)
        tri = tri_ref[d]
        parts = []
        for j in range(tm // CHUNK):
            cs = slice(j * CHUNK, (j + 1) * CHUNK)
            parts.append(jnp.dot(tri, jnp.concatenate([lw_hi[cs], lw_lo[cs]], axis=0),
                                 preferred_element_type=F32))
        ci = jnp.concatenate(parts, axis=0)
        end = 0 if d else CHUNK - 1
        tot = [parts[j][end:end + 1] for j in range(tm // CHUNK)]
        tot += [jnp.zeros_like(tot[0])] * (SUBLANES - len(tot))
        pl_ref[0, 0] = jnp.exp2(jnp.concatenate(tot, axis=0))
        e_inc = jnp.exp2(ci)
        e_exc = jnp.exp2(ci - lw)
        e_inv = 1.0 / e_inc
        at_ref[0] = (neg_kk * e_exc).astype(BF16)
        rt_ref[0] = (r * e_inc).astype(BF16)
        bt_ref[0] = (b_vec * e_inv).astype(BF16)
        kt_ref[0] = (k2 * e_inv).astype(BF16)

    att = _dot(h, w_ref[:, SHIFT_DIM:PROJ_DIM])[core]
    q_ref[0] = att[:, :ATT_DIM].astype(BF16)
    ka_o_ref[0] = att[:, ATT_DIM:ATT_DIM + KV_DIM].astype(BF16)
    va_o_ref[0] = att[:, ATT_DIM + KV_DIM:].astype(BF16)


def _in_proj(x, ln1_g, w_in, mu_cur, mu_prev, mu_next, decay_w0, w2_cat, iclr_a0, a2_pad, gate_g2,
             k_k, k_a, r_k, ones_blk, tri):
    B, T, D = x.shape
    tm = TM_IN
    nt = T // tm
    rows8 = tm // SUBLANES
    const = lambda shape: pl.BlockSpec(shape, lambda b, i: (0,) * len(shape))
    tok = lambda width: pl.BlockSpec((1, tm, width), lambda b, i: (b, i, 0))
    in_specs = [
        tok(D),
        pl.BlockSpec((1, SUBLANES, D), lambda b, i: (b, jnp.maximum(i * rows8 - 1, 0), 0)),
        pl.BlockSpec((1, SUBLANES, D), lambda b, i: (b, jnp.minimum((i + 1) * rows8, T // SUBLANES - 1), 0)),
        const((1, D)), const((D, PROJ_DIM)),
        const((1, SHIFT_DIM)), const((1, SHIFT_DIM)), const((1, SHIFT_DIM)),
        const((2, RWKV_DIM)), const((2, 3 * LANES, RWKV_DIM)),
        const((1, RWKV_DIM)), const((LANES, RWKV_DIM)), const((LANES, RWKV_DIM)),
        const((1, RWKV_DIM)), const((1, RWKV_DIM)), const((1, RWKV_DIM)),
        const((RWKV_DIM, RWKV_DIM)), const((2, CHUNK, 2 * CHUNK)),
    ]
    tok_bf = jax.ShapeDtypeStruct((B, T, RWKV_DIM), BF16)
    pl_shape = jax.ShapeDtypeStruct((B, nt, SUBLANES, RWKV_DIM), F32)
    pl_spec = pl.BlockSpec((1, 1, SUBLANES, RWKV_DIM), lambda b, i: (b, i, 0, 0))
    out_shape = [tok_bf] * 9 + [pl_shape, pl_shape, tok_bf, tok_bf,
                                jax.ShapeDtypeStruct((B, T, ATT_DIM), BF16),
                                jax.ShapeDtypeStruct((B, T, KV_DIM), BF16),
                                jax.ShapeDtypeStruct((B, T, KV_DIM), BF16)]
    out_specs = [tok(RWKV_DIM)] * 9 + [pl_spec, pl_spec, tok(RWKV_DIM), tok(RWKV_DIM),
                                       tok(ATT_DIM), tok(KV_DIM), tok(KV_DIM)]
    return pl.pallas_call(
        _in_proj_kernel, grid=(B, nt), in_specs=in_specs, out_specs=out_specs, out_shape=out_shape,
        compiler_params=pltpu.CompilerParams(dimension_semantics=("parallel", "parallel"),
                                             vmem_limit_bytes=VMEM_LIMIT),
        name="in_proj",
    )(x, x, x, ln1_g, w_in, mu_cur, mu_prev, mu_next, decay_w0, w2_cat, iclr_a0, a2_pad, gate_g2,
      k_k, k_a, r_k, ones_blk, tri)


def _pair_chunks(items, sub_blk, eye, lane0, bd_mask):
    n = range(len(items))
    at, rt, bt, kt, v, p_last, s_prev, strict, incl = zip(*items)

    def bd(x):
        x = x.astype(BF16)
        zero = jnp.zeros_like(x)
        return jnp.concatenate([jnp.where(lane0, x, zero), jnp.where(lane0, zero, x)], axis=0)

    def pmm(x, y):
        return _dot(x, bd(y))

    sc = [_dot_nt(jnp.concatenate([at[i], rt[i]], axis=0),
                  jnp.concatenate([bd(bt[i]), bd(kt[i])], axis=0)) for i in n]
    a_ab = [jnp.where(strict[i], sc[i][:CHUNK, :LANES], 0.0) for i in n]
    a_ak = [jnp.where(strict[i], sc[i][:CHUNK, LANES:], 0.0) for i in n]
    a_rb = [jnp.where(incl[i], sc[i][CHUNK:, :LANES], 0.0) for i in n]
    a_rk = [jnp.where(incl[i], sc[i][CHUNK:, LANES:], 0.0) for i in n]

    a_d = [jnp.where(sub_blk, a_ab[i], 0.0) for i in n]
    a_o = [a_ab[i] - a_d[i] for i in n]
    a2 = [pmm(a_d[i], a_d[i]) for i in n]
    xy = [pmm(jnp.concatenate([a_ak[i], a_rk[i]], axis=0), v[i]) for i in n]
    x1 = [xy[i][:CHUNK] for i in n]
    yk = [xy[i][CHUNK:] for i in n]
    t1 = [eye + a_d[i] for i in n]
    st = [pmm(jnp.concatenate([a2[i], t1[i]], axis=0), a2[i]) for i in n]
    a4 = [st[i][:CHUNK] for i in n]
    t1 = [t1[i] + st[i][CHUNK:] for i in n]
    st = [pmm(jnp.concatenate([a4[i], t1[i]], axis=0), a4[i]) for i in n]
    a8 = [st[i][:CHUNK] for i in n]
    t1 = [t1[i] + st[i][CHUNK:] for i in n]
    t_d = [t1[i] + pmm(t1[i], a8[i]) for i in n]
    mx = [_dot(t_d[i], jnp.concatenate([bd(a_o[i]), bd(at[i]), bd(x1[i])], axis=1)) for i in n]
    m1 = [mx[i][:, :LANES] for i in n]
    xp = [mx[i][:, LANES:] for i in n]
    m2 = [pmm(m1[i], m1[i]) for i in n]
    q = [eye + m1[i] + m2[i] + pmm(m1[i], m2[i]) for i in n]
    wu = [_dot(q[i], jnp.concatenate([bd(xp[i][:, :LANES]), bd(xp[i][:, LANES:])], axis=1))
          for i in n]
    hs = [_dot_nt(jnp.concatenate([wu[i][:, :LANES].astype(BF16), rt[i]], axis=0), s_prev[i])
          for i in n]
    u = [hs[i][:CHUNK] + wu[i][:, LANES:] for i in n]
    y = [hs[i][CHUNK:] + pmm(a_rb[i], u[i]) + yk[i] for i in n]
    upd = [_dot_tn(jnp.concatenate([u[i].astype(BF16), v[i]], axis=0),
                   jnp.concatenate([bt[i], kt[i]], axis=0)) for i in n]
    s_new = [(s_prev[i] + jnp.where(bd_mask, upd[i], 0.0)) * p_last[i] for i in n]
    return list(zip(y, s_new))


def _scan_kernel(at0_ref, rt0_ref, bt0_ref, kt0_ref, v0_ref, pl0_ref,
                 at1_ref, rt1_ref, bt1_ref, kt1_ref, v1_ref, pl1_ref,
                 yf_ref, yb_ref, s_ref):
    c = pl.program_id(0)

    @pl.when(c == 0)
    def _():
        s_ref[...] = jnp.zeros_like(s_ref)

    n_batch = v0_ref.shape[0]
    ri = lax.broadcasted_iota(jnp.int32, (CHUNK, LANES), 0)
    ci = lax.broadcasted_iota(jnp.int32, (CHUNK, LANES), 1)
    cj = jnp.where(ci >= CHUNK, ci - CHUNK, ci)
    lane0 = ci < CHUNK
    eye = jnp.where(ri == cj, 1.0, 0.0).astype(F32)
    sub_blk = (ri // SUB) == (cj // SUB)
    r2 = lax.broadcasted_iota(jnp.int32, (LANES, LANES), 0)
    c2 = lax.broadcasted_iota(jnp.int32, (LANES, LANES), 1)
    bd_mask = (r2 >= CHUNK) == (c2 >= CHUNK)
    dirs = ((at0_ref, rt0_ref, bt0_ref, kt0_ref, v0_ref, pl0_ref, ri > cj, ri >= cj),
            (at1_ref, rt1_ref, bt1_ref, kt1_ref, v1_ref, pl1_ref, ri < cj, ri <= cj))
    n_pairs = RWKV_DIM // LANES
    items = []
    for b in range(n_batch):
        for d, (at_ref, rt_ref, bt_ref, kt_ref, v_ref, pl_ref, strict, incl) in enumerate(dirs):
            for p in range(n_pairs):
                sl = slice(p * LANES, (p + 1) * LANES)
                items.append((at_ref[b, :, sl], rt_ref[b, :, sl], bt_ref[b, :, sl], kt_ref[b, :, sl],
                              v_ref[b, :, sl], pl_ref[b, 0, :, sl], s_ref[b, d, p], strict, incl))
    outs = _pair_chunks(items, sub_blk, eye, lane0, bd_mask)
    for b in range(n_batch):
        for d, y_ref in enumerate((yf_ref, yb_ref)):
            base = (b * 2 + d) * n_pairs
            y_ref[b] = jnp.concatenate([outs[base + p][0] for p in range(n_pairs)],
                                       axis=1).astype(BF16)
            for p in range(n_pairs):
                s_ref[b, d, p] = outs[base + p][1]


def _scan(at0, rt0, bt0, kt0, at1, rt1, bt1, kt1, v, pl0, pl1):
    B, T, C = v.shape
    nc = T // CHUNK
    fwd = pl.BlockSpec((B, CHUNK, C), lambda c: (0, c, 0))
    bwd = pl.BlockSpec((B, CHUNK, C), lambda c: (0, nc - 1 - c, 0))
    pl_f = pl.BlockSpec((B, 1, 1, C), lambda c: (0, c, 0, 0))
    pl_b = pl.BlockSpec((B, 1, 1, C), lambda c: (0, nc - 1 - c, 0, 0))
    y_shape = jax.ShapeDtypeStruct((B, T, C), BF16)
    return pl.pallas_call(
        _scan_kernel, grid=(nc,),
        in_specs=[fwd, fwd, fwd, fwd, fwd, pl_f, bwd, bwd, bwd, bwd, bwd, pl_b],
        out_specs=[fwd, bwd], out_shape=[y_shape, y_shape],
        scratch_shapes=[pltpu.VMEM((B, 2, C // LANES, LANES, LANES), F32)],
        compiler_params=pltpu.CompilerParams(dimension_semantics=("arbitrary",),
                                             vmem_limit_bytes=VMEM_LIMIT),
        name="rwkv_scan",
    )(at0, rt0, bt0, kt0, v, pl0, at1, rt1, bt1, kt1, v, pl1)


def _attn_kernel(sink_ref, *refs):
    n_sub = ATT_QBLOCKS
    bias_refs = refs[:n_sub]
    q_ref, kp_ref, kc_ref, kn_ref, vp_ref, vc_ref, vn_ref, o_ref = refs[n_sub:]
    blk = WINDOW
    k_all = jnp.concatenate([kp_ref[0], kc_ref[0], kn_ref[0]], axis=0)
    v_all = jnp.concatenate([vp_ref[0], vc_ref[0], vn_ref[0]], axis=0)

    def swap_halves(x):
        return jnp.concatenate([x[:, HEAD_DIM:], x[:, :HEAD_DIM]], axis=1)

    lane0 = lax.broadcasted_iota(jnp.int32, k_all.shape, 1) < HEAD_DIM

    def variants(x):
        xs = swap_halves(x)
        zero = jnp.zeros_like(x)
        return ((jnp.where(lane0, x, zero), jnp.where(lane0, zero, xs)),
                (jnp.where(lane0, xs, zero), jnp.where(lane0, zero, x)))

    k_var = variants(k_all)
    v_var = variants(v_all)

    group = ATT_HEADS // (KV_DIM // HEAD_DIM)

    items = [(u, h) for u in range(n_sub) for h in range(ATT_HEADS)]
    keys = lambda var, u, h: var[h // group][h % 2][u * blk:(u + 3) * blk]
    q_pairs = [[q_ref[0, u * blk:(u + 1) * blk, j * LANES:(j + 1) * LANES]
                * jnp.asarray(HEAD_DIM ** -0.5, BF16) for j in range(ATT_DIM // LANES)]
               for u in range(n_sub)]
    s = [_dot_nt(q_pairs[u][h // 2], keys(k_var, u, h)) + bias_refs[u][0, h] for u, h in items]
    m = [jnp.maximum(jnp.max(s[i], axis=-1, keepdims=True), sink_ref[h]) for i, (u, h) in enumerate(items)]
    p = [jnp.exp(s[i] - m[i]) for i in range(len(items))]
    den = [jnp.sum(p[i], axis=-1, keepdims=True) + jnp.exp(sink_ref[h] - m[i])
           for i, (u, h) in enumerate(items)]
    o = [_dot(p[i], keys(v_var, u, h)) * (1.0 / den[i]) for i, (u, h) in enumerate(items)]
    for u in range(n_sub):
        ou = o[u * ATT_HEADS:(u + 1) * ATT_HEADS]
        o_ref[0, u * blk:(u + 1) * blk, :] = jnp.concatenate(
            [ou[2 * j] + ou[2 * j + 1] for j in range(ATT_DIM // LANES)], axis=1).astype(o_ref.dtype)


def _attn_bias(blk):
    qi = np.arange(blk)[:, None]
    kj = np.arange(3 * blk)[None, :]
    dist = np.abs(kj - blk - qi)
    slopes = 2.0 ** (-8.0 * np.arange(1, ATT_HEADS + 1, dtype=np.float32) / ATT_HEADS)
    alibi = -slopes[:, None, None].astype(np.float32) * dist[None].astype(np.float32)
    out = []
    for has_prev, has_next in ((False, True), (True, True), (True, False)):
        valid = (dist <= WINDOW) & (has_prev | (kj >= blk)) & (has_next | (kj < 2 * blk))
        out.append(np.where(valid[None], alibi, np.float32(MASK_VALUE)))
    return jnp.asarray(np.stack(out), F32)


def _attention(q, ka, va, sink):
    B, T, _ = q.shape
    blk = WINDOW
    n_sub = ATT_QBLOCKS
    steps = T // (blk * n_sub)
    nb = T // blk
    cur = lambda width: pl.BlockSpec((1, n_sub * blk, width), lambda b, n: (b, n, 0))
    prev = pl.BlockSpec((1, blk, KV_DIM), lambda b, n: (b, jnp.maximum(n * n_sub - 1, 0), 0))
    nxt = pl.BlockSpec((1, blk, KV_DIM), lambda b, n: (b, jnp.minimum((n + 1) * n_sub, nb - 1), 0))

    def bias_spec(u):
        def index(b, n):
            g = n * n_sub + u
            return (jnp.where(g == 0, 0, jnp.where(g == nb - 1, 2, 1)), 0, 0, 0)
        return pl.BlockSpec((1, ATT_HEADS, blk, 3 * blk), index)

    bias = _attn_bias(blk)
    return pl.pallas_call(
        _attn_kernel, grid=(B, steps),
        in_specs=[pl.BlockSpec(memory_space=pltpu.SMEM)] + [bias_spec(u) for u in range(n_sub)]
                 + [cur(ATT_DIM), prev, cur(KV_DIM), nxt, prev, cur(KV_DIM), nxt],
        out_specs=cur(ATT_DIM), out_shape=jax.ShapeDtypeStruct((B, T, ATT_DIM), BF16),
        compiler_params=pltpu.CompilerParams(dimension_semantics=("parallel", "parallel"),
                                             vmem_limit_bytes=VMEM_LIMIT),
        name="band_attn",
    )(sink, *([bias] * n_sub), q, ka, ka, ka, va, va, va)


HALO = 16


def _mix_ffn_kernel(*refs):
    (x_m, x_p, x_n, yf_m, yf_p, yf_n, yb_m, yb_p, yb_n, bo_m, bo_p, bo_n, g_m, g_p, g_n,
     oa_m, oa_p, oa_n, lg_ref, lb_ref, mean_ref, wo_ref, ln2_ref, wg_ref, wu_ref, cw_ref, cb_ref,
     wd_ref, lnf_ref, o_ref) = refs
    i = pl.program_id(1)
    n_tiles = pl.num_programs(1)
    tm = x_m.shape[1]
    rows = tm + 2 * HALO
    core = slice(HALO, tm + HALO)
    ext = lambda m, p, n: jnp.concatenate([p[0], m[0], n[0]], axis=0)

    def seg_mean(t):
        return _dot(t, mean_ref[...])

    y = ext(yf_m, yf_p, yf_n).astype(F32) + ext(yb_m, yb_p, yb_n).astype(F32)
    d = y - seg_mean(y)
    var = seg_mean(d * d)
    yn = d * lax.rsqrt(var + LNX_EPS) * lg_ref[...] + lb_ref[...]
    o_rwkv = (yn + ext(bo_m, bo_p, bo_n)) * ext(g_m, g_p, g_n)
    mix = _dot(o_rwkv, wo_ref[:RWKV_DIM, :]) + _dot(ext(oa_m, oa_p, oa_n), wo_ref[RWKV_DIM:, :])
    x1 = ext(x_m, x_p, x_n) + mix
    r = lax.broadcasted_iota(jnp.int32, (rows, 1), 0)
    inside = ((r >= HALO) | (i > 0)) & ((r < tm + HALO) | (i < n_tiles - 1))
    x1 = jnp.where(inside, x1, 0.0)

    h = _rms_norm(x1, ln2_ref[...]).astype(BF16)
    h_core = h[core]
    acc = jnp.zeros((tm, D_MODEL), F32)
    for c0, c1 in zip(FF_SPLITS[:-1], FF_SPLITS[1:]):
        cs = slice(c0, c1)
        gp_ext = _dot(h, wg_ref[:, cs])
        prev = pltpu.roll(gp_ext, 1, axis=0)[core]
        nxt = pltpu.roll(gp_ext, rows - 1, axis=0)[core]
        gate = (prev * cw_ref[0:1, cs] + gp_ext[core] * cw_ref[1:2, cs] + nxt * cw_ref[2:3, cs]
                + cb_ref[:, cs])
        act = 0.5 * gate * (1.0 + lax.erf(gate * float(1.0 / np.sqrt(2.0))))
        up = _dot(h_core, wu_ref[:, cs])
        acc = acc + _dot(act * up, wd_ref[cs, :])
    o_ref[0] = _rms_norm(x1[core] + acc, lnf_ref[...])


def _mix_ffn(x, yf, yb, bonus, g, o_att, lnx_g, lnx_b, mean_blk, w_out, ln2_g, wg, wu, conv_w, conv_b,
             wd, lnf_g):
    B, T, D = x.shape
    tm = TM_FFN
    per_tile = tm // HALO
    last = T // HALO - 1

    def tok(width):
        return [pl.BlockSpec((1, tm, width), lambda b, i: (b, i, 0)),
                pl.BlockSpec((1, HALO, width), lambda b, i: (b, jnp.maximum(i * per_tile - 1, 0), 0)),
                pl.BlockSpec((1, HALO, width), lambda b, i: (b, jnp.minimum((i + 1) * per_tile, last), 0))]

    const = lambda shape: pl.BlockSpec(shape, lambda b, i: (0,) * len(shape))
    resident = lambda shape: pl.BlockSpec(shape, lambda b, i: (0,) * len(shape),
                                          pipeline_mode=pl.Buffered(1))
    tokens = (x, yf, yb, bonus, g, o_att)
    return pl.pallas_call(
        _mix_ffn_kernel, grid=(B, T // tm),
        in_specs=[spec for a in tokens for spec in tok(a.shape[-1])]
                 + [const((1, RWKV_DIM)), const((1, RWKV_DIM)), const((RWKV_DIM, RWKV_DIM)), const((D, D)),
                    const((1, D)), resident((D, D_FF)), resident((D, D_FF)),
                    const((CONV_WIDTH, D_FF)), const((1, D_FF)), resident((D_FF, D)), const((1, D))],
        out_specs=pl.BlockSpec((1, tm, D), lambda b, i: (b, i, 0)),
        out_shape=jax.ShapeDtypeStruct((B, T, D), F32),
        compiler_params=pltpu.CompilerParams(dimension_semantics=("parallel", "parallel"),
                                             vmem_limit_bytes=VMEM_LIMIT),
        name="mix_ffn",
    )(*[a for a in tokens for _ in range(3)], lnx_g, lnx_b, mean_blk, w_out, ln2_g, wg, wu, conv_w,
      conv_b, wd, lnf_g)


def _constants():
    idx = np.arange(RWKV_DIM)
    same_head = (idx[:, None] // HEAD_DIM) == (idx[None, :] // HEAD_DIM)
    t = np.arange(CHUNK)
    tri = np.stack([t[:, None] >= t[None, :], t[:, None] <= t[None, :]]).astype(np.float32)
    return (jnp.asarray(same_head.astype(np.float32), BF16),
            jnp.asarray(same_head.astype(np.float32) / HEAD_DIM, BF16),
            jnp.asarray(np.concatenate([tri, tri], axis=2), BF16))


def kernel(x, ln1_g, w_in, shift_mu_prev, shift_mu_next, decay_w0, decay_w2, iclr_a0, iclr_a2,
           gate_g2, k_k, k_a, r_k, lnx_g, lnx_b, attn_sink, w_out, ln2_g, ffn_w_gate, ffn_w_up,
           ffn_conv_w, ffn_conv_b, ffn_w_down, lnf_g):
    B, T, _ = x.shape
    assert w_in.shape[0] == 1, "single-layer block"
    l = 0
    ones_blk, mean_blk, tri = _constants()
    row = lambda a: a.reshape(1, -1)
    w2 = decay_w2[l]
    w2_pad = jnp.concatenate([w2, jnp.zeros_like(w2)], axis=1)
    w2hi = w2_pad.astype(BF16)
    w2lo = (w2_pad - w2hi.astype(F32)).astype(BF16)
    w2_cat = jnp.concatenate([w2hi, w2hi, w2lo], axis=1)
    mu_p, mu_n = shift_mu_prev[l], shift_mu_next[l]
    a2_pad = jnp.concatenate([jnp.zeros_like(iclr_a2[l]), iclr_a2[l]], axis=0).astype(BF16)
    (at0, rt0, bt0, kt0, at1, rt1, bt1, kt1, v, pl0, pl1, g, bonus, q, ka, va) = _in_proj(
        x, row(ln1_g[l]), w_in[l].astype(BF16), row(1.0 - mu_p - mu_n), row(mu_p), row(mu_n),
        decay_w0[l], w2_cat, row(iclr_a0[l]), a2_pad, gate_g2[l].astype(BF16),
        row(k_k[l]), row(k_a[l]), row(r_k[l]), ones_blk, tri)
    cpt = TM_IN // CHUNK
    pl0 = pl0[:, :, :cpt].reshape(B, T // CHUNK, 1, RWKV_DIM)
    pl1 = pl1[:, :, :cpt].reshape(B, T // CHUNK, 1, RWKV_DIM)
    yf, yb = _scan(at0, rt0, bt0, kt0, at1, rt1, bt1, kt1, v, pl0, pl1)
    o_att = _attention(q, ka, va, attn_sink[l])
    return _mix_ffn(x, yf, yb, bonus, g, o_att, row(lnx_g[l]), row(lnx_b[l]), mean_blk,
                    w_out[l].astype(BF16), row(ln2_g[l]), ffn_w_gate[l].astype(BF16),
                    ffn_w_up[l].astype(BF16), ffn_conv_w[l], row(ffn_conv_b[l]),
                    ffn_w_down[l].astype(BF16), row(lnf_g))
```

```python
import functools

import numpy as np
import jax
import jax.numpy as jnp
from jax import lax
from jax.experimental import pallas as pl
from jax.experimental.pallas import tpu as pltpu

F32 = jnp.float32
BF16 = jnp.bfloat16

D_MODEL = 1024
HEAD_DIM = 64
RWKV_DIM = 512
ATT_DIM = 512
ATT_HEADS = 8
KV_DIM = 128
LORA_DIM = 256
SHIFT_DIM = 3 * RWKV_DIM + LORA_DIM
PROJ_DIM = SHIFT_DIM + ATT_DIM + 2 * KV_DIM
WINDOW = 128
D_FF = 2816
CONV_WIDTH = 3
NORM_EPS = 1e-6
LNX_EPS = 64e-5
L2_EPS = 1e-12
MASK_VALUE = -1e30
LOG2E = float(np.log2(np.e))
NEG_DECAY_SCALE = float(-np.exp(-0.5) * np.log2(np.e))

LANES = 128
SUBLANES = 8
CHUNK = 64
SUB = 16
VMEM_LIMIT = 56 * 1024 * 1024

TM_IN = 512
IN_PARTS = 2
TM_FFN = 512
ATT_QBLOCKS = 2
FF_SPLITS = (0, 1536, D_FF)


def _dot(a, b):
    return jnp.dot(a.astype(BF16), b.astype(BF16), preferred_element_type=F32)


def _dot_nt(a, b):
    return lax.dot_general(a.astype(BF16), b.astype(BF16), (((1,), (1,)), ((), ())),
                           preferred_element_type=F32)


def _dot_tn(a, b):
    return lax.dot_general(a.astype(BF16), b.astype(BF16), (((0,), (0,)), ((), ())),
                           preferred_element_type=F32)


def _split2(x):
    hi = x.astype(BF16)
    lo = (x - hi.astype(F32)).astype(BF16)
    return hi, lo


def _rms_norm(x, g):
    return x * lax.rsqrt(jnp.mean(x * x, axis=-1, keepdims=True) + NORM_EPS) * g


def _in_proj_kernel(x_ref, xp_ref, xn_ref, ln1_ref, w_ref, muc_ref, mup_ref, mun_ref, w0_ref,
                    w2_ref, a0_ref, a2_ref, g2_ref, kk_ref, ka_ref, rk_ref, ones_ref, tri_ref,
                    at0_ref, rt0_ref, bt0_ref, kt0_ref, at1_ref, rt1_ref, bt1_ref, kt1_ref,
                    v_ref, pl0_ref, pl1_ref, g_ref, bonus_ref, q_ref, ka_o_ref, va_o_ref):
    i = pl.program_id(1)
    n_tiles = pl.num_programs(1)
    tm = x_ref.shape[1]
    hm = tm // IN_PARTS
    rows = hm + 2 * SUBLANES
    core = slice(SUBLANES, hm + SUBLANES)
    halo_lo = jnp.where(i > 0, xp_ref[0], 0.0)
    halo_hi = jnp.where(i < n_tiles - 1, xn_ref[0], 0.0)
    dir_outs = ((at0_ref, rt0_ref, bt0_ref, kt0_ref, pl0_ref),
                (at1_ref, rt1_ref, bt1_ref, kt1_ref, pl1_ref))

    def seg_sum(t):
        return _dot(t, ones_ref[...])

    def part(lo):
        out = slice(lo, lo + hm)
        before = halo_lo if lo == 0 else x_ref[0, lo - SUBLANES:lo, :]
        after = halo_hi if lo + hm == tm else x_ref[0, lo + hm:lo + hm + SUBLANES, :]
        x_ext = jnp.concatenate([before, x_ref[0, out, :], after], axis=0)
        h = _rms_norm(x_ext, ln1_ref[...]).astype(BF16)
        proj = lambda c0, c1: _dot(h, w_ref[:, c0:c1])
        p_codes = proj(3 * RWKV_DIM, SHIFT_DIM)
        p_k = proj(RWKV_DIM, 2 * RWKV_DIM)
        p_r = proj(0, RWKV_DIM)
        p_v = proj(2 * RWKV_DIM, 3 * RWKV_DIM)
        att = proj(SHIFT_DIM, PROJ_DIM)[core]
        yield

        def shifted(p_ext, c0, c1):
            prev = pltpu.roll(p_ext, 1, axis=0)[core]
            nxt = pltpu.roll(p_ext, rows - 1, axis=0)[core]
            return (p_ext[core] * muc_ref[:, c0:c1] + prev * mup_ref[:, c0:c1]
                    + nxt * mun_ref[:, c0:c1])

        codes = shifted(p_codes, 3 * RWKV_DIM, SHIFT_DIM)
        c_di = codes[:, :LANES]
        th_hi, th_lo = _split2(jnp.tanh(c_di))
        th_cat = jnp.concatenate([th_hi, th_lo, th_hi], axis=1)
        gate_code = jax.nn.sigmoid(codes[:, LANES:])
        k = shifted(p_k, RWKV_DIM, 2 * RWKV_DIM)
        kkr = k * kk_ref[...]
        kkr_sq = kkr * kkr
        r = shifted(p_r, 0, RWKV_DIM)
        v = shifted(p_v, 2 * RWKV_DIM, 3 * RWKV_DIM)
        v_ref[0, out, :] = v.astype(BF16)
        q_ref[0, out, :] = att[:, :ATT_DIM].astype(BF16)
        ka_o_ref[0, out, :] = att[:, ATT_DIM:ATT_DIM + KV_DIM].astype(BF16)
        va_o_ref[0, out, :] = att[:, ATT_DIM + KV_DIM:].astype(BF16)
        yield

        a_pre = _dot(c_di, a2_ref[...])
        g_ref[0, out, :] = _dot(gate_code, g2_ref[...]).astype(BF16)
        n2 = seg_sum(kkr_sq)
        z = [w0_ref[d:d + 1, :] + jnp.dot(th_cat, w2_ref[d], preferred_element_type=F32)
             for d in range(2)]
        yield

        a_vec = jax.nn.sigmoid(a0_ref[...] + a_pre)
        kk = kkr * lax.rsqrt(jnp.maximum(n2, L2_EPS * L2_EPS))
        k2 = k * (1.0 + (a_vec - 1.0) * ka_ref[...])
        b_vec = kk * a_vec
        neg_kk = -kk
        rk2 = r * k2 * rk_ref[...]
        lw = [NEG_DECAY_SCALE / (1.0 + jnp.exp2(z[d] * (-LOG2E))) for d in range(2)]
        lw_split = [_split2(lw[d]) for d in range(2)]
        yield

        bonus_sum = seg_sum(rk2)
        parts = [[jnp.dot(tri_ref[d],
                          jnp.concatenate([lw_split[d][0][cs], lw_split[d][1][cs]], axis=0),
                          preferred_element_type=F32)
                  for cs in (slice(j * CHUNK, (j + 1) * CHUNK) for j in range(hm // CHUNK))]
                 for d in range(2)]
        yield

        bonus_ref[0, out, :] = (bonus_sum * v).astype(BF16)
        for d, (at_ref, rt_ref, bt_ref, kt_ref, pl_ref) in enumerate(dir_outs):
            ci = jnp.concatenate(parts[d], axis=0)
            end = 0 if d else CHUNK - 1
            tot = jnp.concatenate([c[end:end + 1] for c in parts[d]], axis=0)
            pl_ref[0, 0, lo // CHUNK:(lo + hm) // CHUNK, :] = jnp.exp2(tot)
            e_inc = jnp.exp2(ci)
            e_exc = jnp.exp2(ci - lw[d])
            e_inv = 1.0 / e_inc
            at_ref[0, out, :] = (neg_kk * e_exc).astype(BF16)
            rt_ref[0, out, :] = (r * e_inc).astype(BF16)
            bt_ref[0, out, :] = (b_vec * e_inv).astype(BF16)
            kt_ref[0, out, :] = (k2 * e_inv).astype(BF16)

    live = [part(p * hm) for p in range(IN_PARTS)]
    while live:
        live = [g for g in live if next(g, StopIteration) is not StopIteration]
    if tm // CHUNK < SUBLANES:
        for d in range(2):
            dir_outs[d][4][0, 0, tm // CHUNK:, :] = jnp.ones((SUBLANES - tm // CHUNK, RWKV_DIM), F32)


def _in_proj(x, ln1_g, w_in, mu_cur, mu_prev, mu_next, decay_w0, w2_cat, iclr_a0, a2_pad, gate_g2,
             k_k, k_a, r_k, ones_blk, tri):
    B, T, D = x.shape
    tm = TM_IN
    nt = T // tm
    rows8 = tm // SUBLANES
    const = lambda shape: pl.BlockSpec(shape, lambda b, i: (0,) * len(shape))
    tok = lambda width: pl.BlockSpec((1, tm, width), lambda b, i: (b, i, 0))
    in_specs = [
        tok(D),
        pl.BlockSpec((1, SUBLANES, D), lambda b, i: (b, jnp.maximum(i * rows8 - 1, 0), 0)),
        pl.BlockSpec((1, SUBLANES, D), lambda b, i: (b, jnp.minimum((i + 1) * rows8, T // SUBLANES - 1), 0)),
        const((1, D)), const((D, PROJ_DIM)),
        const((1, SHIFT_DIM)), const((1, SHIFT_DIM)), const((1, SHIFT_DIM)),
        const((2, RWKV_DIM)), const((2, 3 * LANES, RWKV_DIM)),
        const((1, RWKV_DIM)), const((LANES, RWKV_DIM)), const((LANES, RWKV_DIM)),
        const((1, RWKV_DIM)), const((1, RWKV_DIM)), const((1, RWKV_DIM)),
        const((RWKV_DIM, RWKV_DIM)), const((2, CHUNK, 2 * CHUNK)),
    ]
    tok_bf = jax.ShapeDtypeStruct((B, T, RWKV_DIM), BF16)
    pl_shape = jax.ShapeDtypeStruct((B, nt, SUBLANES, RWKV_DIM), F32)
    pl_spec = pl.BlockSpec((1, 1, SUBLANES, RWKV_DIM), lambda b, i: (b, i, 0, 0))
    out_shape = [tok_bf] * 9 + [pl_shape, pl_shape, tok_bf, tok_bf,
                                jax.ShapeDtypeStruct((B, T, ATT_DIM), BF16),
                                jax.ShapeDtypeStruct((B, T, KV_DIM), BF16),
                                jax.ShapeDtypeStruct((B, T, KV_DIM), BF16)]
    out_specs = [tok(RWKV_DIM)] * 9 + [pl_spec, pl_spec, tok(RWKV_DIM), tok(RWKV_DIM),
                                       tok(ATT_DIM), tok(KV_DIM), tok(KV_DIM)]
    return pl.pallas_call(
        _in_proj_kernel, grid=(B, nt), in_specs=in_specs, out_specs=out_specs, out_shape=out_shape,
        compiler_params=pltpu.CompilerParams(dimension_semantics=("parallel", "parallel"),
                                             vmem_limit_bytes=VMEM_LIMIT),
        name="in_proj",
    )(x, x, x, ln1_g, w_in, mu_cur, mu_prev, mu_next, decay_w0, w2_cat, iclr_a0, a2_pad, gate_g2,
      k_k, k_a, r_k, ones_blk, tri)


def _pair_chunks(items, sub_blk, eye, lane0, bd_mask, outs):
    n = range(len(items))
    at, rt, bt, kt, v, p_last, s_prev, strict, incl = zip(*items)

    def bd(x):
        x = x.astype(BF16)
        zero = jnp.zeros_like(x)
        return jnp.concatenate([jnp.where(lane0, x, zero), jnp.where(lane0, zero, x)], axis=0)

    def pmm(x, y):
        return _dot(x, bd(y))

    sc = [_dot_nt(jnp.concatenate([at[i], rt[i]], axis=0),
                  jnp.concatenate([bd(bt[i]), bd(kt[i])], axis=0)) for i in n]
    a_ab = [jnp.where(strict[i], sc[i][:CHUNK, :LANES], 0.0) for i in n]
    a_ak = [jnp.where(strict[i], sc[i][:CHUNK, LANES:], 0.0) for i in n]
    a_rb = [jnp.where(incl[i], sc[i][CHUNK:, :LANES], 0.0) for i in n]
    a_rk = [jnp.where(incl[i], sc[i][CHUNK:, LANES:], 0.0) for i in n]
    yield

    a_d = [jnp.where(sub_blk, a_ab[i], 0.0) for i in n]
    a_o = [a_ab[i] - a_d[i] for i in n]
    a2 = [pmm(a_d[i], a_d[i]) for i in n]
    xy = [pmm(jnp.concatenate([a_ak[i], a_rk[i]], axis=0), v[i]) for i in n]
    x1 = [xy[i][:CHUNK] for i in n]
    yk = [xy[i][CHUNK:] for i in n]
    yield
    t1 = [eye + a_d[i] for i in n]
    st = [pmm(jnp.concatenate([a2[i], t1[i]], axis=0), a2[i]) for i in n]
    a4 = [st[i][:CHUNK] for i in n]
    t1 = [t1[i] + st[i][CHUNK:] for i in n]
    yield
    st = [pmm(jnp.concatenate([a4[i], t1[i]], axis=0), a4[i]) for i in n]
    a8 = [st[i][:CHUNK] for i in n]
    t1 = [t1[i] + st[i][CHUNK:] for i in n]
    yield
    t_d = [t1[i] + pmm(t1[i], a8[i]) for i in n]
    yield
    mx = [_dot(t_d[i], jnp.concatenate([bd(a_o[i]), bd(at[i]), bd(x1[i])], axis=1)) for i in n]
    m1 = [mx[i][:, :LANES] for i in n]
    xp = [mx[i][:, LANES:] for i in n]
    yield
    m2 = [pmm(m1[i], m1[i]) for i in n]
    yield
    q = [eye + m1[i] + m2[i] + pmm(m1[i], m2[i]) for i in n]
    yield
    wu = [_dot(q[i], jnp.concatenate([bd(xp[i][:, :LANES]), bd(xp[i][:, LANES:])], axis=1))
          for i in n]
    yield
    hs = [_dot_nt(jnp.concatenate([wu[i][:, :LANES].astype(BF16), rt[i]], axis=0), s_prev[i])
          for i in n]
    yield
    u = [hs[i][:CHUNK] + wu[i][:, LANES:] for i in n]
    y = [hs[i][CHUNK:] + pmm(a_rb[i], u[i]) + yk[i] for i in n]
    yield
    upd = [_dot_tn(jnp.concatenate([u[i].astype(BF16), v[i]], axis=0),
                   jnp.concatenate([bt[i], kt[i]], axis=0)) for i in n]
    s_new = [(s_prev[i] + jnp.where(bd_mask, upd[i], 0.0)) * p_last[i] for i in n]
    outs.extend(zip(y, s_new))


def _scan_attn_kernel(at0_ref, rt0_ref, bt0_ref, kt0_ref, v0_ref, pl0_ref,
                      at1_ref, rt1_ref, bt1_ref, kt1_ref, v1_ref, pl1_ref, *rest):
    attn_in = rest[:ATT_QBLOCKS + 8]
    yf_ref, yb_ref, o_att_ref, s_ref = rest[ATT_QBLOCKS + 8:]
    c = pl.program_id(0)

    @pl.when(c == 0)
    def _():
        s_ref[...] = jnp.zeros_like(s_ref)

    n_batch = v0_ref.shape[0]
    ri = lax.broadcasted_iota(jnp.int32, (CHUNK, LANES), 0)
    ci = lax.broadcasted_iota(jnp.int32, (CHUNK, LANES), 1)
    cj = jnp.where(ci >= CHUNK, ci - CHUNK, ci)
    lane0 = ci < CHUNK
    eye = jnp.where(ri == cj, 1.0, 0.0).astype(F32)
    sub_blk = (ri // SUB) == (cj // SUB)
    r2 = lax.broadcasted_iota(jnp.int32, (LANES, LANES), 0)
    c2 = lax.broadcasted_iota(jnp.int32, (LANES, LANES), 1)
    bd_mask = (r2 >= CHUNK) == (c2 >= CHUNK)
    dirs = ((at0_ref, rt0_ref, bt0_ref, kt0_ref, v0_ref, pl0_ref, ri > cj, ri >= cj),
            (at1_ref, rt1_ref, bt1_ref, kt1_ref, v1_ref, pl1_ref, ri < cj, ri <= cj))
    n_pairs = RWKV_DIM // LANES
    items = []
    for b in range(n_batch):
        for d, (at_ref, rt_ref, bt_ref, kt_ref, v_ref, pl_ref, strict, incl) in enumerate(dirs):
            for p in range(n_pairs):
                sl = slice(p * LANES, (p + 1) * LANES)
                items.append((at_ref[b, :, sl], rt_ref[b, :, sl], bt_ref[b, :, sl], kt_ref[b, :, sl],
                              v_ref[b, :, sl], pl_ref[b, 0, :, sl], s_ref[b, d, p], strict, incl))
    outs = []
    scan = _pair_chunks(items, sub_blk, eye, lane0, bd_mask, outs)
    attn = _attn_stages(*attn_in, o_att_ref)
    while next(scan, StopIteration) is not StopIteration:
        next(attn, None)
    for _ in attn:
        pass
    for b in range(n_batch):
        for d, y_ref in enumerate((yf_ref, yb_ref)):
            base = (b * 2 + d) * n_pairs
            y_ref[b] = jnp.concatenate([outs[base + p][0] for p in range(n_pairs)],
                                       axis=1).astype(BF16)
            for p in range(n_pairs):
                s_ref[b, d, p] = outs[base + p][1]


def _attn_stages(sink_ref, *refs):
    n_sub = ATT_QBLOCKS
    bias_refs = refs[:n_sub]
    q_ref, kp_ref, kc_ref, kn_ref, vp_ref, vc_ref, vn_ref, o_ref = refs[n_sub:]
    blk = WINDOW
    k_all = jnp.concatenate([kp_ref[0], kc_ref[0], kn_ref[0]], axis=0)
    v_all = jnp.concatenate([vp_ref[0], vc_ref[0], vn_ref[0]], axis=0)

    def swap_halves(x):
        return jnp.concatenate([x[:, HEAD_DIM:], x[:, :HEAD_DIM]], axis=1)

    lane0 = lax.broadcasted_iota(jnp.int32, k_all.shape, 1) < HEAD_DIM

    def variants(x):
        xs = swap_halves(x)
        zero = jnp.zeros_like(x)
        return ((jnp.where(lane0, x, zero), jnp.where(lane0, zero, xs)),
                (jnp.where(lane0, xs, zero), jnp.where(lane0, zero, x)))

    k_var = variants(k_all)
    v_var = variants(v_all)
    yield

    group = ATT_HEADS // (KV_DIM // HEAD_DIM)

    items = [(u, h) for u in range(n_sub) for h in range(ATT_HEADS)]
    keys = lambda var, u, h: var[h // group][h % 2][u * blk:(u + 3) * blk]
    q_pairs = [[q_ref[0, u * blk:(u + 1) * blk, j * LANES:(j + 1) * LANES]
                * jnp.asarray(HEAD_DIM ** -0.5, BF16) for j in range(ATT_DIM // LANES)]
               for u in range(n_sub)]
    s = [_dot_nt(q_pairs[u][h // 2], keys(k_var, u, h)) + bias_refs[u][0, h] for u, h in items]
    yield
    m = [jnp.maximum(jnp.max(s[i], axis=-1, keepdims=True), sink_ref[h]) for i, (u, h) in enumerate(items)]
    yield
    p = [jnp.exp(s[i] - m[i]) for i in range(len(items))]
    yield
    den = [jnp.sum(p[i], axis=-1, keepdims=True) + jnp.exp(sink_ref[h] - m[i])
           for i, (u, h) in enumerate(items)]
    yield
    o = [_dot(p[i], keys(v_var, u, h)) * (1.0 / den[i]) for i, (u, h) in enumerate(items)]
    yield
    for u in range(n_sub):
        ou = o[u * ATT_HEADS:(u + 1) * ATT_HEADS]
        o_ref[0, u * blk:(u + 1) * blk, :] = jnp.concatenate(
            [ou[2 * j] + ou[2 * j + 1] for j in range(ATT_DIM // LANES)], axis=1).astype(o_ref.dtype)


def _attn_bias(blk):
    qi = np.arange(blk)[:, None]
    kj = np.arange(3 * blk)[None, :]
    dist = np.abs(kj - blk - qi)
    slopes = 2.0 ** (-8.0 * np.arange(1, ATT_HEADS + 1, dtype=np.float32) / ATT_HEADS)
    alibi = -slopes[:, None, None].astype(np.float32) * dist[None].astype(np.float32)
    out = []
    for has_prev, has_next in ((False, True), (True, True), (True, False)):
        valid = (dist <= WINDOW) & (has_prev | (kj >= blk)) & (has_next | (kj < 2 * blk))
        out.append(np.where(valid[None], alibi, np.float32(MASK_VALUE)))
    return jnp.asarray(np.stack(out), F32)


def _scan_attention(at0, rt0, bt0, kt0, at1, rt1, bt1, kt1, v, pl0, pl1, q, ka, va, sink):
    B, T, C = v.shape
    nc = T // CHUNK
    blk = WINDOW
    n_sub = ATT_QBLOCKS
    per_row = T // (blk * n_sub)
    nb = T // blk
    assert B * per_row == nc, "scan and attention must have the same number of grid steps"
    fwd = pl.BlockSpec((B, CHUNK, C), lambda c: (0, c, 0))
    bwd = pl.BlockSpec((B, CHUNK, C), lambda c: (0, nc - 1 - c, 0))
    pl_f = pl.BlockSpec((B, 1, 1, C), lambda c: (0, c, 0, 0))
    pl_b = pl.BlockSpec((B, 1, 1, C), lambda c: (0, nc - 1 - c, 0, 0))
    y_shape = jax.ShapeDtypeStruct((B, T, C), BF16)

    row = lambda c: c // per_row
    step = lambda c: lax.rem(c, per_row)
    cur = lambda width: pl.BlockSpec((1, n_sub * blk, width), lambda c: (row(c), step(c), 0))
    prev = pl.BlockSpec((1, blk, KV_DIM), lambda c: (row(c), jnp.maximum(step(c) * n_sub - 1, 0), 0))
    nxt = pl.BlockSpec((1, blk, KV_DIM), lambda c: (row(c), jnp.minimum((step(c) + 1) * n_sub, nb - 1), 0))

    def bias_spec(u):
        def index(c):
            g = step(c) * n_sub + u
            return (jnp.where(g == 0, 0, jnp.where(g == nb - 1, 2, 1)), 0, 0, 0)
        return pl.BlockSpec((1, ATT_HEADS, blk, 3 * blk), index)

    bias = _attn_bias(blk)
    return pl.pallas_call(
        _scan_attn_kernel, grid=(nc,),
        in_specs=[fwd, fwd, fwd, fwd, fwd, pl_f, bwd, bwd, bwd, bwd, bwd, pl_b,
                  pl.BlockSpec(memory_space=pltpu.SMEM)] + [bias_spec(u) for u in range(n_sub)]
                 + [cur(ATT_DIM), prev, cur(KV_DIM), nxt, prev, cur(KV_DIM), nxt],
        out_specs=[fwd, bwd, cur(ATT_DIM)],
        out_shape=[y_shape, y_shape, jax.ShapeDtypeStruct((B, T, ATT_DIM), BF16)],
        scratch_shapes=[pltpu.VMEM((B, 2, C // LANES, LANES, LANES), F32)],
        compiler_params=pltpu.CompilerParams(dimension_semantics=("arbitrary",),
                                             vmem_limit_bytes=VMEM_LIMIT),
        name="scan_attn",
    )(at0, rt0, bt0, kt0, v, pl0, at1, rt1, bt1, kt1, v, pl1,
      sink, *([bias] * n_sub), q, ka, ka, ka, va, va, va)


HALO = 16


def _mix_ffn_kernel(*refs):
    (x_m, x_p, x_n, yf_m, yf_p, yf_n, yb_m, yb_p, yb_n, bo_m, bo_p, bo_n, g_m, g_p, g_n,
     oa_m, oa_p, oa_n, lg_ref, lb_ref, mean_ref, wo_ref, ln2_ref, wg_ref, wu_ref, cw_ref, cb_ref,
     wd_ref, lnf_ref, o_ref) = refs
    i = pl.program_id(1)
    n_tiles = pl.num_programs(1)
    tm = x_m.shape[1]
    rows = tm + 2 * HALO
    core = slice(HALO, tm + HALO)
    ext = lambda m, p, n: jnp.concatenate([p[0], m[0], n[0]], axis=0)

    def seg_mean(t):
        return _dot(t, mean_ref[...])

    y = ext(yf_m, yf_p, yf_n).astype(F32) + ext(yb_m, yb_p, yb_n).astype(F32)
    d = y - seg_mean(y)
    var = seg_mean(d * d)
    yn = d * lax.rsqrt(var + LNX_EPS) * lg_ref[...] + lb_ref[...]
    o_rwkv = (yn + ext(bo_m, bo_p, bo_n)) * ext(g_m, g_p, g_n)
    mix = _dot(o_rwkv, wo_ref[:RWKV_DIM, :]) + _dot(ext(oa_m, oa_p, oa_n), wo_ref[RWKV_DIM:, :])
    x1 = ext(x_m, x_p, x_n) + mix
    r = lax.broadcasted_iota(jnp.int32, (rows, 1), 0)
    inside = ((r >= HALO) | (i > 0)) & ((r < tm + HALO) | (i < n_tiles - 1))
    x1 = jnp.where(inside, x1, 0.0)

    h = _rms_norm(x1, ln2_ref[...]).astype(BF16)
    h_core = h[core]
    acc = jnp.zeros((tm, D_MODEL), F32)
    for c0, c1 in zip(FF_SPLITS[:-1], FF_SPLITS[1:]):
        cs = slice(c0, c1)
        gp_ext = _dot(h, wg_ref[:, cs])
        prev = pltpu.roll(gp_ext, 1, axis=0)[core]
        nxt = pltpu.roll(gp_ext, rows - 1, axis=0)[core]
        gate = (prev * cw_ref[0:1, cs] + gp_ext[core] * cw_ref[1:2, cs] + nxt * cw_ref[2:3, cs]
                + cb_ref[:, cs])
        act = 0.5 * gate * (1.0 + lax.erf(gate * float(1.0 / np.sqrt(2.0))))
        up = _dot(h_core, wu_ref[:, cs])
        acc = acc + _dot(act * up, wd_ref[cs, :])
    o_ref[0] = _rms_norm(x1[core] + acc, lnf_ref[...])


def _mix_ffn(x, yf, yb, bonus, g, o_att, lnx_g, lnx_b, mean_blk, w_out, ln2_g, wg, wu, conv_w, conv_b,
             wd, lnf_g):
    B, T, D = x.shape
    tm = TM_FFN
    per_tile = tm // HALO
    last = T // HALO - 1

    def tok(width):
        return [pl.BlockSpec((1, tm, width), lambda b, i: (b, i, 0)),
                pl.BlockSpec((1, HALO, width), lambda b, i: (b, jnp.maximum(i * per_tile - 1, 0), 0)),
                pl.BlockSpec((1, HALO, width), lambda b, i: (b, jnp.minimum((i + 1) * per_tile, last), 0))]

    const = lambda shape: pl.BlockSpec(shape, lambda b, i: (0,) * len(shape))
    resident = lambda shape: pl.BlockSpec(shape, lambda b, i: (0,) * len(shape),
                                          pipeline_mode=pl.Buffered(1))
    tokens = (x, yf, yb, bonus, g, o_att)
    return pl.pallas_call(
        _mix_ffn_kernel, grid=(B, T // tm),
        in_specs=[spec for a in tokens for spec in tok(a.shape[-1])]
                 + [const((1, RWKV_DIM)), const((1, RWKV_DIM)), const((RWKV_DIM, RWKV_DIM)), const((D, D)),
                    const((1, D)), resident((D, D_FF)), resident((D, D_FF)),
                    const((CONV_WIDTH, D_FF)), const((1, D_FF)), resident((D_FF, D)), const((1, D))],
        out_specs=pl.BlockSpec((1, tm, D), lambda b, i: (b, i, 0)),
        out_shape=jax.ShapeDtypeStruct((B, T, D), F32),
        compiler_params=pltpu.CompilerParams(dimension_semantics=("parallel", "parallel"),
                                             vmem_limit_bytes=VMEM_LIMIT),
        name="mix_ffn",
    )(*[a for a in tokens for _ in range(3)], lnx_g, lnx_b, mean_blk, w_out, ln2_g, wg, wu, conv_w,
      conv_b, wd, lnf_g)


def _constants():
    idx = np.arange(RWKV_DIM)
    same_head = (idx[:, None] // HEAD_DIM) == (idx[None, :] // HEAD_DIM)
    t = np.arange(CHUNK)
    tri = np.stack([t[:, None] >= t[None, :], t[:, None] <= t[None, :]]).astype(np.float32)
    return (jnp.asarray(same_head.astype(np.float32), BF16),
            jnp.asarray(same_head.astype(np.float32) / HEAD_DIM, BF16),
            jnp.asarray(np.concatenate([tri, tri], axis=2), BF16))


def kernel(x, ln1_g, w_in, shift_mu_prev, shift_mu_next, decay_w0, decay_w2, iclr_a0, iclr_a2,
           gate_g2, k_k, k_a, r_k, lnx_g, lnx_b, attn_sink, w_out, ln2_g, ffn_w_gate, ffn_w_up,
           ffn_conv_w, ffn_conv_b, ffn_w_down, lnf_g):
    B, T, _ = x.shape
    assert w_in.shape[0] == 1, "single-layer block"
    l = 0
    ones_blk, mean_blk, tri = _constants()
    row = lambda a: a.reshape(1, -1)
    w2 = decay_w2[l]
    w2_pad = jnp.concatenate([w2, jnp.zeros_like(w2)], axis=1)
    w2hi = w2_pad.astype(BF16)
    w2lo = (w2_pad - w2hi.astype(F32)).astype(BF16)
    w2_cat = jnp.concatenate([w2hi, w2hi, w2lo], axis=1)
    mu_p, mu_n = shift_mu_prev[l], shift_mu_next[l]
    a2_pad = jnp.concatenate([jnp.zeros_like(iclr_a2[l]), iclr_a2[l]], axis=0).astype(BF16)
    (at0, rt0, bt0, kt0, at1, rt1, bt1, kt1, v, pl0, pl1, g, bonus, q, ka, va) = _in_proj(
        x, row(ln1_g[l]), w_in[l].astype(BF16), row(1.0 - mu_p - mu_n), row(mu_p), row(mu_n),
        decay_w0[l], w2_cat, row(iclr_a0[l]), a2_pad, gate_g2[l].astype(BF16),
        row(k_k[l]), row(k_a[l]), row(r_k[l]), ones_blk, tri)
    cpt = TM_IN // CHUNK
    pl0 = pl0[:, :, :cpt].reshape(B, T // CHUNK, 1, RWKV_DIM)
    pl1 = pl1[:, :, :cpt].reshape(B, T // CHUNK, 1, RWKV_DIM)
    yf, yb, o_att = _scan_attention(at0, rt0, bt0, kt0, at1, rt1, bt1, kt1, v, pl0, pl1,
                                    q, ka, va, attn_sink[l])
    return _mix_ffn(x, yf, yb, bonus, g, o_att, row(lnx_g[l]), row(lnx_b[l]), mean_blk,
                    w_out[l].astype(BF16), row(ln2_g[l]), ffn_w_gate[l].astype(BF16),
                    ffn_w_up[l].astype(BF16), ffn_conv_w[l], row(ffn_conv_b[l]),
                    ffn_w_down[l].astype(BF16), row(lnf_g))
```

```python
import functools

import numpy as np
import jax
import jax.numpy as jnp
from jax import lax
from jax.experimental import pallas as pl
from jax.experimental.pallas import tpu as pltpu

F32 = jnp.float32
BF16 = jnp.bfloat16

D_MODEL = 1024
HEAD_DIM = 64
RWKV_DIM = 512
ATT_DIM = 512
ATT_HEADS = 8
KV_DIM = 128
LORA_DIM = 256
SHIFT_DIM = 3 * RWKV_DIM + LORA_DIM
PROJ_DIM = SHIFT_DIM + ATT_DIM + 2 * KV_DIM
WINDOW = 128
D_FF = 2816
CONV_WIDTH = 3
NORM_EPS = 1e-6
LNX_EPS = 64e-5
L2_EPS = 1e-12
MASK_VALUE = -1e30
LOG2E = float(np.log2(np.e))
NEG_DECAY_SCALE = float(-np.exp(-0.5) * np.log2(np.e))

LANES = 128
SUBLANES = 8
BF16_ROWS = 16
CHUNK = 64
SUB = 16
VMEM_LIMIT = 56 * 1024 * 1024

TM_IN = 512
IN_PARTS = 2
TM_FFN = 512
ATT_QBLOCKS = 2
FF_SPLITS = (0, 1536, D_FF)


def _dot(a, b):
    return jnp.dot(a.astype(BF16), b.astype(BF16), preferred_element_type=F32)


def _dot_nt(a, b):
    return lax.dot_general(a.astype(BF16), b.astype(BF16), (((1,), (1,)), ((), ())),
                           preferred_element_type=F32)


def _dot_tn(a, b):
    return lax.dot_general(a.astype(BF16), b.astype(BF16), (((0,), (0,)), ((), ())),
                           preferred_element_type=F32)


def _split2(x):
    hi = x.astype(BF16)
    lo = (x - hi.astype(F32)).astype(BF16)
    return hi, lo


def _rms_norm(x, g):
    return x * lax.rsqrt(jnp.mean(x * x, axis=-1, keepdims=True) + NORM_EPS) * g


def _in_proj_kernel(x_ref, xp_ref, xn_ref, ln1_ref, w_ref, muc_ref, mup_ref, mun_ref, w0_ref,
                    w2_ref, a0_ref, a2_ref, g2_ref, kk_ref, ka_ref, rk_ref, ones_ref, tri_ref,
                    at0_ref, rt0_ref, bt0_ref, kt0_ref, at1_ref, rt1_ref, bt1_ref, kt1_ref,
                    v_ref, pl0_ref, pl1_ref, g_ref, bonus_ref, q_ref, ka_o_ref, va_o_ref, wbf_ref):
    i = pl.program_id(1)
    n_tiles = pl.num_programs(1)
    tm = x_ref.shape[1]

    @pl.when((pl.program_id(0) == 0) & (i == 0))
    def _():
        wbf_ref[...] = w_ref[...].astype(BF16)

    hm = tm // IN_PARTS
    rows = hm + 2 * SUBLANES
    core = slice(SUBLANES, hm + SUBLANES)
    halo_lo = jnp.where(i > 0, xp_ref[0], 0.0)
    halo_hi = jnp.where(i < n_tiles - 1, xn_ref[0], 0.0)
    dir_outs = ((at0_ref, rt0_ref, bt0_ref, kt0_ref, pl0_ref),
                (at1_ref, rt1_ref, bt1_ref, kt1_ref, pl1_ref))

    def seg_sum(t):
        return _dot(t, ones_ref[...])

    def part(lo):
        out = slice(lo, lo + hm)
        before = halo_lo if lo == 0 else x_ref[0, lo - SUBLANES:lo, :]
        after = halo_hi if lo + hm == tm else x_ref[0, lo + hm:lo + hm + SUBLANES, :]
        x_ext = jnp.concatenate([before, x_ref[0, out, :], after], axis=0)
        h = _rms_norm(x_ext, ln1_ref[...]).astype(BF16)
        proj = lambda c0, c1: _dot(h, wbf_ref[:, c0:c1])
        p_codes = proj(3 * RWKV_DIM, SHIFT_DIM)
        p_k = proj(RWKV_DIM, 2 * RWKV_DIM)
        p_r = proj(0, RWKV_DIM)
        p_v = proj(2 * RWKV_DIM, 3 * RWKV_DIM)
        att = proj(SHIFT_DIM, PROJ_DIM)[core]
        yield

        def shifted(p_ext, c0, c1):
            prev = pltpu.roll(p_ext, 1, axis=0)[core]
            nxt = pltpu.roll(p_ext, rows - 1, axis=0)[core]
            return (p_ext[core] * muc_ref[:, c0:c1] + prev * mup_ref[:, c0:c1]
                    + nxt * mun_ref[:, c0:c1])

        codes = shifted(p_codes, 3 * RWKV_DIM, SHIFT_DIM)
        c_di = codes[:, :LANES]
        th_hi, th_lo = _split2(jnp.tanh(c_di))
        th_cat = jnp.concatenate([th_hi, th_lo], axis=1)
        gate_code = jax.nn.sigmoid(codes[:, LANES:])
        k = shifted(p_k, RWKV_DIM, 2 * RWKV_DIM)
        kkr = k * kk_ref[...]
        kkr_sq = kkr * kkr
        r = shifted(p_r, 0, RWKV_DIM)
        v = shifted(p_v, 2 * RWKV_DIM, 3 * RWKV_DIM)
        v_ref[0, out, :] = v.astype(BF16)
        q_ref[0, out, :] = att[:, :ATT_DIM].astype(BF16)
        ka_o_ref[0, out, :] = att[:, ATT_DIM:ATT_DIM + KV_DIM].astype(BF16)
        va_o_ref[0, out, :] = att[:, ATT_DIM + KV_DIM:].astype(BF16)
        yield

        a_pre = _dot(c_di, a2_ref[...])
        g_ref[0, out, :] = _dot(gate_code, g2_ref[...]).astype(BF16)
        n2 = seg_sum(kkr_sq)
        z = [w0_ref[d:d + 1, :] + jnp.dot(th_cat, w2_ref[d], preferred_element_type=F32)
             for d in range(2)]
        yield

        a_vec = jax.nn.sigmoid(a0_ref[...] + a_pre)
        kk = kkr * lax.rsqrt(jnp.maximum(n2, L2_EPS * L2_EPS))
        k2 = k * (1.0 + (a_vec - 1.0) * ka_ref[...])
        b_vec = kk * a_vec
        neg_kk = -kk
        rk2 = r * k2 * rk_ref[...]
        lw = [NEG_DECAY_SCALE / (1.0 + jnp.exp2(z[d] * (-LOG2E))) for d in range(2)]
        lw_split = [_split2(lw[d]) for d in range(2)]
        yield

        bonus_sum = seg_sum(rk2)
        parts = [[jnp.dot(tri_ref[d],
                          jnp.concatenate([lw_split[d][0][cs], lw_split[d][1][cs]], axis=0),
                          preferred_element_type=F32)
                  for cs in (slice(j * CHUNK, (j + 1) * CHUNK) for j in range(hm // CHUNK))]
                 for d in range(2)]
        yield

        bonus_ref[0, out, :] = (bonus_sum * v).astype(BF16)
        for d, (at_ref, rt_ref, bt_ref, kt_ref, pl_ref) in enumerate(dir_outs):
            ci = jnp.concatenate(parts[d], axis=0)
            end = 0 if d else CHUNK - 1
            tot = jnp.concatenate([c[end:end + 1] for c in parts[d]], axis=0)
            pl_ref[0, 0, lo // CHUNK:(lo + hm) // CHUNK, :] = jnp.exp2(tot)
            e_inc = jnp.exp2(ci)
            e_exc = jnp.exp2(ci - lw[d])
            e_inv = 1.0 / e_inc
            at_ref[0, out, :] = (neg_kk * e_exc).astype(BF16)
            rt_ref[0, out, :] = (r * e_inc).astype(BF16)
            bt_ref[0, out, :] = (b_vec * e_inv).astype(BF16)
            kt_ref[0, out, :] = (k2 * e_inv).astype(BF16)

    live = [part(p * hm) for p in range(IN_PARTS)]
    while live:
        live = [g for g in live if next(g, StopIteration) is not StopIteration]
    if tm // CHUNK < SUBLANES:
        for d in range(2):
            dir_outs[d][4][0, 0, tm // CHUNK:, :] = jnp.ones((SUBLANES - tm // CHUNK, RWKV_DIM), F32)


def _in_proj(x, ln1_g, w_in, mu_cur, mu_prev, mu_next, decay_w0, w2_cat, iclr_a0, a2_pad, gate_g2,
             k_k, k_a, r_k, ones_blk, tri):
    B, T, D = x.shape
    tm = TM_IN
    nt = T // tm
    rows8 = tm // SUBLANES
    const = lambda shape: pl.BlockSpec(shape, lambda b, i: (0,) * len(shape))
    tok = lambda width: pl.BlockSpec((1, tm, width), lambda b, i: (b, i, 0))
    in_specs = [
        tok(D),
        pl.BlockSpec((1, SUBLANES, D), lambda b, i: (b, jnp.maximum(i * rows8 - 1, 0), 0)),
        pl.BlockSpec((1, SUBLANES, D), lambda b, i: (b, jnp.minimum((i + 1) * rows8, T // SUBLANES - 1), 0)),
        const((1, D)),
        pl.BlockSpec((D, PROJ_DIM), lambda b, i: (0, 0), pipeline_mode=pl.Buffered(1)),
        const((1, SHIFT_DIM)), const((1, SHIFT_DIM)), const((1, SHIFT_DIM)),
        const((2, RWKV_DIM)), const((2, 2 * LANES, RWKV_DIM)),
        const((1, RWKV_DIM)), const((LANES, RWKV_DIM)), const((LANES, RWKV_DIM)),
        const((1, RWKV_DIM)), const((1, RWKV_DIM)), const((1, RWKV_DIM)),
        const((RWKV_DIM, RWKV_DIM)), const((2, CHUNK, 2 * CHUNK)),
    ]
    tok_bf = jax.ShapeDtypeStruct((B, T, RWKV_DIM), BF16)
    pl_shape = jax.ShapeDtypeStruct((B, nt, SUBLANES, RWKV_DIM), F32)
    pl_spec = pl.BlockSpec((1, 1, SUBLANES, RWKV_DIM), lambda b, i: (b, i, 0, 0))
    out_shape = [tok_bf] * 9 + [pl_shape, pl_shape, tok_bf, tok_bf,
                                jax.ShapeDtypeStruct((B, T, ATT_DIM), BF16),
                                jax.ShapeDtypeStruct((B, T, KV_DIM), BF16),
                                jax.ShapeDtypeStruct((B, T, KV_DIM), BF16)]
    out_specs = [tok(RWKV_DIM)] * 9 + [pl_spec, pl_spec, tok(RWKV_DIM), tok(RWKV_DIM),
                                       tok(ATT_DIM), tok(KV_DIM), tok(KV_DIM)]
    return pl.pallas_call(
        _in_proj_kernel, grid=(B, nt), in_specs=in_specs, out_specs=out_specs, out_shape=out_shape,
        scratch_shapes=[pltpu.VMEM((D, PROJ_DIM), BF16)],
        compiler_params=pltpu.CompilerParams(dimension_semantics=("arbitrary", "arbitrary"),
                                             vmem_limit_bytes=VMEM_LIMIT),
        name="in_proj",
    )(x, x, x, ln1_g, w_in, mu_cur, mu_prev, mu_next, decay_w0, w2_cat, iclr_a0, a2_pad, gate_g2,
      k_k, k_a, r_k, ones_blk, tri)


def _pair_chunks(items, sub_blk, eye, lane0, bd_mask, outs):
    n = range(len(items))
    at, rt, bt, kt, v, p_last, s_prev, strict, incl = zip(*items)

    def bd(x):
        x = x.astype(BF16)
        zero = jnp.zeros_like(x)
        return jnp.concatenate([jnp.where(lane0, x, zero), jnp.where(lane0, zero, x)], axis=0)

    def pmm(x, y):
        return _dot(x, bd(y))

    sc = [_dot_nt(jnp.concatenate([at[i], rt[i]], axis=0),
                  jnp.concatenate([bd(bt[i]), bd(kt[i])], axis=0)) for i in n]
    a_ab = [jnp.where(strict[i], sc[i][:CHUNK, :LANES], 0.0) for i in n]
    a_ak = [jnp.where(strict[i], sc[i][:CHUNK, LANES:], 0.0) for i in n]
    a_rb = [jnp.where(incl[i], sc[i][CHUNK:, :LANES], 0.0) for i in n]
    a_rk = [jnp.where(incl[i], sc[i][CHUNK:, LANES:], 0.0) for i in n]
    yield

    a_d = [jnp.where(sub_blk, a_ab[i], 0.0) for i in n]
    a_o = [a_ab[i] - a_d[i] for i in n]
    a2 = [pmm(a_d[i], a_d[i]) for i in n]
    xy = [pmm(jnp.concatenate([a_ak[i], a_rk[i]], axis=0), v[i]) for i in n]
    x1 = [xy[i][:CHUNK] for i in n]
    yk = [xy[i][CHUNK:] for i in n]
    yield
    t1 = [eye + a_d[i] for i in n]
    st = [pmm(jnp.concatenate([a2[i], t1[i]], axis=0), a2[i]) for i in n]
    a4 = [st[i][:CHUNK] for i in n]
    t1 = [t1[i] + st[i][CHUNK:] for i in n]
    yield
    st = [pmm(jnp.concatenate([a4[i], t1[i]], axis=0), a4[i]) for i in n]
    a8 = [st[i][:CHUNK] for i in n]
    t1 = [t1[i] + st[i][CHUNK:] for i in n]
    yield
    t_d = [t1[i] + pmm(t1[i], a8[i]) for i in n]
    yield
    mx = [_dot(t_d[i], jnp.concatenate([bd(a_o[i]), bd(at[i]), bd(x1[i])], axis=1)) for i in n]
    m1 = [mx[i][:, :LANES] for i in n]
    xp = [mx[i][:, LANES:] for i in n]
    yield
    m2 = [pmm(m1[i], m1[i]) for i in n]
    yield
    q = [eye + m1[i] + m2[i] + pmm(m1[i], m2[i]) for i in n]
    yield
    wu = [_dot(q[i], jnp.concatenate([bd(xp[i][:, :LANES]), bd(xp[i][:, LANES:])], axis=1))
          for i in n]
    yield
    hs = [_dot_nt(jnp.concatenate([wu[i][:, :LANES].astype(BF16), rt[i]], axis=0), s_prev[i])
          for i in n]
    yield
    u = [hs[i][:CHUNK] + wu[i][:, LANES:] for i in n]
    y = [hs[i][CHUNK:] + pmm(a_rb[i], u[i]) + yk[i] for i in n]
    yield
    upd = [_dot_tn(jnp.concatenate([u[i].astype(BF16), v[i]], axis=0),
                   jnp.concatenate([bt[i], kt[i]], axis=0)) for i in n]
    s_new = [(s_prev[i] + jnp.where(bd_mask, upd[i], 0.0)) * p_last[i] for i in n]
    outs.extend(zip(y, s_new))


def _scan_attn_kernel(at0_ref, rt0_ref, bt0_ref, kt0_ref, v0_ref, pl0_ref,
                      at1_ref, rt1_ref, bt1_ref, kt1_ref, v1_ref, pl1_ref, *rest, n_cast):
    n_attn = ATT_QBLOCKS + 8
    attn_in = rest[:n_attn]
    cast_in = rest[n_attn:n_attn + n_cast]
    yf_ref, yb_ref, o_att_ref = rest[n_attn + n_cast:n_attn + n_cast + 3]
    cast_out = rest[n_attn + n_cast + 3:n_attn + 2 * n_cast + 3]
    s_ref = rest[-1]
    c = pl.program_id(0)
    for src_ref, dst_ref in zip(cast_in, cast_out):
        dst_ref[...] = src_ref[...].astype(BF16)

    @pl.when(c == 0)
    def _():
        s_ref[...] = jnp.zeros_like(s_ref)

    n_batch = v0_ref.shape[0]
    ri = lax.broadcasted_iota(jnp.int32, (CHUNK, LANES), 0)
    ci = lax.broadcasted_iota(jnp.int32, (CHUNK, LANES), 1)
    cj = jnp.where(ci >= CHUNK, ci - CHUNK, ci)
    lane0 = ci < CHUNK
    eye = jnp.where(ri == cj, 1.0, 0.0).astype(F32)
    sub_blk = (ri // SUB) == (cj // SUB)
    r2 = lax.broadcasted_iota(jnp.int32, (LANES, LANES), 0)
    c2 = lax.broadcasted_iota(jnp.int32, (LANES, LANES), 1)
    bd_mask = (r2 >= CHUNK) == (c2 >= CHUNK)
    dirs = ((at0_ref, rt0_ref, bt0_ref, kt0_ref, v0_ref, pl0_ref, ri > cj, ri >= cj),
            (at1_ref, rt1_ref, bt1_ref, kt1_ref, v1_ref, pl1_ref, ri < cj, ri <= cj))
    n_pairs = RWKV_DIM // LANES
    items = []
    for b in range(n_batch):
        for d, (at_ref, rt_ref, bt_ref, kt_ref, v_ref, pl_ref, strict, incl) in enumerate(dirs):
            for p in range(n_pairs):
                sl = slice(p * LANES, (p + 1) * LANES)
                items.append((at_ref[b, :, sl], rt_ref[b, :, sl], bt_ref[b, :, sl], kt_ref[b, :, sl],
                              v_ref[b, :, sl], pl_ref[b, 0, :, sl], s_ref[b, d, p], strict, incl))
    outs = []
    scan = _pair_chunks(items, sub_blk, eye, lane0, bd_mask, outs)
    attn = _attn_stages(*attn_in, o_att_ref)
    while next(scan, StopIteration) is not StopIteration:
        next(attn, None)
    for _ in attn:
        pass
    for b in range(n_batch):
        for d, y_ref in enumerate((yf_ref, yb_ref)):
            base = (b * 2 + d) * n_pairs
            y_ref[b] = jnp.concatenate([outs[base + p][0] for p in range(n_pairs)],
                                       axis=1).astype(BF16)
            for p in range(n_pairs):
                s_ref[b, d, p] = outs[base + p][1]


def _attn_stages(sink_ref, *refs):
    n_sub = ATT_QBLOCKS
    bias_refs = refs[:n_sub]
    q_ref, kp_ref, kc_ref, kn_ref, vp_ref, vc_ref, vn_ref, o_ref = refs[n_sub:]
    blk = WINDOW
    k_all = jnp.concatenate([kp_ref[0], kc_ref[0], kn_ref[0]], axis=0)
    v_all = jnp.concatenate([vp_ref[0], vc_ref[0], vn_ref[0]], axis=0)

    def swap_halves(x):
        return jnp.concatenate([x[:, HEAD_DIM:], x[:, :HEAD_DIM]], axis=1)

    lane0 = lax.broadcasted_iota(jnp.int32, k_all.shape, 1) < HEAD_DIM

    def variants(x):
        xs = swap_halves(x)
        zero = jnp.zeros_like(x)
        return ((jnp.where(lane0, x, zero), jnp.where(lane0, zero, xs)),
                (jnp.where(lane0, xs, zero), jnp.where(lane0, zero, x)))

    k_var = variants(k_all)
    v_var = variants(v_all)
    yield

    group = ATT_HEADS // (KV_DIM // HEAD_DIM)

    items = [(u, h) for u in range(n_sub) for h in range(ATT_HEADS)]
    keys = lambda var, u, h: var[h // group][h % 2][u * blk:(u + 3) * blk]
    q_pairs = [[q_ref[0, u * blk:(u + 1) * blk, j * LANES:(j + 1) * LANES]
                * jnp.asarray(HEAD_DIM ** -0.5, BF16) for j in range(ATT_DIM // LANES)]
               for u in range(n_sub)]
    s = [_dot_nt(q_pairs[u][h // 2], keys(k_var, u, h)) + bias_refs[u][0, h] for u, h in items]
    yield
    m = [jnp.maximum(jnp.max(s[i], axis=-1, keepdims=True), sink_ref[h]) for i, (u, h) in enumerate(items)]
    yield
    p = [jnp.exp(s[i] - m[i]) for i in range(len(items))]
    yield
    den = [jnp.sum(p[i], axis=-1, keepdims=True) + jnp.exp(sink_ref[h] - m[i])
           for i, (u, h) in enumerate(items)]
    yield
    o = [_dot(p[i], keys(v_var, u, h)) * (1.0 / den[i]) for i, (u, h) in enumerate(items)]
    yield
    for u in range(n_sub):
        ou = o[u * ATT_HEADS:(u + 1) * ATT_HEADS]
        o_ref[0, u * blk:(u + 1) * blk, :] = jnp.concatenate(
            [ou[2 * j] + ou[2 * j + 1] for j in range(ATT_DIM // LANES)], axis=1).astype(o_ref.dtype)


def _attn_bias(blk):
    qi = np.arange(blk)[:, None]
    kj = np.arange(3 * blk)[None, :]
    dist = np.abs(kj - blk - qi)
    slopes = 2.0 ** (-8.0 * np.arange(1, ATT_HEADS + 1, dtype=np.float32) / ATT_HEADS)
    alibi = -slopes[:, None, None].astype(np.float32) * dist[None].astype(np.float32)
    out = []
    for has_prev, has_next in ((False, True), (True, True), (True, False)):
        valid = (dist <= WINDOW) & (has_prev | (kj >= blk)) & (has_next | (kj < 2 * blk))
        out.append(np.where(valid[None], alibi, np.float32(MASK_VALUE)))
    return jnp.asarray(np.stack(out), F32)


def _scan_attention(at0, rt0, bt0, kt0, at1, rt1, bt1, kt1, v, pl0, pl1, q, ka, va, sink, weights):
    B, T, C = v.shape
    nc = T // CHUNK
    blk = WINDOW
    n_sub = ATT_QBLOCKS
    per_row = T // (blk * n_sub)
    nb = T // blk
    assert B * per_row == nc, "scan and attention must have the same number of grid steps"
    fwd = pl.BlockSpec((B, CHUNK, C), lambda c: (0, c, 0))
    bwd = pl.BlockSpec((B, CHUNK, C), lambda c: (0, nc - 1 - c, 0))
    pl_f = pl.BlockSpec((B, 1, 1, C), lambda c: (0, c, 0, 0))
    pl_b = pl.BlockSpec((B, 1, 1, C), lambda c: (0, nc - 1 - c, 0, 0))
    y_shape = jax.ShapeDtypeStruct((B, T, C), BF16)

    row = lambda c: c // per_row
    step = lambda c: lax.rem(c, per_row)
    cur = lambda width: pl.BlockSpec((1, n_sub * blk, width), lambda c: (row(c), step(c), 0))
    prev = pl.BlockSpec((1, blk, KV_DIM), lambda c: (row(c), jnp.maximum(step(c) * n_sub - 1, 0), 0))
    nxt = pl.BlockSpec((1, blk, KV_DIM), lambda c: (row(c), jnp.minimum((step(c) + 1) * n_sub, nb - 1), 0))

    def bias_spec(u):
        def index(c):
            g = step(c) * n_sub + u
            return (jnp.where(g == 0, 0, jnp.where(g == nb - 1, 2, 1)), 0, 0, 0)
        return pl.BlockSpec((1, ATT_HEADS, blk, 3 * blk), index)

    def cast_spec(w):
        rows, cols = w.shape
        per_step = next(r for r in range(BF16_ROWS, rows + 1, BF16_ROWS)
                        if rows % r == 0 and rows // r <= nc)
        last = rows // per_step - 1
        return pl.BlockSpec((per_step, cols), lambda c: (jnp.minimum(c, last), 0))

    cast_specs = [cast_spec(w) for w in weights]
    bias = _attn_bias(blk)
    outs = pl.pallas_call(
        functools.partial(_scan_attn_kernel, n_cast=len(weights)), grid=(nc,),
        in_specs=[fwd, fwd, fwd, fwd, fwd, pl_f, bwd, bwd, bwd, bwd, bwd, pl_b,
                  pl.BlockSpec(memory_space=pltpu.SMEM)] + [bias_spec(u) for u in range(n_sub)]
                 + [cur(ATT_DIM), prev, cur(KV_DIM), nxt, prev, cur(KV_DIM), nxt] + cast_specs,
        out_specs=[fwd, bwd, cur(ATT_DIM)] + cast_specs,
        out_shape=[y_shape, y_shape, jax.ShapeDtypeStruct((B, T, ATT_DIM), BF16)]
                  + [jax.ShapeDtypeStruct(w.shape, BF16) for w in weights],
        scratch_shapes=[pltpu.VMEM((B, 2, C // LANES, LANES, LANES), F32)],
        compiler_params=pltpu.CompilerParams(dimension_semantics=("arbitrary",),
                                             vmem_limit_bytes=VMEM_LIMIT),
        name="scan_attn",
    )(at0, rt0, bt0, kt0, v, pl0, at1, rt1, bt1, kt1, v, pl1,
      sink, *([bias] * n_sub), q, ka, ka, ka, va, va, va, *weights)
    return outs[0], outs[1], outs[2], outs[3:]


HALO = 16


def _mix_ffn_kernel(*refs):
    (x_m, x_p, x_n, yf_m, yf_p, yf_n, yb_m, yb_p, yb_n, bo_m, bo_p, bo_n, g_m, g_p, g_n,
     oa_m, oa_p, oa_n, lg_ref, lb_ref, mean_ref, wo_ref, ln2_ref, wg_ref, wu_ref, cw_ref, cb_ref,
     wd_ref, lnf_ref, o_ref) = refs
    i = pl.program_id(1)
    n_tiles = pl.num_programs(1)
    tm = x_m.shape[1]
    rows = tm + 2 * HALO
    core = slice(HALO, tm + HALO)
    ext = lambda m, p, n: jnp.concatenate([p[0], m[0], n[0]], axis=0)

    def seg_mean(t):
        return _dot(t, mean_ref[...])

    y = ext(yf_m, yf_p, yf_n).astype(F32) + ext(yb_m, yb_p, yb_n).astype(F32)
    d = y - seg_mean(y)
    var = seg_mean(d * d)
    yn = d * lax.rsqrt(var + LNX_EPS) * lg_ref[...] + lb_ref[...]
    o_rwkv = (yn + ext(bo_m, bo_p, bo_n)) * ext(g_m, g_p, g_n)
    mix = _dot(o_rwkv, wo_ref[:RWKV_DIM, :]) + _dot(ext(oa_m, oa_p, oa_n), wo_ref[RWKV_DIM:, :])
    x1 = ext(x_m, x_p, x_n) + mix
    r = lax.broadcasted_iota(jnp.int32, (rows, 1), 0)
    inside = ((r >= HALO) | (i > 0)) & ((r < tm + HALO) | (i < n_tiles - 1))
    x1 = jnp.where(inside, x1, 0.0)

    h = _rms_norm(x1, ln2_ref[...]).astype(BF16)
    h_core = h[core]
    acc = jnp.zeros((tm, D_MODEL), F32)
    for c0, c1 in zip(FF_SPLITS[:-1], FF_SPLITS[1:]):
        cs = slice(c0, c1)
        gp_ext = _dot(h, wg_ref[:, cs])
        prev = pltpu.roll(gp_ext, 1, axis=0)[core]
        nxt = pltpu.roll(gp_ext, rows - 1, axis=0)[core]
        gate = (prev * cw_ref[0:1, cs] + gp_ext[core] * cw_ref[1:2, cs] + nxt * cw_ref[2:3, cs]
                + cb_ref[:, cs])
        act = 0.5 * gate * (1.0 + lax.erf(gate * float(1.0 / np.sqrt(2.0))))
        up = _dot(h_core, wu_ref[:, cs])
        acc = acc + _dot(act * up, wd_ref[cs, :])
    o_ref[0] = _rms_norm(x1[core] + acc, lnf_ref[...])


def _mix_ffn(x, yf, yb, bonus, g, o_att, lnx_g, lnx_b, mean_blk, w_out, ln2_g, wg, wu, conv_w, conv_b,
             wd, lnf_g):
    B, T, D = x.shape
    tm = TM_FFN
    per_tile = tm // HALO
    last = T // HALO - 1

    def tok(width):
        return [pl.BlockSpec((1, tm, width), lambda b, i: (b, i, 0)),
                pl.BlockSpec((1, HALO, width), lambda b, i: (b, jnp.maximum(i * per_tile - 1, 0), 0)),
                pl.BlockSpec((1, HALO, width), lambda b, i: (b, jnp.minimum((i + 1) * per_tile, last), 0))]

    const = lambda shape: pl.BlockSpec(shape, lambda b, i: (0,) * len(shape))
    resident = lambda shape: pl.BlockSpec(shape, lambda b, i: (0,) * len(shape),
                                          pipeline_mode=pl.Buffered(1))
    tokens = (x, yf, yb, bonus, g, o_att)
    return pl.pallas_call(
        _mix_ffn_kernel, grid=(B, T // tm),
        in_specs=[spec for a in tokens for spec in tok(a.shape[-1])]
                 + [const((1, RWKV_DIM)), const((1, RWKV_DIM)), const((RWKV_DIM, RWKV_DIM)), const((D, D)),
                    const((1, D)), resident((D, D_FF)), resident((D, D_FF)),
                    const((CONV_WIDTH, D_FF)), const((1, D_FF)), resident((D_FF, D)), const((1, D))],
        out_specs=pl.BlockSpec((1, tm, D), lambda b, i: (b, i, 0)),
        out_shape=jax.ShapeDtypeStruct((B, T, D), F32),
        compiler_params=pltpu.CompilerParams(dimension_semantics=("parallel", "parallel"),
                                             vmem_limit_bytes=VMEM_LIMIT),
        name="mix_ffn",
    )(*[a for a in tokens for _ in range(3)], lnx_g, lnx_b, mean_blk, w_out, ln2_g, wg, wu, conv_w,
      conv_b, wd, lnf_g)


def _constants():
    idx = np.arange(RWKV_DIM)
    same_head = (idx[:, None] // HEAD_DIM) == (idx[None, :] // HEAD_DIM)
    t = np.arange(CHUNK)
    tri = np.stack([t[:, None] >= t[None, :], t[:, None] <= t[None, :]]).astype(np.float32)
    return (jnp.asarray(same_head.astype(np.float32), BF16),
            jnp.asarray(same_head.astype(np.float32) / HEAD_DIM, BF16),
            jnp.asarray(np.concatenate([tri, tri], axis=2), BF16))


def kernel(x, ln1_g, w_in, shift_mu_prev, shift_mu_next, decay_w0, decay_w2, iclr_a0, iclr_a2,
           gate_g2, k_k, k_a, r_k, lnx_g, lnx_b, attn_sink, w_out, ln2_g, ffn_w_gate, ffn_w_up,
           ffn_conv_w, ffn_conv_b, ffn_w_down, lnf_g):
    B, T, _ = x.shape
    assert w_in.shape[0] == 1, "single-layer block"
    l = 0
    ones_blk, mean_blk, tri = _constants()
    row = lambda a: a.reshape(1, -1)
    w2 = decay_w2[l]
    w2_pad = jnp.concatenate([w2, jnp.zeros_like(w2)], axis=1)
    w2_bf = w2_pad.astype(BF16)
    w2_cat = jnp.concatenate([w2_bf, w2_bf], axis=1)
    mu_p, mu_n = shift_mu_prev[l], shift_mu_next[l]
    a2_pad = jnp.concatenate([jnp.zeros_like(iclr_a2[l]), iclr_a2[l]], axis=0).astype(BF16)
    (at0, rt0, bt0, kt0, at1, rt1, bt1, kt1, v, pl0, pl1, g, bonus, q, ka, va) = _in_proj(
        x, row(ln1_g[l]), w_in[l], row(1.0 - mu_p - mu_n), row(mu_p), row(mu_n),
        decay_w0[l], w2_cat, row(iclr_a0[l]), a2_pad, gate_g2[l].astype(BF16),
        row(k_k[l]), row(k_a[l]), row(r_k[l]), ones_blk, tri)
    cpt = TM_IN // CHUNK
    pl0 = pl0[:, :, :cpt].reshape(B, T // CHUNK, 1, RWKV_DIM)
    pl1 = pl1[:, :, :cpt].reshape(B, T // CHUNK, 1, RWKV_DIM)
    yf, yb, o_att, (wo, wg, wu, wd) = _scan_attention(
        at0, rt0, bt0, kt0, at1, rt1, bt1, kt1, v, pl0, pl1, q, ka, va, attn_sink[l],
        (w_out[l], ffn_w_gate[l], ffn_w_up[l], ffn_w_down[l]))
    return _mix_ffn(x, yf, yb, bonus, g, o_att, row(lnx_g[l]), row(lnx_b[l]), mean_blk, wo,
                    row(ln2_g[l]), wg, wu, ffn_conv_w[l], row(ffn_conv_b[l]), wd, row(lnf_g))
```

```python
import functools

import numpy as np
import jax
import jax.numpy as jnp
from jax import lax
from jax.experimental import pallas as pl
from jax.experimental.pallas import tpu as pltpu

F32 = jnp.float32
BF16 = jnp.bfloat16

D_MODEL = 1024
HEAD_DIM = 64
RWKV_DIM = 512
ATT_DIM = 512
ATT_HEADS = 8
KV_DIM = 128
LORA_DIM = 256
SHIFT_DIM = 3 * RWKV_DIM + LORA_DIM
PROJ_DIM = SHIFT_DIM + ATT_DIM + 2 * KV_DIM
WINDOW = 128
D_FF = 2816
CONV_WIDTH = 3
NORM_EPS = 1e-6
LNX_EPS = 64e-5
L2_EPS = 1e-12
MASK_VALUE = -1e30
LOG2E = float(np.log2(np.e))
NEG_DECAY_SCALE = float(-np.exp(-0.5) * np.log2(np.e))

LANES = 128
SUBLANES = 8
BF16_ROWS = 16
CHUNK = 64
VMEM_LIMIT = 56 * 1024 * 1024

TM_IN = 512
IN_PARTS = 2
TM_FFN = 512
ATT_QBLOCKS = 2
FF_SPLITS = (0, 1536, D_FF)


def _dot(a, b):
    return jnp.dot(a.astype(BF16), b.astype(BF16), preferred_element_type=F32)


def _dot_nt(a, b):
    return lax.dot_general(a.astype(BF16), b.astype(BF16), (((1,), (1,)), ((), ())),
                           preferred_element_type=F32)


def _dot_tn(a, b):
    return lax.dot_general(a.astype(BF16), b.astype(BF16), (((0,), (0,)), ((), ())),
                           preferred_element_type=F32)


def _split2(x):
    hi = x.astype(BF16)
    lo = (x - hi.astype(F32)).astype(BF16)
    return hi, lo


def _rms_norm(x, g):
    return x * lax.rsqrt(jnp.mean(x * x, axis=-1, keepdims=True) + NORM_EPS) * g


def _in_proj_kernel(x_ref, xp_ref, xn_ref, ln1_ref, w_ref, muc_ref, mup_ref, mun_ref, w0_ref,
                    w2_ref, a0_ref, a2_ref, g2_ref, kk_ref, ka_ref, rk_ref, ones_ref, tri_ref,
                    at0_ref, rt0_ref, bt0_ref, kt0_ref, at1_ref, rt1_ref, bt1_ref, kt1_ref,
                    v_ref, pl0_ref, pl1_ref, g_ref, bonus_ref, q_ref, ka_o_ref, va_o_ref, wbf_ref):
    i = pl.program_id(1)
    n_tiles = pl.num_programs(1)
    tm = x_ref.shape[1]

    @pl.when((pl.program_id(0) == 0) & (i == 0))
    def _():
        wbf_ref[...] = w_ref[...].astype(BF16)

    hm = tm // IN_PARTS
    rows = hm + 2 * SUBLANES
    core = slice(SUBLANES, hm + SUBLANES)
    halo_lo = jnp.where(i > 0, xp_ref[0], 0.0)
    halo_hi = jnp.where(i < n_tiles - 1, xn_ref[0], 0.0)
    dir_outs = ((at0_ref, rt0_ref, bt0_ref, kt0_ref, pl0_ref),
                (at1_ref, rt1_ref, bt1_ref, kt1_ref, pl1_ref))

    def seg_sum(t):
        return _dot(t, ones_ref[...])

    def part(lo):
        out = slice(lo, lo + hm)
        before = halo_lo if lo == 0 else x_ref[0, lo - SUBLANES:lo, :]
        after = halo_hi if lo + hm == tm else x_ref[0, lo + hm:lo + hm + SUBLANES, :]
        x_ext = jnp.concatenate([before, x_ref[0, out, :], after], axis=0)
        h = _rms_norm(x_ext, ln1_ref[...]).astype(BF16)
        proj = lambda c0, c1: _dot(h, wbf_ref[:, c0:c1])
        p_codes = proj(3 * RWKV_DIM, SHIFT_DIM)
        p_k = proj(RWKV_DIM, 2 * RWKV_DIM)
        p_r = proj(0, RWKV_DIM)
        p_v = proj(2 * RWKV_DIM, 3 * RWKV_DIM)
        att = proj(SHIFT_DIM, PROJ_DIM)[core]
        yield

        def shifted(p_ext, c0, c1):
            prev = pltpu.roll(p_ext, 1, axis=0)[core]
            nxt = pltpu.roll(p_ext, rows - 1, axis=0)[core]
            return (p_ext[core] * muc_ref[:, c0:c1] + prev * mup_ref[:, c0:c1]
                    + nxt * mun_ref[:, c0:c1])

        codes = shifted(p_codes, 3 * RWKV_DIM, SHIFT_DIM)
        c_di = codes[:, :LANES]
        th_hi, th_lo = _split2(jnp.tanh(c_di))
        th_cat = jnp.concatenate([th_hi, th_lo], axis=1)
        gate_code = jax.nn.sigmoid(codes[:, LANES:])
        k = shifted(p_k, RWKV_DIM, 2 * RWKV_DIM)
        kkr = k * kk_ref[...]
        kkr_sq = kkr * kkr
        r = shifted(p_r, 0, RWKV_DIM)
        v = shifted(p_v, 2 * RWKV_DIM, 3 * RWKV_DIM)
        v_ref[0, out, :] = v.astype(BF16)
        q_ref[0, out, :] = att[:, :ATT_DIM].astype(BF16)
        ka_o_ref[0, out, :] = att[:, ATT_DIM:ATT_DIM + KV_DIM].astype(BF16)
        va_o_ref[0, out, :] = att[:, ATT_DIM + KV_DIM:].astype(BF16)
        yield

        a_pre = _dot(c_di, a2_ref[...])
        g_ref[0, out, :] = _dot(gate_code, g2_ref[...]).astype(BF16)
        n2 = seg_sum(kkr_sq)
        z = [w0_ref[d:d + 1, :] + jnp.dot(th_cat, w2_ref[d], preferred_element_type=F32)
             for d in range(2)]
        yield

        a_vec = jax.nn.sigmoid(a0_ref[...] + a_pre)
        kk = kkr * lax.rsqrt(jnp.maximum(n2, L2_EPS * L2_EPS))
        k2 = k * (1.0 + (a_vec - 1.0) * ka_ref[...])
        b_vec = kk * a_vec
        neg_kk = -kk
        rk2 = r * k2 * rk_ref[...]
        lw = [NEG_DECAY_SCALE / (1.0 + jnp.exp2(z[d] * (-LOG2E))) for d in range(2)]
        lw_split = [_split2(lw[d]) for d in range(2)]
        yield

        bonus_sum = seg_sum(rk2)
        parts = [[jnp.dot(tri_ref[d],
                          jnp.concatenate([lw_split[d][0][cs], lw_split[d][1][cs]], axis=0),
                          preferred_element_type=F32)
                  for cs in (slice(j * CHUNK, (j + 1) * CHUNK) for j in range(hm // CHUNK))]
                 for d in range(2)]
        yield

        bonus_ref[0, out, :] = (bonus_sum * v).astype(BF16)
        for d, (at_ref, rt_ref, bt_ref, kt_ref, pl_ref) in enumerate(dir_outs):
            ci = jnp.concatenate(parts[d], axis=0)
            end = 0 if d else CHUNK - 1
            tot = jnp.concatenate([c[end:end + 1] for c in parts[d]], axis=0)
            pl_ref[0, 0, lo // CHUNK:(lo + hm) // CHUNK, :] = jnp.exp2(tot)
            e_inc = jnp.exp2(ci)
            e_exc = jnp.exp2(ci - lw[d])
            e_inv = 1.0 / e_inc
            at_ref[0, out, :] = (neg_kk * e_exc).astype(BF16)
            rt_ref[0, out, :] = (r * e_inc).astype(BF16)
            bt_ref[0, out, :] = (b_vec * e_inv).astype(BF16)
            kt_ref[0, out, :] = (k2 * e_inv).astype(BF16)

    live = [part(p * hm) for p in range(IN_PARTS)]
    while live:
        live = [g for g in live if next(g, StopIteration) is not StopIteration]
    if tm // CHUNK < SUBLANES:
        for d in range(2):
            dir_outs[d][4][0, 0, tm // CHUNK:, :] = jnp.ones((SUBLANES - tm // CHUNK, RWKV_DIM), F32)


def _in_proj(x, ln1_g, w_in, mu_cur, mu_prev, mu_next, decay_w0, w2_cat, iclr_a0, a2_pad, gate_g2,
             k_k, k_a, r_k, ones_blk, tri):
    B, T, D = x.shape
    tm = TM_IN
    nt = T // tm
    rows8 = tm // SUBLANES
    const = lambda shape: pl.BlockSpec(shape, lambda b, i: (0,) * len(shape))
    tok = lambda width: pl.BlockSpec((1, tm, width), lambda b, i: (b, i, 0))
    in_specs = [
        tok(D),
        pl.BlockSpec((1, SUBLANES, D), lambda b, i: (b, jnp.maximum(i * rows8 - 1, 0), 0)),
        pl.BlockSpec((1, SUBLANES, D), lambda b, i: (b, jnp.minimum((i + 1) * rows8, T // SUBLANES - 1), 0)),
        const((1, D)),
        pl.BlockSpec((D, PROJ_DIM), lambda b, i: (0, 0), pipeline_mode=pl.Buffered(1)),
        const((1, SHIFT_DIM)), const((1, SHIFT_DIM)), const((1, SHIFT_DIM)),
        const((2, RWKV_DIM)), const((2, 2 * LANES, RWKV_DIM)),
        const((1, RWKV_DIM)), const((LANES, RWKV_DIM)), const((LANES, RWKV_DIM)),
        const((1, RWKV_DIM)), const((1, RWKV_DIM)), const((1, RWKV_DIM)),
        const((RWKV_DIM, RWKV_DIM)), const((2, CHUNK, 2 * CHUNK)),
    ]
    tok_bf = jax.ShapeDtypeStruct((B, T, RWKV_DIM), BF16)
    pl_shape = jax.ShapeDtypeStruct((B, nt, SUBLANES, RWKV_DIM), F32)
    pl_spec = pl.BlockSpec((1, 1, SUBLANES, RWKV_DIM), lambda b, i: (b, i, 0, 0))
    out_shape = [tok_bf] * 9 + [pl_shape, pl_shape, tok_bf, tok_bf,
                                jax.ShapeDtypeStruct((B, T, ATT_DIM), BF16),
                                jax.ShapeDtypeStruct((B, T, KV_DIM), BF16),
                                jax.ShapeDtypeStruct((B, T, KV_DIM), BF16)]
    out_specs = [tok(RWKV_DIM)] * 9 + [pl_spec, pl_spec, tok(RWKV_DIM), tok(RWKV_DIM),
                                       tok(ATT_DIM), tok(KV_DIM), tok(KV_DIM)]
    return pl.pallas_call(
        _in_proj_kernel, grid=(B, nt), in_specs=in_specs, out_specs=out_specs, out_shape=out_shape,
        scratch_shapes=[pltpu.VMEM((D, PROJ_DIM), BF16)],
        compiler_params=pltpu.CompilerParams(dimension_semantics=("arbitrary", "arbitrary"),
                                             vmem_limit_bytes=VMEM_LIMIT),
        name="in_proj",
    )(x, x, x, ln1_g, w_in, mu_cur, mu_prev, mu_next, decay_w0, w2_cat, iclr_a0, a2_pad, gate_g2,
      k_k, k_a, r_k, ones_blk, tri)


def _pair_chunks(items, levels, eye, lane0, bd_mask, outs):
    n = range(len(items))
    at, rt, bt, kt, v, p_last, s_prev, strict, incl = zip(*items)

    def bd(x):
        x = x.astype(BF16)
        zero = jnp.zeros_like(x)
        return jnp.concatenate([jnp.where(lane0, x, zero), jnp.where(lane0, zero, x)], axis=0)

    def pmm(x, y):
        return _dot(x, bd(y))

    sc = [_dot_nt(jnp.concatenate([at[i], rt[i]], axis=0),
                  jnp.concatenate([bd(bt[i]), bd(kt[i])], axis=0)) for i in n]
    a_ab = [jnp.where(strict[i], sc[i][:CHUNK, :LANES], 0.0) for i in n]
    a_ak = [jnp.where(strict[i], sc[i][:CHUNK, LANES:], 0.0) for i in n]
    a_rb = [jnp.where(incl[i], sc[i][CHUNK:, :LANES], 0.0) for i in n]
    a_rk = [jnp.where(incl[i], sc[i][CHUNK:, LANES:], 0.0) for i in n]
    yield

    xy = [pmm(jnp.concatenate([a_ak[i], a_rk[i]], axis=0), v[i]) for i in n]
    x1 = [xy[i][:CHUNK] for i in n]
    yk = [xy[i][CHUNK:] for i in n]
    t_inv = [eye + jnp.where(levels[0], a_ab[i], 0.0) for i in n]
    yield
    for level in levels[1:]:
        e_t = [pmm(jnp.where(level, a_ab[i], 0.0), t_inv[i]) for i in n]
        yield
        t_inv = [t_inv[i] + pmm(t_inv[i], e_t[i]) for i in n]
        yield
    wu = [_dot(t_inv[i], jnp.concatenate([bd(at[i]), bd(x1[i])], axis=1)) for i in n]
    yield
    hs = [_dot_nt(jnp.concatenate([wu[i][:, :LANES].astype(BF16), rt[i]], axis=0), s_prev[i])
          for i in n]
    yield
    u = [hs[i][:CHUNK] + wu[i][:, LANES:] for i in n]
    y = [hs[i][CHUNK:] + pmm(a_rb[i], u[i]) + yk[i] for i in n]
    yield
    upd = [_dot_tn(jnp.concatenate([u[i].astype(BF16), v[i]], axis=0),
                   jnp.concatenate([bt[i], kt[i]], axis=0)) for i in n]
    s_new = [(s_prev[i] + jnp.where(bd_mask, upd[i], 0.0)) * p_last[i] for i in n]
    outs.extend(zip(y, s_new))


def _scan_attn_kernel(at0_ref, rt0_ref, bt0_ref, kt0_ref, v0_ref, pl0_ref,
                      at1_ref, rt1_ref, bt1_ref, kt1_ref, v1_ref, pl1_ref, *rest, n_cast):
    n_attn = ATT_QBLOCKS + 8
    attn_in = rest[:n_attn]
    cast_in = rest[n_attn:n_attn + n_cast]
    yf_ref, yb_ref, o_att_ref = rest[n_attn + n_cast:n_attn + n_cast + 3]
    cast_out = rest[n_attn + n_cast + 3:n_attn + 2 * n_cast + 3]
    s_ref = rest[-1]
    c = pl.program_id(0)
    for src_ref, dst_ref in zip(cast_in, cast_out):
        dst_ref[...] = src_ref[...].astype(BF16)

    @pl.when(c == 0)
    def _():
        s_ref[...] = jnp.zeros_like(s_ref)

    n_batch = v0_ref.shape[0]
    ri = lax.broadcasted_iota(jnp.int32, (CHUNK, LANES), 0)
    ci = lax.broadcasted_iota(jnp.int32, (CHUNK, LANES), 1)
    cj = jnp.where(ci >= CHUNK, ci - CHUNK, ci)
    lane0 = ci < CHUNK
    eye = jnp.where(ri == cj, 1.0, 0.0).astype(F32)
    levels = [((ri // (2 * s)) == (cj // (2 * s))) & ((ri // s) != (cj // s))
              for s in (2 ** e for e in range(CHUNK.bit_length() - 1))]
    r2 =lax.broadcasted_iota(jnp.int32, (LANES, LANES), 0)
    c2 = lax.broadcasted_iota(jnp.int32, (LANES, LANES), 1)
    bd_mask = (r2 >= CHUNK) == (c2 >= CHUNK)
    dirs = ((at0_ref, rt0_ref, bt0_ref, kt0_ref, v0_ref, pl0_ref, ri > cj, ri >= cj),
            (at1_ref, rt1_ref, bt1_ref, kt1_ref, v1_ref, pl1_ref, ri < cj, ri <= cj))
    n_pairs = RWKV_DIM // LANES
    items = []
    for b in range(n_batch):
        for d, (at_ref, rt_ref, bt_ref, kt_ref, v_ref, pl_ref, strict, incl) in enumerate(dirs):
            for p in range(n_pairs):
                sl = slice(p * LANES, (p + 1) * LANES)
                items.append((at_ref[b, :, sl], rt_ref[b, :, sl], bt_ref[b, :, sl], kt_ref[b, :, sl],
                              v_ref[b, :, sl], pl_ref[b, 0, :, sl], s_ref[b, d, p], strict, incl))
    outs = []
    scan = _pair_chunks(items, levels, eye, lane0, bd_mask, outs)
    attn = _attn_stages(*attn_in, o_att_ref)
    while next(scan, StopIteration) is not StopIteration:
        next(attn, None)
    for _ in attn:
        pass
    for b in range(n_batch):
        for d, y_ref in enumerate((yf_ref, yb_ref)):
            base = (b * 2 + d) * n_pairs
            y_ref[b] = jnp.concatenate([outs[base + p][0] for p in range(n_pairs)],
                                       axis=1).astype(BF16)
            for p in range(n_pairs):
                s_ref[b, d, p] = outs[base + p][1]


def _attn_stages(sink_ref, *refs):
    n_sub = ATT_QBLOCKS
    bias_refs = refs[:n_sub]
    q_ref, kp_ref, kc_ref, kn_ref, vp_ref, vc_ref, vn_ref, o_ref = refs[n_sub:]
    blk = WINDOW
    k_all = jnp.concatenate([kp_ref[0], kc_ref[0], kn_ref[0]], axis=0)
    v_all = jnp.concatenate([vp_ref[0], vc_ref[0], vn_ref[0]], axis=0)

    def swap_halves(x):
        return jnp.concatenate([x[:, HEAD_DIM:], x[:, :HEAD_DIM]], axis=1)

    lane0 = lax.broadcasted_iota(jnp.int32, k_all.shape, 1) < HEAD_DIM

    def variants(x):
        xs = swap_halves(x)
        zero = jnp.zeros_like(x)
        return ((jnp.where(lane0, x, zero), jnp.where(lane0, zero, xs)),
                (jnp.where(lane0, xs, zero), jnp.where(lane0, zero, x)))

    k_var = variants(k_all)
    v_var = variants(v_all)
    yield

    group = ATT_HEADS // (KV_DIM // HEAD_DIM)

    items = [(u, h) for u in range(n_sub) for h in range(ATT_HEADS)]
    keys = lambda var, u, h: var[h // group][h % 2][u * blk:(u + 3) * blk]
    q_pairs = [[q_ref[0, u * blk:(u + 1) * blk, j * LANES:(j + 1) * LANES]
                * jnp.asarray(HEAD_DIM ** -0.5, BF16) for j in range(ATT_DIM // LANES)]
               for u in range(n_sub)]
    s = [_dot_nt(q_pairs[u][h // 2], keys(k_var, u, h)) + bias_refs[u][0, h] for u, h in items]
    yield
    m = [jnp.maximum(jnp.max(s[i], axis=-1, keepdims=True), sink_ref[h]) for i, (u, h) in enumerate(items)]
    yield
    p = [jnp.exp(s[i] - m[i]) for i in range(len(items))]
    yield
    den = [jnp.sum(p[i], axis=-1, keepdims=True) + jnp.exp(sink_ref[h] - m[i])
           for i, (u, h) in enumerate(items)]
    yield
    o = [_dot(p[i], keys(v_var, u, h)) * (1.0 / den[i]) for i, (u, h) in enumerate(items)]
    yield
    for u in range(n_sub):
        ou = o[u * ATT_HEADS:(u + 1) * ATT_HEADS]
        o_ref[0, u * blk:(u + 1) * blk, :] = jnp.concatenate(
            [ou[2 * j] + ou[2 * j + 1] for j in range(ATT_DIM // LANES)], axis=1).astype(o_ref.dtype)


def _attn_bias(blk):
    qi = np.arange(blk)[:, None]
    kj = np.arange(3 * blk)[None, :]
    dist = np.abs(kj - blk - qi)
    slopes = 2.0 ** (-8.0 * np.arange(1, ATT_HEADS + 1, dtype=np.float32) / ATT_HEADS)
    alibi = -slopes[:, None, None].astype(np.float32) * dist[None].astype(np.float32)
    out = []
    for has_prev, has_next in ((False, True), (True, True), (True, False)):
        valid = (dist <= WINDOW) & (has_prev | (kj >= blk)) & (has_next | (kj < 2 * blk))
        out.append(np.where(valid[None], alibi, np.float32(MASK_VALUE)))
    return jnp.asarray(np.stack(out), F32)


def _scan_attention(at0, rt0, bt0, kt0, at1, rt1, bt1, kt1, v, pl0, pl1, q, ka, va, sink, weights):
    B, T, C = v.shape
    nc = T // CHUNK
    blk = WINDOW
    n_sub = ATT_QBLOCKS
    per_row = T // (blk * n_sub)
    nb = T // blk
    assert B * per_row == nc, "scan and attention must have the same number of grid steps"
    fwd = pl.BlockSpec((B, CHUNK, C), lambda c: (0, c, 0))
    bwd = pl.BlockSpec((B, CHUNK, C), lambda c: (0, nc - 1 - c, 0))
    pl_f = pl.BlockSpec((B, 1, 1, C), lambda c: (0, c, 0, 0))
    pl_b = pl.BlockSpec((B, 1, 1, C), lambda c: (0, nc - 1 - c, 0, 0))
    y_shape = jax.ShapeDtypeStruct((B, T, C), BF16)

    row = lambda c: c // per_row
    step = lambda c: lax.rem(c, per_row)
    cur = lambda width: pl.BlockSpec((1, n_sub * blk, width), lambda c: (row(c), step(c), 0))
    prev = pl.BlockSpec((1, blk, KV_DIM), lambda c: (row(c), jnp.maximum(step(c) * n_sub - 1, 0), 0))
    nxt = pl.BlockSpec((1, blk, KV_DIM), lambda c: (row(c), jnp.minimum((step(c) + 1) * n_sub, nb - 1), 0))

    def bias_spec(u):
        def index(c):
            g = step(c) * n_sub + u
            return (jnp.where(g == 0, 0, jnp.where(g == nb - 1, 2, 1)), 0, 0, 0)
        return pl.BlockSpec((1, ATT_HEADS, blk, 3 * blk), index)

    def cast_spec(w):
        rows, cols = w.shape
        per_step = next(r for r in range(BF16_ROWS, rows + 1, BF16_ROWS)
                        if rows % r == 0 and rows // r <= nc)
        last = rows // per_step - 1
        return pl.BlockSpec((per_step, cols), lambda c: (jnp.minimum(c, last), 0))

    cast_specs = [cast_spec(w) for w in weights]
    bias = _attn_bias(blk)
    outs = pl.pallas_call(
        functools.partial(_scan_attn_kernel, n_cast=len(weights)), grid=(nc,),
        in_specs=[fwd, fwd, fwd, fwd, fwd, pl_f, bwd, bwd, bwd, bwd, bwd, pl_b,
                  pl.BlockSpec(memory_space=pltpu.SMEM)] + [bias_spec(u) for u in range(n_sub)]
                 + [cur(ATT_DIM), prev, cur(KV_DIM), nxt, prev, cur(KV_DIM), nxt] + cast_specs,
        out_specs=[fwd, bwd, cur(ATT_DIM)] + cast_specs,
        out_shape=[y_shape, y_shape, jax.ShapeDtypeStruct((B, T, ATT_DIM), BF16)]
                  + [jax.ShapeDtypeStruct(w.shape, BF16) for w in weights],
        scratch_shapes=[pltpu.VMEM((B, 2, C // LANES, LANES, LANES), F32)],
        compiler_params=pltpu.CompilerParams(dimension_semantics=("arbitrary",),
                                             vmem_limit_bytes=VMEM_LIMIT),
        name="scan_attn",
    )(at0, rt0, bt0, kt0, v, pl0, at1, rt1, bt1, kt1, v, pl1,
      sink, *([bias] * n_sub), q, ka, ka, ka, va, va, va, *weights)
    return outs[0], outs[1], outs[2], outs[3:]


HALO = 16


def _mix_ffn_kernel(*refs):
    (x_m, x_p, x_n, yf_m, yf_p, yf_n, yb_m, yb_p, yb_n, bo_m, bo_p, bo_n, g_m, g_p, g_n,
     oa_m, oa_p, oa_n, lg_ref, lb_ref, mean_ref, wo_ref, ln2_ref, wg_ref, wu_ref, cw_ref, cb_ref,
     wd_ref, lnf_ref, o_ref) = refs
    i = pl.program_id(1)
    n_tiles = pl.num_programs(1)
    tm = x_m.shape[1]
    rows = tm + 2 * HALO
    core = slice(HALO, tm + HALO)
    ext = lambda m, p, n: jnp.concatenate([p[0], m[0], n[0]], axis=0)

    def seg_mean(t):
        return _dot(t, mean_ref[...])

    y = ext(yf_m, yf_p, yf_n).astype(F32) + ext(yb_m, yb_p, yb_n).astype(F32)
    d = y - seg_mean(y)
    var = seg_mean(d * d)
    yn = d * lax.rsqrt(var + LNX_EPS) * lg_ref[...] + lb_ref[...]
    o_rwkv = (yn + ext(bo_m, bo_p, bo_n)) * ext(g_m, g_p, g_n)
    mix = _dot(o_rwkv, wo_ref[:RWKV_DIM, :]) + _dot(ext(oa_m, oa_p, oa_n), wo_ref[RWKV_DIM:, :])
    x1 = ext(x_m, x_p, x_n) + mix
    r = lax.broadcasted_iota(jnp.int32, (rows, 1), 0)
    inside = ((r >= HALO) | (i > 0)) & ((r < tm + HALO) | (i < n_tiles - 1))
    x1 = jnp.where(inside, x1, 0.0)

    h = _rms_norm(x1, ln2_ref[...]).astype(BF16)
    h_core = h[core]
    acc = jnp.zeros((tm, D_MODEL), F32)
    for c0, c1 in zip(FF_SPLITS[:-1], FF_SPLITS[1:]):
        cs = slice(c0, c1)
        gp_ext = _dot(h, wg_ref[:, cs])
        prev = pltpu.roll(gp_ext, 1, axis=0)[core]
        nxt = pltpu.roll(gp_ext, rows - 1, axis=0)[core]
        gate = (prev * cw_ref[0:1, cs] + gp_ext[core] * cw_ref[1:2, cs] + nxt * cw_ref[2:3, cs]
                + cb_ref[:, cs])
        act = 0.5 * gate * (1.0 + lax.erf(gate * float(1.0 / np.sqrt(2.0))))
        up = _dot(h_core, wu_ref[:, cs])
        acc = acc + _dot(act * up, wd_ref[cs, :])
    o_ref[0] = _rms_norm(x1[core] + acc, lnf_ref[...])


def _mix_ffn(x, yf, yb, bonus, g, o_att, lnx_g, lnx_b, mean_blk, w_out, ln2_g, wg, wu, conv_w, conv_b,
             wd, lnf_g):
    B, T, D = x.shape
    tm = TM_FFN
    per_tile = tm // HALO
    last = T // HALO - 1

    def tok(width):
        return [pl.BlockSpec((1, tm, width), lambda b, i: (b, i, 0)),
                pl.BlockSpec((1, HALO, width), lambda b, i: (b, jnp.maximum(i * per_tile - 1, 0), 0)),
                pl.BlockSpec((1, HALO, width), lambda b, i: (b, jnp.minimum((i + 1) * per_tile, last), 0))]

    const = lambda shape: pl.BlockSpec(shape, lambda b, i: (0,) * len(shape))
    resident = lambda shape: pl.BlockSpec(shape, lambda b, i: (0,) * len(shape),
                                          pipeline_mode=pl.Buffered(1))
    tokens = (x, yf, yb, bonus, g, o_att)
    return pl.pallas_call(
        _mix_ffn_kernel, grid=(B, T // tm),
        in_specs=[spec for a in tokens for spec in tok(a.shape[-1])]
                 + [const((1, RWKV_DIM)), const((1, RWKV_DIM)), const((RWKV_DIM, RWKV_DIM)), const((D, D)),
                    const((1, D)), resident((D, D_FF)), resident((D, D_FF)),
                    const((CONV_WIDTH, D_FF)), const((1, D_FF)), resident((D_FF, D)), const((1, D))],
        out_specs=pl.BlockSpec((1, tm, D), lambda b, i: (b, i, 0)),
        out_shape=jax.ShapeDtypeStruct((B, T, D), F32),
        compiler_params=pltpu.CompilerParams(dimension_semantics=("parallel", "parallel"),
                                             vmem_limit_bytes=VMEM_LIMIT),
        name="mix_ffn",
    )(*[a for a in tokens for _ in range(3)], lnx_g, lnx_b, mean_blk, w_out, ln2_g, wg, wu, conv_w,
      conv_b, wd, lnf_g)


def _constants():
    idx = np.arange(RWKV_DIM)
    same_head = (idx[:, None] // HEAD_DIM) == (idx[None, :] // HEAD_DIM)
    t = np.arange(CHUNK)
    tri = np.stack([t[:, None] >= t[None, :], t[:, None] <= t[None, :]]).astype(np.float32)
    return (jnp.asarray(same_head.astype(np.float32), BF16),
            jnp.asarray(same_head.astype(np.float32) / HEAD_DIM, BF16),
            jnp.asarray(np.concatenate([tri, tri], axis=2), BF16))


def kernel(x, ln1_g, w_in, shift_mu_prev, shift_mu_next, decay_w0, decay_w2, iclr_a0, iclr_a2,
           gate_g2, k_k, k_a, r_k, lnx_g, lnx_b, attn_sink, w_out, ln2_g, ffn_w_gate, ffn_w_up,
           ffn_conv_w, ffn_conv_b, ffn_w_down, lnf_g):
    B, T, _ = x.shape
    assert w_in.shape[0] == 1, "single-layer block"
    l = 0
    ones_blk, mean_blk, tri = _constants()
    row = lambda a: a.reshape(1, -1)
    w2 = decay_w2[l]
    w2_pad = jnp.concatenate([w2, jnp.zeros_like(w2)], axis=1)
    w2_bf = w2_pad.astype(BF16)
    w2_cat = jnp.concatenate([w2_bf, w2_bf], axis=1)
    mu_p, mu_n = shift_mu_prev[l], shift_mu_next[l]
    a2_pad = jnp.concatenate([jnp.zeros_like(iclr_a2[l]), iclr_a2[l]], axis=0).astype(BF16)
    (at0, rt0, bt0, kt0, at1, rt1, bt1, kt1, v, pl0, pl1, g, bonus, q, ka, va) = _in_proj(
        x, row(ln1_g[l]), w_in[l], row(1.0 - mu_p - mu_n), row(mu_p), row(mu_n),
        decay_w0[l], w2_cat, row(iclr_a0[l]), a2_pad, gate_g2[l].astype(BF16),
        row(k_k[l]), row(k_a[l]), row(r_k[l]), ones_blk, tri)
    cpt = TM_IN // CHUNK
    pl0 = pl0[:, :, :cpt].reshape(B, T // CHUNK, 1, RWKV_DIM)
    pl1 = pl1[:, :, :cpt].reshape(B, T // CHUNK, 1, RWKV_DIM)
    yf, yb, o_att, (wo, wg, wu, wd) = _scan_attention(
        at0, rt0, bt0, kt0, at1, rt1, bt1, kt1, v, pl0, pl1, q, ka, va, attn_sink[l],
        (w_out[l], ffn_w_gate[l], ffn_w_up[l], ffn_w_down[l]))
    return _mix_ffn(x, yf, yb, bonus, g, o_att, row(lnx_g[l]), row(lnx_b[l]), mean_blk, wo,
                    row(ln2_g[l]), wg, wu, ffn_conv_w[l], row(ffn_conv_b[l]), wd, row(lnf_g))
```

```python
import functools

import numpy as np
import jax
import jax.numpy as jnp
from jax import lax
from jax.experimental import pallas as pl
from jax.experimental.pallas import tpu as pltpu

F32 = jnp.float32
BF16 = jnp.bfloat16

D_MODEL = 1024
HEAD_DIM = 64
RWKV_DIM = 512
ATT_DIM = 512
ATT_HEADS = 8
KV_DIM = 128
LORA_DIM = 256
SHIFT_DIM = 3 * RWKV_DIM + LORA_DIM
PROJ_DIM = SHIFT_DIM + ATT_DIM + 2 * KV_DIM
WINDOW = 128
D_FF = 2816
CONV_WIDTH = 3
NORM_EPS = 1e-6
LNX_EPS = 64e-5
L2_EPS = 1e-12
MASK_VALUE = -1e30
LOG2E = float(np.log2(np.e))
NEG_DECAY_SCALE = float(-np.exp(-0.5) * np.log2(np.e))

LANES = 128
SUBLANES = 8
BF16_ROWS = 16
CHUNK = 64
VMEM_LIMIT = 56 * 1024 * 1024

TM_IN = 512
IN_PARTS = 2
TM_FFN = 512
ATT_QBLOCKS = 2
ATTN_PER_SCAN_STAGE = 1
FF_SPLITS = (0, 1536, D_FF)


def _dot(a, b):
    return jnp.dot(a.astype(BF16), b.astype(BF16), preferred_element_type=F32)


def _dot_nt(a, b):
    return lax.dot_general(a.astype(BF16), b.astype(BF16), (((1,), (1,)), ((), ())),
                           preferred_element_type=F32)


def _dot_tn(a, b):
    return lax.dot_general(a.astype(BF16), b.astype(BF16), (((0,), (0,)), ((), ())),
                           preferred_element_type=F32)


def _split2(x):
    hi = x.astype(BF16)
    lo = (x - hi.astype(F32)).astype(BF16)
    return hi, lo


def _rms_norm(x, g):
    return x * lax.rsqrt(jnp.mean(x * x, axis=-1, keepdims=True) + NORM_EPS) * g


def _in_proj_kernel(x_ref, xp_ref, xn_ref, ln1_ref, w_ref, muc_ref, mup_ref, mun_ref, w0_ref,
                    w2_ref, a0_ref, a2_ref, g2_ref, kk_ref, ka_ref, rk_ref, ones_ref, tri_ref,
                    at0_ref, rt0_ref, bt0_ref, kt0_ref, at1_ref, rt1_ref, bt1_ref, kt1_ref,
                    v_ref, pl0_ref, pl1_ref, g_ref, bonus_ref, q_ref, ka_o_ref, va_o_ref, wbf_ref):
    i = pl.program_id(1)
    n_tiles = pl.num_programs(1)
    tm = x_ref.shape[1]

    @pl.when((pl.program_id(0) == 0) & (i == 0))
    def _():
        wbf_ref[...] = w_ref[...].astype(BF16)

    hm = tm // IN_PARTS
    rows = hm + 2 * SUBLANES
    core = slice(SUBLANES, hm + SUBLANES)
    halo_lo = jnp.where(i > 0, xp_ref[0], 0.0)
    halo_hi = jnp.where(i < n_tiles - 1, xn_ref[0], 0.0)
    dir_outs = ((at0_ref, rt0_ref, bt0_ref, kt0_ref, pl0_ref),
                (at1_ref, rt1_ref, bt1_ref, kt1_ref, pl1_ref))

    def seg_sum(t):
        return _dot(t, ones_ref[...])

    def part(lo):
        out = slice(lo, lo + hm)
        before = halo_lo if lo == 0 else x_ref[0, lo - SUBLANES:lo, :]
        after = halo_hi if lo + hm == tm else x_ref[0, lo + hm:lo + hm + SUBLANES, :]
        x_ext = jnp.concatenate([before, x_ref[0, out, :], after], axis=0)
        h = _rms_norm(x_ext, ln1_ref[...]).astype(BF16)
        proj = lambda c0, c1: _dot(h, wbf_ref[:, c0:c1])
        p_codes = proj(3 * RWKV_DIM, SHIFT_DIM)
        p_k = proj(RWKV_DIM, 2 * RWKV_DIM)
        p_r = proj(0, RWKV_DIM)
        p_v = proj(2 * RWKV_DIM, 3 * RWKV_DIM)
        att = proj(SHIFT_DIM, PROJ_DIM)[core]
        yield

        def shifted(p_ext, c0, c1):
            prev = pltpu.roll(p_ext, 1, axis=0)[core]
            nxt = pltpu.roll(p_ext, rows - 1, axis=0)[core]
            return (p_ext[core] * muc_ref[:, c0:c1] + prev * mup_ref[:, c0:c1]
                    + nxt * mun_ref[:, c0:c1])

        codes = shifted(p_codes, 3 * RWKV_DIM, SHIFT_DIM)
        c_di = codes[:, :LANES]
        th_hi, th_lo = _split2(jnp.tanh(c_di))
        th_cat = jnp.concatenate([th_hi, th_lo], axis=1)
        gate_code = jax.nn.sigmoid(codes[:, LANES:])
        k = shifted(p_k, RWKV_DIM, 2 * RWKV_DIM)
        kkr = k * kk_ref[...]
        kkr_sq = kkr * kkr
        r = shifted(p_r, 0, RWKV_DIM)
        v = shifted(p_v, 2 * RWKV_DIM, 3 * RWKV_DIM)
        v_ref[0, out, :] = v.astype(BF16)
        q_ref[0, out, :] = att[:, :ATT_DIM].astype(BF16)
        ka_o_ref[0, out, :] = att[:, ATT_DIM:ATT_DIM + KV_DIM].astype(BF16)
        va_o_ref[0, out, :] = att[:, ATT_DIM + KV_DIM:].astype(BF16)
        yield

        a_pre = _dot(c_di, a2_ref[...])
        g_ref[0, out, :] = _dot(gate_code, g2_ref[...]).astype(BF16)
        n2 = seg_sum(kkr_sq)
        z = [w0_ref[d:d + 1, :] + jnp.dot(th_cat, w2_ref[d], preferred_element_type=F32)
             for d in range(2)]
        yield

        a_vec = jax.nn.sigmoid(a0_ref[...] + a_pre)
        kk = kkr * lax.rsqrt(jnp.maximum(n2, L2_EPS * L2_EPS))
        k2 = k * (1.0 + (a_vec - 1.0) * ka_ref[...])
        b_vec = kk * a_vec
        neg_kk = -kk
        rk2 = r * k2 * rk_ref[...]
        lw = [NEG_DECAY_SCALE / (1.0 + jnp.exp2(z[d] * (-LOG2E))) for d in range(2)]
        lw_split = [_split2(lw[d]) for d in range(2)]
        yield

        bonus_sum = seg_sum(rk2)
        parts = [[jnp.dot(tri_ref[d],
                          jnp.concatenate([lw_split[d][0][cs], lw_split[d][1][cs]], axis=0),
                          preferred_element_type=F32)
                  for cs in (slice(j * CHUNK, (j + 1) * CHUNK) for j in range(hm // CHUNK))]
                 for d in range(2)]
        yield

        bonus_ref[0, out, :] = (bonus_sum * v).astype(BF16)
        for d, (at_ref, rt_ref, bt_ref, kt_ref, pl_ref) in enumerate(dir_outs):
            ci = jnp.concatenate(parts[d], axis=0)
            end = 0 if d else CHUNK - 1
            tot = jnp.concatenate([c[end:end + 1] for c in parts[d]], axis=0)
            pl_ref[0, 0, lo // CHUNK:(lo + hm) // CHUNK, :] = jnp.exp2(tot)
            e_inc = jnp.exp2(ci)
            e_exc = jnp.exp2(ci - lw[d])
            e_inv = 1.0 / e_inc
            at_ref[0, out, :] = (neg_kk * e_exc).astype(BF16)
            rt_ref[0, out, :] = (r * e_inc).astype(BF16)
            bt_ref[0, out, :] = (b_vec * e_inv).astype(BF16)
            kt_ref[0, out, :] = (k2 * e_inv).astype(BF16)

    live = [part(p * hm) for p in range(IN_PARTS)]
    while live:
        live = [g for g in live if next(g, StopIteration) is not StopIteration]
    if tm // CHUNK < SUBLANES:
        for d in range(2):
            dir_outs[d][4][0, 0, tm // CHUNK:, :] = jnp.ones((SUBLANES - tm // CHUNK, RWKV_DIM), F32)


def _in_proj(x, ln1_g, w_in, mu_cur, mu_prev, mu_next, decay_w0, w2_cat, iclr_a0, a2_pad, gate_g2,
             k_k, k_a, r_k, ones_blk, tri):
    B, T, D = x.shape
    tm = TM_IN
    nt = T // tm
    rows8 = tm // SUBLANES
    const = lambda shape: pl.BlockSpec(shape, lambda b, i: (0,) * len(shape))
    tok = lambda width: pl.BlockSpec((1, tm, width), lambda b, i: (b, i, 0))
    in_specs = [
        tok(D),
        pl.BlockSpec((1, SUBLANES, D), lambda b, i: (b, jnp.maximum(i * rows8 - 1, 0), 0)),
        pl.BlockSpec((1, SUBLANES, D), lambda b, i: (b, jnp.minimum((i + 1) * rows8, T // SUBLANES - 1), 0)),
        const((1, D)),
        pl.BlockSpec((D, PROJ_DIM), lambda b, i: (0, 0), pipeline_mode=pl.Buffered(1)),
        const((1, SHIFT_DIM)), const((1, SHIFT_DIM)), const((1, SHIFT_DIM)),
        const((2, RWKV_DIM)), const((2, 2 * LANES, RWKV_DIM)),
        const((1, RWKV_DIM)), const((LANES, RWKV_DIM)), const((LANES, RWKV_DIM)),
        const((1, RWKV_DIM)), const((1, RWKV_DIM)), const((1, RWKV_DIM)),
        const((RWKV_DIM, RWKV_DIM)), const((2, CHUNK, 2 * CHUNK)),
    ]
    tok_bf = jax.ShapeDtypeStruct((B, T, RWKV_DIM), BF16)
    pl_shape = jax.ShapeDtypeStruct((B, nt, SUBLANES, RWKV_DIM), F32)
    pl_spec = pl.BlockSpec((1, 1, SUBLANES, RWKV_DIM), lambda b, i: (b, i, 0, 0))
    out_shape = [tok_bf] * 9 + [pl_shape, pl_shape, tok_bf, tok_bf,
                                jax.ShapeDtypeStruct((B, T, ATT_DIM), BF16),
                                jax.ShapeDtypeStruct((B, T, KV_DIM), BF16),
                                jax.ShapeDtypeStruct((B, T, KV_DIM), BF16)]
    out_specs = [tok(RWKV_DIM)] * 9 + [pl_spec, pl_spec, tok(RWKV_DIM), tok(RWKV_DIM),
                                       tok(ATT_DIM), tok(KV_DIM), tok(KV_DIM)]
    return pl.pallas_call(
        _in_proj_kernel, grid=(B, nt), in_specs=in_specs, out_specs=out_specs, out_shape=out_shape,
        scratch_shapes=[pltpu.VMEM((D, PROJ_DIM), BF16)],
        compiler_params=pltpu.CompilerParams(dimension_semantics=("arbitrary", "arbitrary"),
                                             vmem_limit_bytes=VMEM_LIMIT),
        name="in_proj",
    )(x, x, x, ln1_g, w_in, mu_cur, mu_prev, mu_next, decay_w0, w2_cat, iclr_a0, a2_pad, gate_g2,
      k_k, k_a, r_k, ones_blk, tri)


def _pair_chunks(items, levels, eye, lane0, bd_mask, outs):
    n = range(len(items))
    at, rt, bt, kt, v, p_last, s_prev, strict, incl = zip(*items)

    def bd(x):
        x = x.astype(BF16)
        zero = jnp.zeros_like(x)
        return jnp.concatenate([jnp.where(lane0, x, zero), jnp.where(lane0, zero, x)], axis=0)

    def pmm(x, y):
        return _dot(x, bd(y))

    sc = [_dot_nt(jnp.concatenate([at[i], rt[i]], axis=0),
                  jnp.concatenate([bd(bt[i]), bd(kt[i])], axis=0)) for i in n]
    a_ab = [jnp.where(strict[i], sc[i][:CHUNK, :LANES], 0.0) for i in n]
    a_ak = [jnp.where(strict[i], sc[i][:CHUNK, LANES:], 0.0) for i in n]
    a_rb = [jnp.where(incl[i], sc[i][CHUNK:, :LANES], 0.0) for i in n]
    a_rk = [jnp.where(incl[i], sc[i][CHUNK:, LANES:], 0.0) for i in n]
    yield

    xy = [pmm(jnp.concatenate([a_ak[i], a_rk[i]], axis=0), v[i]) for i in n]
    x1 = [xy[i][:CHUNK] for i in n]
    yk = [xy[i][CHUNK:] for i in n]
    t_inv = [eye + jnp.where(levels[0], a_ab[i], 0.0) for i in n]
    yield
    for level in levels[1:]:
        e_t = [pmm(jnp.where(level, a_ab[i], 0.0), t_inv[i]) for i in n]
        yield
        t_inv = [t_inv[i] + pmm(t_inv[i], e_t[i]) for i in n]
        yield
    wu = [_dot(t_inv[i], jnp.concatenate([bd(at[i]), bd(x1[i])], axis=1)) for i in n]
    yield
    hs = [_dot_nt(jnp.concatenate([wu[i][:, :LANES].astype(BF16), rt[i]], axis=0), s_prev[i])
          for i in n]
    yield
    u = [hs[i][:CHUNK] + wu[i][:, LANES:] for i in n]
    y = [hs[i][CHUNK:] + pmm(a_rb[i], u[i]) + yk[i] for i in n]
    yield
    upd = [_dot_tn(jnp.concatenate([u[i].astype(BF16), v[i]], axis=0),
                   jnp.concatenate([bt[i], kt[i]], axis=0)) for i in n]
    s_new = [(s_prev[i] + jnp.where(bd_mask, upd[i], 0.0)) * p_last[i] for i in n]
    outs.extend(zip(y, s_new))


def _scan_attn_kernel(at0_ref, rt0_ref, bt0_ref, kt0_ref, v0_ref, pl0_ref,
                      at1_ref, rt1_ref, bt1_ref, kt1_ref, v1_ref, pl1_ref, *rest, n_cast,
                      chunks_per_block):
    n_attn = ATT_QBLOCKS + 8
    attn_in = rest[:n_attn]
    cast_in = rest[n_attn:n_attn + n_cast]
    yf_ref, yb_ref, o_att_ref = rest[n_attn + n_cast:n_attn + n_cast + 3]
    cast_out = rest[n_attn + n_cast + 3:n_attn + 2 * n_cast + 3]
    s_ref = rest[-1]
    c = pl.program_id(0)
    for src_ref, dst_ref in zip(cast_in, cast_out):
        dst_ref[...] = src_ref[...].astype(BF16)

    @pl.when(c == 0)
    def _():
        s_ref[...] = jnp.zeros_like(s_ref)

    n_batch = v0_ref.shape[0]
    ri = lax.broadcasted_iota(jnp.int32, (CHUNK, LANES), 0)
    ci = lax.broadcasted_iota(jnp.int32, (CHUNK, LANES), 1)
    cj = jnp.where(ci >= CHUNK, ci - CHUNK, ci)
    lane0 = ci < CHUNK
    eye = jnp.where(ri == cj, 1.0, 0.0).astype(F32)
    levels = [((ri // (2 * s)) == (cj // (2 * s))) & ((ri // s) != (cj // s))
              for s in (2 ** e for e in range(CHUNK.bit_length() - 1))]
    r2 = lax.broadcasted_iota(jnp.int32, (LANES, LANES), 0)
    c2 = lax.broadcasted_iota(jnp.int32, (LANES, LANES), 1)
    bd_mask = (r2 >= CHUNK) == (c2 >= CHUNK)
    row_f = lax.rem(c, chunks_per_block)
    row_b = lax.rem(pl.num_programs(0) - 1 - c, chunks_per_block)
    dirs = ((at0_ref, rt0_ref, bt0_ref, kt0_ref, v0_ref, pl0_ref, row_f, ri > cj, ri >= cj),
            (at1_ref, rt1_ref, bt1_ref, kt1_ref, v1_ref, pl1_ref, row_b, ri < cj, ri <= cj))
    n_pairs = RWKV_DIM // LANES
    items = []
    for b in range(n_batch):
        for d, (at_ref, rt_ref, bt_ref, kt_ref, v_ref, pl_ref, pl_row, strict, incl) in enumerate(dirs):
            p_last = pl_ref[b, 0, pl.ds(pl_row, 1), :]
            for p in range(n_pairs):
                sl = slice(p * LANES, (p + 1) * LANES)
                items.append((at_ref[b, :, sl], rt_ref[b, :, sl], bt_ref[b, :, sl], kt_ref[b, :, sl],
                              v_ref[b, :, sl], p_last[:, sl], s_ref[b, d, p], strict, incl))
    outs = []
    scan = _pair_chunks(items, levels, eye, lane0, bd_mask, outs)
    attn = _attn_stages(*attn_in, o_att_ref)
    while next(scan, StopIteration) is not StopIteration:
        for _ in range(ATTN_PER_SCAN_STAGE):
            next(attn, None)
    for _ in attn:
        pass
    for b in range(n_batch):
        for d, y_ref in enumerate((yf_ref, yb_ref)):
            base = (b * 2 + d) * n_pairs
            y_ref[b] = jnp.concatenate([outs[base + p][0] for p in range(n_pairs)],
                                       axis=1).astype(BF16)
            for p in range(n_pairs):
                s_ref[b, d, p] = outs[base + p][1]


def _attn_stages(sink_ref, *refs):
    n_sub = ATT_QBLOCKS
    bias_refs = refs[:n_sub]
    q_ref, kp_ref, kc_ref, kn_ref, vp_ref, vc_ref, vn_ref, o_ref = refs[n_sub:]
    blk = WINDOW
    k_all = jnp.concatenate([kp_ref[0], kc_ref[0], kn_ref[0]], axis=0)
    v_all = jnp.concatenate([vp_ref[0], vc_ref[0], vn_ref[0]], axis=0)

    def swap_halves(x):
        return jnp.concatenate([x[:, HEAD_DIM:], x[:, :HEAD_DIM]], axis=1)

    lane0 = lax.broadcasted_iota(jnp.int32, k_all.shape, 1) < HEAD_DIM

    def variants(x):
        xs = swap_halves(x)
        zero = jnp.zeros_like(x)
        return ((jnp.where(lane0, x, zero), jnp.where(lane0, zero, xs)),
                (jnp.where(lane0, xs, zero), jnp.where(lane0, zero, x)))

    k_var = variants(k_all)
    v_var = variants(v_all)
    yield

    group = ATT_HEADS // (KV_DIM // HEAD_DIM)

    items = [(u, h) for u in range(n_sub) for h in range(ATT_HEADS)]
    keys = lambda var, u, h: var[h // group][h % 2][u * blk:(u + 3) * blk]
    q_pairs = [[q_ref[0, u * blk:(u + 1) * blk, j * LANES:(j + 1) * LANES]
                * jnp.asarray(HEAD_DIM ** -0.5, BF16) for j in range(ATT_DIM // LANES)]
               for u in range(n_sub)]
    s = [_dot_nt(q_pairs[u][h // 2], keys(k_var, u, h)) + bias_refs[u][0, h] for u, h in items]
    yield
    m = [jnp.maximum(jnp.max(s[i], axis=-1, keepdims=True), sink_ref[h]) for i, (u, h) in enumerate(items)]
    yield
    p = [jnp.exp(s[i] - m[i]) for i in range(len(items))]
    yield
    den = [jnp.sum(p[i], axis=-1, keepdims=True) + jnp.exp(sink_ref[h] - m[i])
           for i, (u, h) in enumerate(items)]
    yield
    o = [_dot(p[i], keys(v_var, u, h)) * (1.0 / den[i]) for i, (u, h) in enumerate(items)]
    yield
    for u in range(n_sub):
        ou = o[u * ATT_HEADS:(u + 1) * ATT_HEADS]
        o_ref[0, u * blk:(u + 1) * blk, :] = jnp.concatenate(
            [ou[2 * j] + ou[2 * j + 1] for j in range(ATT_DIM // LANES)], axis=1).astype(o_ref.dtype)


def _attn_bias(blk):
    qi = np.arange(blk)[:, None]
    kj = np.arange(3 * blk)[None, :]
    dist = np.abs(kj - blk - qi)
    slopes = 2.0 ** (-8.0 * np.arange(1, ATT_HEADS + 1, dtype=np.float32) / ATT_HEADS)
    alibi = -slopes[:, None, None].astype(np.float32) * dist[None].astype(np.float32)
    out = []
    for has_prev, has_next in ((False, True), (True, True), (True, False)):
        valid = (dist <= WINDOW) & (has_prev | (kj >= blk)) & (has_next | (kj < 2 * blk))
        out.append(np.where(valid[None], alibi, np.float32(MASK_VALUE)))
    return jnp.asarray(np.stack(out), F32)


def _scan_attention(at0, rt0, bt0, kt0, at1, rt1, bt1, kt1, v, pl0, pl1, q, ka, va, sink, weights):
    B, T, C = v.shape
    nc = T // CHUNK
    blk = WINDOW
    n_sub = ATT_QBLOCKS
    per_row = T // (blk * n_sub)
    nb = T // blk
    assert B * per_row == nc, "scan and attention must have the same number of grid steps"
    fwd = pl.BlockSpec((B, CHUNK, C), lambda c: (0, c, 0))
    bwd = pl.BlockSpec((B, CHUNK, C), lambda c: (0, nc - 1 - c, 0))
    cpb = TM_IN // CHUNK
    pl_f = pl.BlockSpec((B, 1, SUBLANES, C), lambda c: (0, c // cpb, 0, 0))
    pl_b = pl.BlockSpec((B, 1, SUBLANES, C), lambda c: (0, (nc - 1 - c) // cpb, 0, 0))
    y_shape = jax.ShapeDtypeStruct((B, T, C), BF16)

    row = lambda c: c // per_row
    step = lambda c: lax.rem(c, per_row)
    cur = lambda width: pl.BlockSpec((1, n_sub * blk, width), lambda c: (row(c), step(c), 0))
    prev = pl.BlockSpec((1, blk, KV_DIM), lambda c: (row(c), jnp.maximum(step(c) * n_sub - 1, 0), 0))
    nxt = pl.BlockSpec((1, blk, KV_DIM), lambda c: (row(c), jnp.minimum((step(c) + 1) * n_sub, nb - 1), 0))

    def bias_spec(u):
        def index(c):
            g = step(c) * n_sub + u
            return (jnp.where(g == 0, 0, jnp.where(g == nb - 1, 2, 1)), 0, 0, 0)
        return pl.BlockSpec((1, ATT_HEADS, blk, 3 * blk), index)

    def cast_spec(w):
        rows, cols = w.shape
        per_step = next(r for r in range(BF16_ROWS, rows + 1, BF16_ROWS)
                        if rows % r == 0 and rows // r <= nc)
        last = rows // per_step - 1
        return pl.BlockSpec((per_step, cols), lambda c: (jnp.minimum(c, last), 0))

    cast_specs = [cast_spec(w) for w in weights]
    bias = _attn_bias(blk)
    outs = pl.pallas_call(
        functools.partial(_scan_attn_kernel, n_cast=len(weights), chunks_per_block=cpb), grid=(nc,),
        in_specs=[fwd, fwd, fwd, fwd, fwd, pl_f, bwd, bwd, bwd, bwd, bwd, pl_b,
                  pl.BlockSpec(memory_space=pltpu.SMEM)] + [bias_spec(u) for u in range(n_sub)]
                 + [cur(ATT_DIM), prev, cur(KV_DIM), nxt, prev, cur(KV_DIM), nxt] + cast_specs,
        out_specs=[fwd, bwd, cur(ATT_DIM)] + cast_specs,
        out_shape=[y_shape, y_shape, jax.ShapeDtypeStruct((B, T, ATT_DIM), BF16)]
                  + [jax.ShapeDtypeStruct(w.shape, BF16) for w in weights],
        scratch_shapes=[pltpu.VMEM((B, 2, C // LANES, LANES, LANES), F32)],
        compiler_params=pltpu.CompilerParams(dimension_semantics=("arbitrary",),
                                             vmem_limit_bytes=VMEM_LIMIT),
        name="scan_attn",
    )(at0, rt0, bt0, kt0, v, pl0, at1, rt1, bt1, kt1, v, pl1,
      sink, *([bias] * n_sub), q, ka, ka, ka, va, va, va, *weights)
    return outs[0], outs[1], outs[2], outs[3:]


HALO = 16


def _mix_ffn_kernel(*refs):
    (x_m, x_p, x_n, yf_m, yf_p, yf_n, yb_m, yb_p, yb_n, bo_m, bo_p, bo_n, g_m, g_p, g_n,
     oa_m, oa_p, oa_n, lg_ref, lb_ref, mean_ref, wo_ref, ln2_ref, wg_ref, wu_ref, cw_ref, cb_ref,
     wd_ref, lnf_ref, o_ref) = refs
    i = pl.program_id(1)
    n_tiles = pl.num_programs(1)
    tm = x_m.shape[1]
    rows = tm + 2 * HALO
    core = slice(HALO, tm + HALO)
    ext = lambda m, p, n: jnp.concatenate([p[0], m[0], n[0]], axis=0)

    def seg_mean(t):
        return _dot(t, mean_ref[...])

    y = ext(yf_m, yf_p, yf_n).astype(F32) + ext(yb_m, yb_p, yb_n).astype(F32)
    d = y - seg_mean(y)
    var = seg_mean(d * d)
    yn = d * lax.rsqrt(var + LNX_EPS) * lg_ref[...] + lb_ref[...]
    o_rwkv = (yn + ext(bo_m, bo_p, bo_n)) * ext(g_m, g_p, g_n)
    mix = _dot(o_rwkv, wo_ref[:RWKV_DIM, :]) + _dot(ext(oa_m, oa_p, oa_n), wo_ref[RWKV_DIM:, :])
    x1 = ext(x_m, x_p, x_n) + mix
    r = lax.broadcasted_iota(jnp.int32, (rows, 1), 0)
    inside = ((r >= HALO) | (i > 0)) & ((r < tm + HALO) | (i < n_tiles - 1))
    x1 = jnp.where(inside, x1, 0.0)

    h = _rms_norm(x1, ln2_ref[...]).astype(BF16)
    h_core = h[core]
    acc = jnp.zeros((tm, D_MODEL), F32)
    for c0, c1 in zip(FF_SPLITS[:-1], FF_SPLITS[1:]):
        cs = slice(c0, c1)
        gp_ext = _dot(h, wg_ref[:, cs])
        prev = pltpu.roll(gp_ext, 1, axis=0)[core]
        nxt = pltpu.roll(gp_ext, rows - 1, axis=0)[core]
        gate = (prev * cw_ref[0:1, cs] + gp_ext[core] * cw_ref[1:2, cs] + nxt * cw_ref[2:3, cs]
                + cb_ref[:, cs])
        act = 0.5 * gate * (1.0 + lax.erf(gate * float(1.0 / np.sqrt(2.0))))
        up = _dot(h_core, wu_ref[:, cs])
        acc = acc + _dot(act * up, wd_ref[cs, :])
    o_ref[0] = _rms_norm(x1[core] + acc, lnf_ref[...])


def _mix_ffn(x, yf, yb, bonus, g, o_att, lnx_g, lnx_b, mean_blk, w_out, ln2_g, wg, wu, conv_w, conv_b,
             wd, lnf_g):
    B, T, D = x.shape
    tm = TM_FFN
    per_tile = tm // HALO
    last = T // HALO - 1

    def tok(width):
        return [pl.BlockSpec((1, tm, width), lambda b, i: (b, i, 0)),
                pl.BlockSpec((1, HALO, width), lambda b, i: (b, jnp.maximum(i * per_tile - 1, 0), 0)),
                pl.BlockSpec((1, HALO, width), lambda b, i: (b, jnp.minimum((i + 1) * per_tile, last), 0))]

    const = lambda shape: pl.BlockSpec(shape, lambda b, i: (0,) * len(shape))
    resident = lambda shape: pl.BlockSpec(shape, lambda b, i: (0,) * len(shape),
                                          pipeline_mode=pl.Buffered(1))
    tokens = (x, yf, yb, bonus, g, o_att)
    return pl.pallas_call(
        _mix_ffn_kernel, grid=(B, T // tm),
        in_specs=[spec for a in tokens for spec in tok(a.shape[-1])]
                 + [const((1, RWKV_DIM)), const((1, RWKV_DIM)), const((RWKV_DIM, RWKV_DIM)), const((D, D)),
                    const((1, D)), resident((D, D_FF)), resident((D, D_FF)),
                    const((CONV_WIDTH, D_FF)), const((1, D_FF)), resident((D_FF, D)), const((1, D))],
        out_specs=pl.BlockSpec((1, tm, D), lambda b, i: (b, i, 0)),
        out_shape=jax.ShapeDtypeStruct((B, T, D), F32),
        compiler_params=pltpu.CompilerParams(dimension_semantics=("parallel", "parallel"),
                                             vmem_limit_bytes=VMEM_LIMIT),
        name="mix_ffn",
    )(*[a for a in tokens for _ in range(3)], lnx_g, lnx_b, mean_blk, w_out, ln2_g, wg, wu, conv_w,
      conv_b, wd, lnf_g)


def _constants():
    idx = np.arange(RWKV_DIM)
    same_head = (idx[:, None] // HEAD_DIM) == (idx[None, :] // HEAD_DIM)
    t = np.arange(CHUNK)
    tri = np.stack([t[:, None] >= t[None, :], t[:, None] <= t[None, :]]).astype(np.float32)
    return (jnp.asarray(same_head.astype(np.float32), BF16),
            jnp.asarray(same_head.astype(np.float32) / HEAD_DIM, BF16),
            jnp.asarray(np.concatenate([tri, tri], axis=2), BF16))


def kernel(x, ln1_g, w_in, shift_mu_prev, shift_mu_next, decay_w0, decay_w2, iclr_a0, iclr_a2,
           gate_g2, k_k, k_a, r_k, lnx_g, lnx_b, attn_sink, w_out, ln2_g, ffn_w_gate, ffn_w_up,
           ffn_conv_w, ffn_conv_b, ffn_w_down, lnf_g):
    B, T, _ = x.shape
    assert w_in.shape[0] == 1, "single-layer block"
    l = 0
    ones_blk, mean_blk, tri = _constants()
    row = lambda a: a.reshape(1, -1)
    w2 = decay_w2[l]
    w2_pad = jnp.concatenate([w2, jnp.zeros_like(w2)], axis=1)
    w2_bf = w2_pad.astype(BF16)
    w2_cat = jnp.concatenate([w2_bf, w2_bf], axis=1)
    mu_p, mu_n = shift_mu_prev[l], shift_mu_next[l]
    a2_pad = jnp.concatenate([jnp.zeros_like(iclr_a2[l]), iclr_a2[l]], axis=0).astype(BF16)
    (at0, rt0, bt0, kt0, at1, rt1, bt1, kt1, v, pl0, pl1, g, bonus, q, ka, va) = _in_proj(
        x, row(ln1_g[l]), w_in[l], row(1.0 - mu_p - mu_n), row(mu_p), row(mu_n),
        decay_w0[l], w2_cat, row(iclr_a0[l]), a2_pad, gate_g2[l].astype(BF16),
        row(k_k[l]), row(k_a[l]), row(r_k[l]), ones_blk, tri)
    yf, yb, o_att, (wo, wg, wu, wd) = _scan_attention(
        at0, rt0, bt0, kt0, at1, rt1, bt1, kt1, v, pl0, pl1, q, ka, va, attn_sink[l],
        (w_out[l], ffn_w_gate[l], ffn_w_up[l], ffn_w_down[l]))
    return _mix_ffn(x, yf, yb, bonus, g, o_att, row(lnx_g[l]), row(lnx_b[l]), mean_blk, wo,
                    row(ln2_g[l]), wg, wu, ffn_conv_w[l], row(ffn_conv_b[l]), wd, row(lnf_g))
```

```python
import functools

import numpy as np
import jax
import jax.numpy as jnp
from jax import lax
from jax.experimental import pallas as pl
from jax.experimental.pallas import tpu as pltpu

F32 = jnp.float32
BF16 = jnp.bfloat16

D_MODEL = 1024
HEAD_DIM = 64
RWKV_DIM = 512
ATT_DIM = 512
ATT_HEADS = 8
KV_DIM = 128
LORA_DIM = 256
SHIFT_DIM = 3 * RWKV_DIM + LORA_DIM
PROJ_DIM = SHIFT_DIM + ATT_DIM + 2 * KV_DIM
WINDOW = 128
D_FF = 2816
CONV_WIDTH = 3
NORM_EPS = 1e-6
LNX_EPS = 64e-5
L2_EPS = 1e-12
MASK_VALUE = -1e30
LOG2E = float(np.log2(np.e))
NEG_DECAY_SCALE = float(-np.exp(-0.5) * np.log2(np.e))

LANES = 128
SUBLANES = 8
BF16_ROWS = 16
CHUNK = 64
VMEM_LIMIT = 56 * 1024 * 1024

TM_IN = 512
IN_PARTS = 2
TM_FFN = 512
SCAN_CHUNKS = 2
ATT_QBLOCKS = 4
ATTN_PER_SCAN_STAGE = 1
FF_SPLITS = (0, 1536, D_FF)


def _dot(a, b):
    return jnp.dot(a.astype(BF16), b.astype(BF16), preferred_element_type=F32)


def _dot_nt(a, b):
    return lax.dot_general(a.astype(BF16), b.astype(BF16), (((1,), (1,)), ((), ())),
                           preferred_element_type=F32)


def _dot_tn(a, b):
    return lax.dot_general(a.astype(BF16), b.astype(BF16), (((0,), (0,)), ((), ())),
                           preferred_element_type=F32)


def _split2(x):
    hi = x.astype(BF16)
    lo = (x - hi.astype(F32)).astype(BF16)
    return hi, lo


def _rms_norm(x, g):
    return x * lax.rsqrt(jnp.mean(x * x, axis=-1, keepdims=True) + NORM_EPS) * g


def _in_proj_kernel(x_ref, xp_ref, xn_ref, ln1_ref, w_ref, muc_ref, mup_ref, mun_ref, w0_ref,
                    w2_ref, a0_ref, a2_ref, g2_ref, kk_ref, ka_ref, rk_ref, ones_ref, tri_ref,
                    at0_ref, rt0_ref, bt0_ref, kt0_ref, at1_ref, rt1_ref, bt1_ref, kt1_ref,
                    v_ref, pl0_ref, pl1_ref, g_ref, bonus_ref, q_ref, ka_o_ref, va_o_ref, wbf_ref):
    i = pl.program_id(1)
    n_tiles = pl.num_programs(1)
    tm = x_ref.shape[1]

    @pl.when((pl.program_id(0) == 0) & (i == 0))
    def _():
        wbf_ref[...] = w_ref[...].astype(BF16)

    hm = tm // IN_PARTS
    rows = hm + 2 * SUBLANES
    core = slice(SUBLANES, hm + SUBLANES)
    halo_lo = jnp.where(i > 0, xp_ref[0], 0.0)
    halo_hi = jnp.where(i < n_tiles - 1, xn_ref[0], 0.0)
    dir_outs = ((at0_ref, rt0_ref, bt0_ref, kt0_ref, pl0_ref),
                (at1_ref, rt1_ref, bt1_ref, kt1_ref, pl1_ref))

    def seg_sum(t):
        return _dot(t, ones_ref[...])

    def part(lo):
        out = slice(lo, lo + hm)
        before = halo_lo if lo == 0 else x_ref[0, lo - SUBLANES:lo, :]
        after = halo_hi if lo + hm == tm else x_ref[0, lo + hm:lo + hm + SUBLANES, :]
        x_ext = jnp.concatenate([before, x_ref[0, out, :], after], axis=0)
        h = _rms_norm(x_ext, ln1_ref[...]).astype(BF16)
        proj = lambda c0, c1: _dot(h, wbf_ref[:, c0:c1])
        p_codes = proj(3 * RWKV_DIM, SHIFT_DIM)
        p_k = proj(RWKV_DIM, 2 * RWKV_DIM)
        p_r = proj(0, RWKV_DIM)
        p_v = proj(2 * RWKV_DIM, 3 * RWKV_DIM)
        att = proj(SHIFT_DIM, PROJ_DIM)[core]
        yield

        def shifted(p_ext, c0, c1):
            prev = pltpu.roll(p_ext, 1, axis=0)[core]
            nxt = pltpu.roll(p_ext, rows - 1, axis=0)[core]
            return (p_ext[core] * muc_ref[:, c0:c1] + prev * mup_ref[:, c0:c1]
                    + nxt * mun_ref[:, c0:c1])

        codes = shifted(p_codes, 3 * RWKV_DIM, SHIFT_DIM)
        c_di = codes[:, :LANES]
        th_hi, th_lo = _split2(jnp.tanh(c_di))
        th_cat = jnp.concatenate([th_hi, th_lo], axis=1)
        gate_code = jax.nn.sigmoid(codes[:, LANES:])
        k = shifted(p_k, RWKV_DIM, 2 * RWKV_DIM)
        kkr = k * kk_ref[...]
        kkr_sq = kkr * kkr
        r = shifted(p_r, 0, RWKV_DIM)
        v = shifted(p_v, 2 * RWKV_DIM, 3 * RWKV_DIM)
        v_ref[0, out, :] = v.astype(BF16)
        q_ref[0, out, :] = att[:, :ATT_DIM].astype(BF16)
        ka_o_ref[0, out, :] = att[:, ATT_DIM:ATT_DIM + KV_DIM].astype(BF16)
        va_o_ref[0, out, :] = att[:, ATT_DIM + KV_DIM:].astype(BF16)
        yield

        a_pre = _dot(c_di, a2_ref[...])
        g_ref[0, out, :] = _dot(gate_code, g2_ref[...]).astype(BF16)
        n2 = seg_sum(kkr_sq)
        z = [w0_ref[d:d + 1, :] + jnp.dot(th_cat, w2_ref[d], preferred_element_type=F32)
             for d in range(2)]
        yield

        a_vec = jax.nn.sigmoid(a0_ref[...] + a_pre)
        kk = kkr * lax.rsqrt(jnp.maximum(n2, L2_EPS * L2_EPS))
        k2 = k * (1.0 + (a_vec - 1.0) * ka_ref[...])
        b_vec = kk * a_vec
        neg_kk = -kk
        rk2 = r * k2 * rk_ref[...]
        lw = [NEG_DECAY_SCALE / (1.0 + jnp.exp2(z[d] * (-LOG2E))) for d in range(2)]
        lw_split = [_split2(lw[d]) for d in range(2)]
        yield

        bonus_sum = seg_sum(rk2)
        parts = [[jnp.dot(tri_ref[d],
                          jnp.concatenate([lw_split[d][0][cs], lw_split[d][1][cs]], axis=0),
                          preferred_element_type=F32)
                  for cs in (slice(j * CHUNK, (j + 1) * CHUNK) for j in range(hm // CHUNK))]
                 for d in range(2)]
        yield

        bonus_ref[0, out, :] = (bonus_sum * v).astype(BF16)
        for d, (at_ref, rt_ref, bt_ref, kt_ref, pl_ref) in enumerate(dir_outs):
            ci = jnp.concatenate(parts[d], axis=0)
            end = 0 if d else CHUNK - 1
            tot = jnp.concatenate([c[end:end + 1] for c in parts[d]], axis=0)
            pl_ref[0, 0, lo // CHUNK:(lo + hm) // CHUNK, :] = jnp.exp2(tot)
            e_inc = jnp.exp2(ci)
            e_exc = jnp.exp2(ci - lw[d])
            e_inv = 1.0 / e_inc
            at_ref[0, out, :] = (neg_kk * e_exc).astype(BF16)
            rt_ref[0, out, :] = (r * e_inc).astype(BF16)
            bt_ref[0, out, :] = (b_vec * e_inv).astype(BF16)
            kt_ref[0, out, :] = (k2 * e_inv).astype(BF16)

    live = [part(p * hm) for p in range(IN_PARTS)]
    while live:
        live = [g for g in live if next(g, StopIteration) is not StopIteration]
    if tm // CHUNK < SUBLANES:
        for d in range(2):
            dir_outs[d][4][0, 0, tm // CHUNK:, :] = jnp.ones((SUBLANES - tm // CHUNK, RWKV_DIM), F32)


def _in_proj(x, ln1_g, w_in, mu_cur, mu_prev, mu_next, decay_w0, w2_cat, iclr_a0, a2_pad, gate_g2,
             k_k, k_a, r_k, ones_blk, tri):
    B, T, D = x.shape
    tm = TM_IN
    nt = T // tm
    rows8 = tm // SUBLANES
    const = lambda shape: pl.BlockSpec(shape, lambda b, i: (0,) * len(shape))
    tok = lambda width: pl.BlockSpec((1, tm, width), lambda b, i: (b, i, 0))
    in_specs = [
        tok(D),
        pl.BlockSpec((1, SUBLANES, D), lambda b, i: (b, jnp.maximum(i * rows8 - 1, 0), 0)),
        pl.BlockSpec((1, SUBLANES, D), lambda b, i: (b, jnp.minimum((i + 1) * rows8, T // SUBLANES - 1), 0)),
        const((1, D)),
        pl.BlockSpec((D, PROJ_DIM), lambda b, i: (0, 0), pipeline_mode=pl.Buffered(1)),
        const((1, SHIFT_DIM)), const((1, SHIFT_DIM)), const((1, SHIFT_DIM)),
        const((2, RWKV_DIM)), const((2, 2 * LANES, RWKV_DIM)),
        const((1, RWKV_DIM)), const((LANES, RWKV_DIM)), const((LANES, RWKV_DIM)),
        const((1, RWKV_DIM)), const((1, RWKV_DIM)), const((1, RWKV_DIM)),
        const((RWKV_DIM, RWKV_DIM)), const((2, CHUNK, 2 * CHUNK)),
    ]
    tok_bf = jax.ShapeDtypeStruct((B, T, RWKV_DIM), BF16)
    pl_shape = jax.ShapeDtypeStruct((B, nt, SUBLANES, RWKV_DIM), F32)
    pl_spec = pl.BlockSpec((1, 1, SUBLANES, RWKV_DIM), lambda b, i: (b, i, 0, 0))
    out_shape = [tok_bf] * 9 + [pl_shape, pl_shape, tok_bf, tok_bf,
                                jax.ShapeDtypeStruct((B, T, ATT_DIM), BF16),
                                jax.ShapeDtypeStruct((B, T, KV_DIM), BF16),
                                jax.ShapeDtypeStruct((B, T, KV_DIM), BF16)]
    out_specs = [tok(RWKV_DIM)] * 9 + [pl_spec, pl_spec, tok(RWKV_DIM), tok(RWKV_DIM),
                                       tok(ATT_DIM), tok(KV_DIM), tok(KV_DIM)]
    return pl.pallas_call(
        _in_proj_kernel, grid=(B, nt), in_specs=in_specs, out_specs=out_specs, out_shape=out_shape,
        scratch_shapes=[pltpu.VMEM((D, PROJ_DIM), BF16)],
        compiler_params=pltpu.CompilerParams(dimension_semantics=("arbitrary", "arbitrary"),
                                             vmem_limit_bytes=VMEM_LIMIT),
        name="in_proj",
    )(x, x, x, ln1_g, w_in, mu_cur, mu_prev, mu_next, decay_w0, w2_cat, iclr_a0, a2_pad, gate_g2,
      k_k, k_a, r_k, ones_blk, tri)


def _pair_chunks(items, states, levels, eye, lane0, bd_mask, ys):
    n = range(len(items))
    at, rt, bt, kt, v, p_last, strict, incl = zip(*items)

    def bd(x):
        x = x.astype(BF16)
        zero = jnp.zeros_like(x)
        return jnp.concatenate([jnp.where(lane0, x, zero), jnp.where(lane0, zero, x)], axis=0)

    def pmm(x, y):
        return _dot(x, bd(y))

    sc = [_dot_nt(jnp.concatenate([at[i], rt[i]], axis=0),
                  jnp.concatenate([bd(bt[i]), bd(kt[i])], axis=0)) for i in n]
    a_ab = [jnp.where(strict[i], sc[i][:CHUNK, :LANES], 0.0) for i in n]
    a_ak = [jnp.where(strict[i], sc[i][:CHUNK, LANES:], 0.0) for i in n]
    a_rb = [jnp.where(incl[i], sc[i][CHUNK:, :LANES], 0.0) for i in n]
    a_rk = [jnp.where(incl[i], sc[i][CHUNK:, LANES:], 0.0) for i in n]
    yield

    xy = [pmm(jnp.concatenate([a_ak[i], a_rk[i]], axis=0), v[i]) for i in n]
    x1 = [xy[i][:CHUNK] for i in n]
    yk = [xy[i][CHUNK:] for i in n]
    t_inv = [eye + jnp.where(levels[0], a_ab[i], 0.0) for i in n]
    yield
    for level in levels[1:]:
        e_t = [pmm(jnp.where(level, a_ab[i], 0.0), t_inv[i]) for i in n]
        yield
        t_inv = [t_inv[i] + pmm(t_inv[i], e_t[i]) for i in n]
        yield
    wu = [_dot(t_inv[i], jnp.concatenate([bd(at[i]), bd(x1[i])], axis=1)) for i in n]
    yield
    m = len(states)
    for first in range(0, len(items), m):
        n = range(first, first + m)
        hs = {i: _dot_nt(jnp.concatenate([wu[i][:, :LANES].astype(BF16), rt[i]], axis=0), states[i - first])
              for i in n}
        yield
        u = {i: hs[i][:CHUNK] + wu[i][:, LANES:] for i in n}
        ys.extend(hs[i][CHUNK:] + pmm(a_rb[i], u[i]) + yk[i] for i in n)
        yield
        upd = {i: _dot_tn(jnp.concatenate([u[i].astype(BF16), v[i]], axis=0),
                          jnp.concatenate([bt[i], kt[i]], axis=0)) for i in n}
        states[:] = [(states[i - first] + jnp.where(bd_mask, upd[i], 0.0)) * p_last[i] for i in n]
        yield


def _scan_attn_kernel(at0_ref, rt0_ref, bt0_ref, kt0_ref, v0_ref, pl0_ref,
                      at1_ref, rt1_ref, bt1_ref, kt1_ref, v1_ref, pl1_ref, *rest, n_cast,
                      chunks_per_block):
    n_attn = ATT_QBLOCKS + 8
    attn_in = rest[:n_attn]
    cast_in = rest[n_attn:n_attn + n_cast]
    yf_ref, yb_ref, o_att_ref = rest[n_attn + n_cast:n_attn + n_cast + 3]
    cast_out = rest[n_attn + n_cast + 3:n_attn + 2 * n_cast + 3]
    s_ref = rest[-1]
    c = pl.program_id(0)
    for src_ref, dst_ref in zip(cast_in, cast_out):
        dst_ref[...] = src_ref[...].astype(BF16)

    @pl.when(c == 0)
    def _():
        s_ref[...] = jnp.zeros_like(s_ref)

    n_batch = v0_ref.shape[0]
    ri = lax.broadcasted_iota(jnp.int32, (CHUNK, LANES), 0)
    ci = lax.broadcasted_iota(jnp.int32, (CHUNK, LANES), 1)
    cj = jnp.where(ci >= CHUNK, ci - CHUNK, ci)
    lane0 = ci < CHUNK
    eye = jnp.where(ri == cj, 1.0, 0.0).astype(F32)
    levels = [((ri // (2 * s)) == (cj // (2 * s))) & ((ri // s) != (cj // s))
              for s in (2 ** e for e in range(CHUNK.bit_length() - 1))]
    r2 = lax.broadcasted_iota(jnp.int32, (LANES, LANES), 0)
    c2 = lax.broadcasted_iota(jnp.int32, (LANES, LANES), 1)
    bd_mask = (r2 >= CHUNK) == (c2 >= CHUNK)
    dirs = ((at0_ref, rt0_ref, bt0_ref, kt0_ref, v0_ref, pl0_ref, ri > cj, ri >= cj),
            (at1_ref, rt1_ref, bt1_ref, kt1_ref, v1_ref, pl1_ref, ri < cj, ri <= cj))
    n_pairs = RWKV_DIM // LANES
    n_chunks = pl.num_programs(0) * SCAN_CHUNKS
    order = [(b, d, p) for b in range(n_batch) for d in range(2) for p in range(n_pairs)]
    items = []
    for j in range(SCAN_CHUNKS):
        first = c * SCAN_CHUNKS + j
        chunk = (first, n_chunks - 1 - first)
        sub = (j, SCAN_CHUNKS - 1 - j)
        p_last = {(b, d): dirs[d][5][b, 0, pl.ds(lax.rem(chunk[d], chunks_per_block), 1), :]
                  for b in range(n_batch) for d in range(2)}
        for b, d, p in order:
            at_ref, rt_ref, bt_ref, kt_ref, v_ref, _, strict, incl = dirs[d]
            rows = slice(sub[d] * CHUNK, (sub[d] + 1) * CHUNK)
            sl = slice(p * LANES, (p + 1) * LANES)
            items.append((at_ref[b, rows, sl], rt_ref[b, rows, sl], bt_ref[b, rows, sl],
                          kt_ref[b, rows, sl], v_ref[b, rows, sl], p_last[b, d][:, sl], strict, incl))
    states = [s_ref[b, d, p] for b, d, p in order]
    ys = []
    scan = _pair_chunks(items, states, levels, eye, lane0, bd_mask, ys)
    attn = _attn_stages(*attn_in, o_att_ref)
    while next(scan, StopIteration) is not StopIteration:
        for _ in range(ATTN_PER_SCAN_STAGE):
            next(attn, None)
    for _ in attn:
        pass
    for j in range(SCAN_CHUNKS):
        sub = (j, SCAN_CHUNKS - 1 - j)
        for b in range(n_batch):
            for d, y_ref in enumerate((yf_ref, yb_ref)):
                base = j * len(order) + (b * 2 + d) * n_pairs
                y_ref[b, sub[d] * CHUNK:(sub[d] + 1) * CHUNK, :] = jnp.concatenate(
                    ys[base:base + n_pairs], axis=1).astype(BF16)
    for (b, d, p), state in zip(order, states):
        s_ref[b, d, p] = state


def _attn_stages(sink_ref, *refs):
    n_sub = ATT_QBLOCKS
    bias_refs = refs[:n_sub]
    q_ref, kp_ref, kc_ref, kn_ref, vp_ref, vc_ref, vn_ref, o_ref = refs[n_sub:]
    blk = WINDOW
    k_all = jnp.concatenate([kp_ref[0], kc_ref[0], kn_ref[0]], axis=0)
    v_all = jnp.concatenate([vp_ref[0], vc_ref[0], vn_ref[0]], axis=0)

    def swap_halves(x):
        return jnp.concatenate([x[:, HEAD_DIM:], x[:, :HEAD_DIM]], axis=1)

    lane0 = lax.broadcasted_iota(jnp.int32, k_all.shape, 1) < HEAD_DIM

    def variants(x):
        xs = swap_halves(x)
        zero = jnp.zeros_like(x)
        return ((jnp.where(lane0, x, zero), jnp.where(lane0, zero, xs)),
                (jnp.where(lane0, xs, zero), jnp.where(lane0, zero, x)))

    k_var = variants(k_all)
    v_var = variants(v_all)
    yield

    group = ATT_HEADS // (KV_DIM // HEAD_DIM)

    items = [(u, h) for u in range(n_sub) for h in range(ATT_HEADS)]
    keys = lambda var, u, h: var[h // group][h % 2][u * blk:(u + 3) * blk]
    q_pairs = [[q_ref[0, u * blk:(u + 1) * blk, j * LANES:(j + 1) * LANES]
                * jnp.asarray(HEAD_DIM ** -0.5, BF16) for j in range(ATT_DIM // LANES)]
               for u in range(n_sub)]
    s = [_dot_nt(q_pairs[u][h // 2], keys(k_var, u, h)) + bias_refs[u][0, h] for u, h in items]
    yield
    m = [jnp.maximum(jnp.max(s[i], axis=-1, keepdims=True), sink_ref[h]) for i, (u, h) in enumerate(items)]
    yield
    p = [jnp.exp(s[i] - m[i]) for i in range(len(items))]
    yield
    den = [jnp.sum(p[i], axis=-1, keepdims=True) + jnp.exp(sink_ref[h] - m[i])
           for i, (u, h) in enumerate(items)]
    yield
    o = [_dot(p[i], keys(v_var, u, h)) * (1.0 / den[i]) for i, (u, h) in enumerate(items)]
    yield
    for u in range(n_sub):
        ou = o[u * ATT_HEADS:(u + 1) * ATT_HEADS]
        o_ref[0, u * blk:(u + 1) * blk, :] = jnp.concatenate(
            [ou[2 * j] + ou[2 * j + 1] for j in range(ATT_DIM // LANES)], axis=1).astype(o_ref.dtype)


def _attn_bias(blk):
    qi = np.arange(blk)[:, None]
    kj = np.arange(3 * blk)[None, :]
    dist = np.abs(kj - blk - qi)
    slopes = 2.0 ** (-8.0 * np.arange(1, ATT_HEADS + 1, dtype=np.float32) / ATT_HEADS)
    alibi = -slopes[:, None, None].astype(np.float32) * dist[None].astype(np.float32)
    out = []
    for has_prev, has_next in ((False, True), (True, True), (True, False)):
        valid = (dist <= WINDOW) & (has_prev | (kj >= blk)) & (has_next | (kj < 2 * blk))
        out.append(np.where(valid[None], alibi, np.float32(MASK_VALUE)))
    return jnp.asarray(np.stack(out), F32)


def _scan_attention(at0, rt0, bt0, kt0, at1, rt1, bt1, kt1, v, pl0, pl1, q, ka, va, sink, weights):
    B, T, C = v.shape
    nc = T // (CHUNK * SCAN_CHUNKS)
    blk = WINDOW
    n_sub = ATT_QBLOCKS
    per_row = T // (blk * n_sub)
    nb = T // blk
    assert B * per_row == nc, "scan and attention must have the same number of grid steps"
    fwd = pl.BlockSpec((B, SCAN_CHUNKS * CHUNK, C), lambda c: (0, c, 0))
    bwd = pl.BlockSpec((B, SCAN_CHUNKS * CHUNK, C), lambda c: (0, nc - 1 - c, 0))
    cpb = TM_IN // CHUNK
    assert cpb % SCAN_CHUNKS == 0, "a step's chunks must share one block of chunk decays"
    spb = cpb // SCAN_CHUNKS
    pl_f = pl.BlockSpec((B, 1, SUBLANES, C), lambda c: (0, c // spb, 0, 0))
    pl_b = pl.BlockSpec((B, 1, SUBLANES, C), lambda c: (0, (nc - 1 - c) // spb, 0, 0))
    y_shape = jax.ShapeDtypeStruct((B, T, C), BF16)

    row = lambda c: c // per_row
    step = lambda c: lax.rem(c, per_row)
    cur = lambda width: pl.BlockSpec((1, n_sub * blk, width), lambda c: (row(c), step(c), 0))
    prev = pl.BlockSpec((1, blk, KV_DIM), lambda c: (row(c), jnp.maximum(step(c) * n_sub - 1, 0), 0))
    nxt = pl.BlockSpec((1, blk, KV_DIM), lambda c: (row(c), jnp.minimum((step(c) + 1) * n_sub, nb - 1), 0))

    def bias_spec(u):
        def index(c):
            g = step(c) * n_sub + u
            return (jnp.where(g == 0, 0, jnp.where(g == nb - 1, 2, 1)), 0, 0, 0)
        return pl.BlockSpec((1, ATT_HEADS, blk, 3 * blk), index)

    def cast_spec(w):
        rows, cols = w.shape
        per_step = next(r for r in range(BF16_ROWS, rows + 1, BF16_ROWS)
                        if rows % r == 0 and rows // r <= nc)
        last = rows // per_step - 1
        return pl.BlockSpec((per_step, cols), lambda c: (jnp.minimum(c, last), 0))

    cast_specs = [cast_spec(w) for w in weights]
    bias = _attn_bias(blk)
    outs = pl.pallas_call(
        functools.partial(_scan_attn_kernel, n_cast=len(weights), chunks_per_block=cpb), grid=(nc,),
        in_specs=[fwd, fwd, fwd, fwd, fwd, pl_f, bwd, bwd, bwd, bwd, bwd, pl_b,
                  pl.BlockSpec(memory_space=pltpu.SMEM)] + [bias_spec(u) for u in range(n_sub)]
                 + [cur(ATT_DIM), prev, cur(KV_DIM), nxt, prev, cur(KV_DIM), nxt] + cast_specs,
        out_specs=[fwd, bwd, cur(ATT_DIM)] + cast_specs,
        out_shape=[y_shape, y_shape, jax.ShapeDtypeStruct((B, T, ATT_DIM), BF16)]
                  + [jax.ShapeDtypeStruct(w.shape, BF16) for w in weights],
        scratch_shapes=[pltpu.VMEM((B, 2, C // LANES, LANES, LANES), F32)],
        compiler_params=pltpu.CompilerParams(dimension_semantics=("arbitrary",),
                                             vmem_limit_bytes=VMEM_LIMIT),
        name="scan_attn",
    )(at0, rt0, bt0, kt0, v, pl0, at1, rt1, bt1, kt1, v, pl1,
      sink, *([bias] * n_sub), q, ka, ka, ka, va, va, va, *weights)
    return outs[0], outs[1], outs[2], outs[3:]


HALO = 16


def _mix_ffn_kernel(*refs):
    (x_m, x_p, x_n, yf_m, yf_p, yf_n, yb_m, yb_p, yb_n, bo_m, bo_p, bo_n, g_m, g_p, g_n,
     oa_m, oa_p, oa_n, lg_ref, lb_ref, mean_ref, wo_ref, ln2_ref, wg_ref, wu_ref, cw_ref, cb_ref,
     wd_ref, lnf_ref, o_ref) = refs
    i = pl.program_id(1)
    n_tiles = pl.num_programs(1)
    tm = x_m.shape[1]
    rows = tm + 2 * HALO
    core = slice(HALO, tm + HALO)
    ext = lambda m, p, n: jnp.concatenate([p[0], m[0], n[0]], axis=0)

    def seg_mean(t):
        return _dot(t, mean_ref[...])

    y = ext(yf_m, yf_p, yf_n).astype(F32) + ext(yb_m, yb_p, yb_n).astype(F32)
    d = y - seg_mean(y)
    var = seg_mean(d * d)
    yn = d * lax.rsqrt(var + LNX_EPS) * lg_ref[...] + lb_ref[...]
    o_rwkv = (yn + ext(bo_m, bo_p, bo_n)) * ext(g_m, g_p, g_n)
    mix = _dot(o_rwkv, wo_ref[:RWKV_DIM, :]) + _dot(ext(oa_m, oa_p, oa_n), wo_ref[RWKV_DIM:, :])
    x1 = ext(x_m, x_p, x_n) + mix
    r = lax.broadcasted_iota(jnp.int32, (rows, 1), 0)
    inside = ((r >= HALO) | (i > 0)) & ((r < tm + HALO) | (i < n_tiles - 1))
    x1 = jnp.where(inside, x1, 0.0)

    h = _rms_norm(x1, ln2_ref[...]).astype(BF16)
    h_core = h[core]
    acc = jnp.zeros((tm, D_MODEL), F32)
    for c0, c1 in zip(FF_SPLITS[:-1], FF_SPLITS[1:]):
        cs = slice(c0, c1)
        gp_ext = _dot(h, wg_ref[:, cs])
        prev = pltpu.roll(gp_ext, 1, axis=0)[core]
        nxt = pltpu.roll(gp_ext, rows - 1, axis=0)[core]
        gate = (prev * cw_ref[0:1, cs] + gp_ext[core] * cw_ref[1:2, cs] + nxt * cw_ref[2:3, cs]
                + cb_ref[:, cs])
        act = 0.5 * gate * (1.0 + lax.erf(gate * float(1.0 / np.sqrt(2.0))))
        up = _dot(h_core, wu_ref[:, cs])
        acc = acc + _dot(act * up, wd_ref[cs, :])
    o_ref[0] = _rms_norm(x1[core] + acc, lnf_ref[...])


def _mix_ffn(x, yf, yb, bonus, g, o_att, lnx_g, lnx_b, mean_blk, w_out, ln2_g, wg, wu, conv_w, conv_b,
             wd, lnf_g):
    B, T, D = x.shape
    tm = TM_FFN
    per_tile = tm // HALO
    last = T // HALO - 1

    def tok(width):
        return [pl.BlockSpec((1, tm, width), lambda b, i: (b, i, 0)),
                pl.BlockSpec((1, HALO, width), lambda b, i: (b, jnp.maximum(i * per_tile - 1, 0), 0)),
                pl.BlockSpec((1, HALO, width), lambda b, i: (b, jnp.minimum((i + 1) * per_tile, last), 0))]

    const = lambda shape: pl.BlockSpec(shape, lambda b, i: (0,) * len(shape))
    resident = lambda shape: pl.BlockSpec(shape, lambda b, i: (0,) * len(shape),
                                          pipeline_mode=pl.Buffered(1))
    tokens = (x, yf, yb, bonus, g, o_att)
    return pl.pallas_call(
        _mix_ffn_kernel, grid=(B, T // tm),
        in_specs=[spec for a in tokens for spec in tok(a.shape[-1])]
                 + [const((1, RWKV_DIM)), const((1, RWKV_DIM)), const((RWKV_DIM, RWKV_DIM)), const((D, D)),
                    const((1, D)), resident((D, D_FF)), resident((D, D_FF)),
                    const((CONV_WIDTH, D_FF)), const((1, D_FF)), resident((D_FF, D)), const((1, D))],
        out_specs=pl.BlockSpec((1, tm, D), lambda b, i: (b, i, 0)),
        out_shape=jax.ShapeDtypeStruct((B, T, D), F32),
        compiler_params=pltpu.CompilerParams(dimension_semantics=("parallel", "parallel"),
                                             vmem_limit_bytes=VMEM_LIMIT),
        name="mix_ffn",
    )(*[a for a in tokens for _ in range(3)], lnx_g, lnx_b, mean_blk, w_out, ln2_g, wg, wu, conv_w,
      conv_b, wd, lnf_g)


def _constants():
    idx = np.arange(RWKV_DIM)
    same_head = (idx[:, None] // HEAD_DIM) == (idx[None, :] // HEAD_DIM)
    t = np.arange(CHUNK)
    tri = np.stack([t[:, None] >= t[None, :], t[:, None] <= t[None, :]]).astype(np.float32)
    return (jnp.asarray(same_head.astype(np.float32), BF16),
            jnp.asarray(same_head.astype(np.float32) / HEAD_DIM, BF16),
            jnp.asarray(np.concatenate([tri, tri], axis=2), BF16))


def kernel(x, ln1_g, w_in, shift_mu_prev, shift_mu_next, decay_w0, decay_w2, iclr_a0, iclr_a2,
           gate_g2, k_k, k_a, r_k, lnx_g, lnx_b, attn_sink, w_out, ln2_g, ffn_w_gate, ffn_w_up,
           ffn_conv_w, ffn_conv_b, ffn_w_down, lnf_g):
    B, T, _ = x.shape
    assert w_in.shape[0] == 1, "single-layer block"
    l = 0
    ones_blk, mean_blk, tri = _constants()
    row = lambda a: a.reshape(1, -1)
    w2 = decay_w2[l]
    w2_pad = jnp.concatenate([w2, jnp.zeros_like(w2)], axis=1)
    w2_bf = w2_pad.astype(BF16)
    w2_cat = jnp.concatenate([w2_bf, w2_bf], axis=1)
    mu_p, mu_n = shift_mu_prev[l], shift_mu_next[l]
    a2_pad = jnp.concatenate([jnp.zeros_like(iclr_a2[l]), iclr_a2[l]], axis=0).astype(BF16)
    (at0, rt0, bt0, kt0, at1, rt1, bt1, kt1, v, pl0, pl1, g, bonus, q, ka, va) = _in_proj(
        x, row(ln1_g[l]), w_in[l], row(1.0 - mu_p - mu_n), row(mu_p), row(mu_n),
        decay_w0[l], w2_cat, row(iclr_a0[l]), a2_pad, gate_g2[l].astype(BF16),
        row(k_k[l]), row(k_a[l]), row(r_k[l]), ones_blk, tri)
    yf, yb, o_att, (wo, wg, wu, wd) = _scan_attention(
        at0, rt0, bt0, kt0, at1, rt1, bt1, kt1, v, pl0, pl1, q, ka, va, attn_sink[l],
        (w_out[l], ffn_w_gate[l], ffn_w_up[l], ffn_w_down[l]))
    return _mix_ffn(x, yf, yb, bonus, g, o_att, row(lnx_g[l]), row(lnx_b[l]), mean_blk, wo,
                    row(ln2_g[l]), wg, wu, ffn_conv_w[l], row(ffn_conv_b[l]), wd, row(lnf_g))
```

```python
import functools

import numpy as np
import jax
import jax.numpy as jnp
from jax import lax
from jax.experimental import pallas as pl
from jax.experimental.pallas import tpu as pltpu

F32 = jnp.float32
BF16 = jnp.bfloat16

D_MODEL = 1024
HEAD_DIM = 64
RWKV_DIM = 512
ATT_DIM = 512
ATT_HEADS = 8
KV_DIM = 128
LORA_DIM = 256
SHIFT_DIM = 3 * RWKV_DIM + LORA_DIM
PROJ_DIM = SHIFT_DIM + ATT_DIM + 2 * KV_DIM
WINDOW = 128
D_FF = 2816
CONV_WIDTH = 3
NORM_EPS = 1e-6
LNX_EPS = 64e-5
L2_EPS = 1e-12
MASK_VALUE = -1e30
LOG2E = float(np.log2(np.e))
NEG_DECAY_SCALE = float(-np.exp(-0.5) * np.log2(np.e))

LANES = 128
SUBLANES = 8
BF16_ROWS = 16
CHUNK = 64
VMEM_LIMIT = 56 * 1024 * 1024

TM_IN = 512
IN_PARTS = 2
TM_FFN = 512
SCAN_CHUNKS = 2
ATT_QBLOCKS = 4
FF_SPLITS = (0, 1536, D_FF)


def _dot(a, b):
    return jnp.dot(a.astype(BF16), b.astype(BF16), preferred_element_type=F32)


def _dot_nt(a, b):
    return lax.dot_general(a.astype(BF16), b.astype(BF16), (((1,), (1,)), ((), ())),
                           preferred_element_type=F32)


def _dot_tn(a, b):
    return lax.dot_general(a.astype(BF16), b.astype(BF16), (((0,), (0,)), ((), ())),
                           preferred_element_type=F32)


def _split2(x):
    hi = x.astype(BF16)
    lo = (x - hi.astype(F32)).astype(BF16)
    return hi, lo


def _rms_norm(x, g):
    return x * lax.rsqrt(jnp.mean(x * x, axis=-1, keepdims=True) + NORM_EPS) * g


def _head_sums(t):
    first_head = lax.broadcasted_iota(jnp.int32, (1, LANES), 1) < HEAD_DIM
    tiles = []
    for j in range(t.shape[-1] // LANES):
        tj = t[:, j * LANES:(j + 1) * LANES]
        both = jnp.sum(tj, axis=-1, keepdims=True)
        head0 = jnp.sum(jnp.where(first_head, tj, 0.0), axis=-1, keepdims=True)
        tiles.append(jnp.where(first_head, head0, both - head0))
    return jnp.concatenate(tiles, axis=1)


def _in_proj_kernel(x_ref, xp_ref, xn_ref, ln1_ref, w_ref, muc_ref, mup_ref, mun_ref, w0_ref,
                    w2_ref, a0_ref, a2_ref, g2_ref, kk_ref, ka_ref, rk_ref, tri_ref,
                    at0_ref, rt0_ref, bt0_ref, kt0_ref, at1_ref, rt1_ref, bt1_ref, kt1_ref,
                    v_ref, pl0_ref, pl1_ref, g_ref, bonus_ref, q_ref, ka_o_ref, va_o_ref, wbf_ref):
    i = pl.program_id(1)
    n_tiles = pl.num_programs(1)
    tm = x_ref.shape[1]

    @pl.when((pl.program_id(0) == 0) & (i == 0))
    def _():
        wbf_ref[...] = w_ref[...].astype(BF16)

    hm = tm // IN_PARTS
    rows = hm + 2 * SUBLANES
    core = slice(SUBLANES, hm + SUBLANES)
    halo_lo = jnp.where(i > 0, xp_ref[0], 0.0)
    halo_hi = jnp.where(i < n_tiles - 1, xn_ref[0], 0.0)
    dir_outs = ((at0_ref, rt0_ref, bt0_ref, kt0_ref, pl0_ref),
                (at1_ref, rt1_ref, bt1_ref, kt1_ref, pl1_ref))

    def part(lo):
        out = slice(lo, lo + hm)
        before = halo_lo if lo == 0 else x_ref[0, lo - SUBLANES:lo, :]
        after = halo_hi if lo + hm == tm else x_ref[0, lo + hm:lo + hm + SUBLANES, :]
        x_ext = jnp.concatenate([before, x_ref[0, out, :], after], axis=0)
        h = _rms_norm(x_ext, ln1_ref[...]).astype(BF16)
        proj = lambda c0, c1: _dot(h, wbf_ref[:, c0:c1])
        p_codes = proj(3 * RWKV_DIM, SHIFT_DIM)
        p_k = proj(RWKV_DIM, 2 * RWKV_DIM)
        p_r = proj(0, RWKV_DIM)
        p_v = proj(2 * RWKV_DIM, 3 * RWKV_DIM)
        att = proj(SHIFT_DIM, PROJ_DIM)[core]
        yield

        def shifted(p_ext, c0, c1):
            prev = pltpu.roll(p_ext, 1, axis=0)[core]
            nxt = pltpu.roll(p_ext, rows - 1, axis=0)[core]
            return (p_ext[core] * muc_ref[:, c0:c1] + prev * mup_ref[:, c0:c1]
                    + nxt * mun_ref[:, c0:c1])

        codes = shifted(p_codes, 3 * RWKV_DIM, SHIFT_DIM)
        c_di = codes[:, :LANES]
        th_hi, th_lo = _split2(jnp.tanh(c_di))
        th_cat = jnp.concatenate([th_hi, th_lo], axis=1)
        gate_code = jax.nn.sigmoid(codes[:, LANES:])
        k = shifted(p_k, RWKV_DIM, 2 * RWKV_DIM)
        kkr = k * kk_ref[...]
        kkr_sq = kkr * kkr
        r = shifted(p_r, 0, RWKV_DIM)
        v = shifted(p_v, 2 * RWKV_DIM, 3 * RWKV_DIM)
        v_ref[0, out, :] = v.astype(BF16)
        q_ref[0, out, :] = att[:, :ATT_DIM].astype(BF16)
        ka_o_ref[0, out, :] = att[:, ATT_DIM:ATT_DIM + KV_DIM].astype(BF16)
        va_o_ref[0, out, :] = att[:, ATT_DIM + KV_DIM:].astype(BF16)
        yield

        a_pre = _dot(c_di, a2_ref[...])
        g_ref[0, out, :] = _dot(gate_code, g2_ref[...]).astype(BF16)
        n2 = _head_sums(kkr_sq)
        z = [w0_ref[d:d + 1, :] + jnp.dot(th_cat, w2_ref[d], preferred_element_type=F32)
             for d in range(2)]
        yield

        a_vec = jax.nn.sigmoid(a0_ref[...] + a_pre)
        kk = kkr * lax.rsqrt(jnp.maximum(n2, L2_EPS * L2_EPS))
        k2 = k * (1.0 + (a_vec - 1.0) * ka_ref[...])
        b_vec = kk * a_vec
        neg_kk = -kk
        rk2 = r * k2 * rk_ref[...]
        lw = [NEG_DECAY_SCALE / (1.0 + jnp.exp2(z[d] * (-LOG2E))) for d in range(2)]
        lw_split = [_split2(lw[d]) for d in range(2)]
        yield

        bonus_sum = _head_sums(rk2)
        parts = [[jnp.dot(tri_ref[d],
                          jnp.concatenate([lw_split[d][0][cs], lw_split[d][1][cs]], axis=0),
                          preferred_element_type=F32)
                  for cs in (slice(j * CHUNK, (j + 1) * CHUNK) for j in range(hm // CHUNK))]
                 for d in range(2)]
        yield

        bonus_ref[0, out, :] = (bonus_sum * v).astype(BF16)
        for d, (at_ref, rt_ref, bt_ref, kt_ref, pl_ref) in enumerate(dir_outs):
            ci = jnp.concatenate(parts[d], axis=0)
            end = 0 if d else CHUNK - 1
            tot = jnp.concatenate([c[end:end + 1] for c in parts[d]], axis=0)
            pl_ref[0, 0, lo // CHUNK:(lo + hm) // CHUNK, :] = jnp.exp2(tot)
            e_inc = jnp.exp2(ci)
            e_exc = jnp.exp2(ci - lw[d])
            e_inv = 1.0 / e_inc
            at_ref[0, out, :] = (neg_kk * e_exc).astype(BF16)
            rt_ref[0, out, :] = (r * e_inc).astype(BF16)
            bt_ref[0, out, :] = (b_vec * e_inv).astype(BF16)
            kt_ref[0, out, :] = (k2 * e_inv).astype(BF16)

    live = [part(p * hm) for p in range(IN_PARTS)]
    while live:
        live = [g for g in live if next(g, StopIteration) is not StopIteration]
    if tm // CHUNK < SUBLANES:
        for d in range(2):
            dir_outs[d][4][0, 0, tm // CHUNK:, :] = jnp.ones((SUBLANES - tm // CHUNK, RWKV_DIM), F32)


def _in_proj(x, ln1_g, w_in, mu_cur, mu_prev, mu_next, decay_w0, w2_cat, iclr_a0, a2_pad, gate_g2,
             k_k, k_a, r_k, tri):
    B, T, D = x.shape
    tm = TM_IN
    nt = T // tm
    rows8 = tm // SUBLANES
    const = lambda shape: pl.BlockSpec(shape, lambda b, i: (0,) * len(shape))
    tok = lambda width: pl.BlockSpec((1, tm, width), lambda b, i: (b, i, 0))
    in_specs = [
        tok(D),
        pl.BlockSpec((1, SUBLANES, D), lambda b, i: (b, jnp.maximum(i * rows8 - 1, 0), 0)),
        pl.BlockSpec((1, SUBLANES, D), lambda b, i: (b, jnp.minimum((i + 1) * rows8, T // SUBLANES - 1), 0)),
        const((1, D)),
        pl.BlockSpec((D, PROJ_DIM), lambda b, i: (0, 0), pipeline_mode=pl.Buffered(1)),
        const((1, SHIFT_DIM)), const((1, SHIFT_DIM)), const((1, SHIFT_DIM)),
        const((2, RWKV_DIM)), const((2, 2 * LANES, RWKV_DIM)),
        const((1, RWKV_DIM)), const((LANES, RWKV_DIM)), const((LANES, RWKV_DIM)),
        const((1, RWKV_DIM)), const((1, RWKV_DIM)), const((1, RWKV_DIM)),
        const((2, CHUNK, 2 * CHUNK)),
    ]
    tok_bf = jax.ShapeDtypeStruct((B, T, RWKV_DIM), BF16)
    pl_shape = jax.ShapeDtypeStruct((B, nt, SUBLANES, RWKV_DIM), F32)
    pl_spec = pl.BlockSpec((1, 1, SUBLANES, RWKV_DIM), lambda b, i: (b, i, 0, 0))
    out_shape = [tok_bf] * 9 + [pl_shape, pl_shape, tok_bf, tok_bf,
                                jax.ShapeDtypeStruct((B, T, ATT_DIM), BF16),
                                jax.ShapeDtypeStruct((B, T, KV_DIM), BF16),
                                jax.ShapeDtypeStruct((B, T, KV_DIM), BF16)]
    out_specs = [tok(RWKV_DIM)] * 9 + [pl_spec, pl_spec, tok(RWKV_DIM), tok(RWKV_DIM),
                                       tok(ATT_DIM), tok(KV_DIM), tok(KV_DIM)]
    return pl.pallas_call(
        _in_proj_kernel, grid=(B, nt), in_specs=in_specs, out_specs=out_specs, out_shape=out_shape,
        scratch_shapes=[pltpu.VMEM((D, PROJ_DIM), BF16)],
        compiler_params=pltpu.CompilerParams(dimension_semantics=("arbitrary", "arbitrary"),
                                             vmem_limit_bytes=VMEM_LIMIT),
        name="in_proj",
    )(x, x, x, ln1_g, w_in, mu_cur, mu_prev, mu_next, decay_w0, w2_cat, iclr_a0, a2_pad, gate_g2,
      k_k, k_a, r_k, tri)


def _pair_chunks(items, states, levels, eye, lane0, bd_mask, ys):
    n = range(len(items))
    at, rt, bt, kt, v, p_last, strict, incl = zip(*items)

    def bd(x):
        x = x.astype(BF16)
        zero = jnp.zeros_like(x)
        return jnp.concatenate([jnp.where(lane0, x, zero), jnp.where(lane0, zero, x)], axis=0)

    def pmm(x, y):
        return _dot(x, bd(y))

    sc = [_dot_nt(jnp.concatenate([at[i], rt[i]], axis=0),
                  jnp.concatenate([bd(bt[i]), bd(kt[i])], axis=0)) for i in n]
    a_ab = [jnp.where(strict[i], sc[i][:CHUNK, :LANES], 0.0) for i in n]
    a_ak = [jnp.where(strict[i], sc[i][:CHUNK, LANES:], 0.0) for i in n]
    a_rb = [jnp.where(incl[i], sc[i][CHUNK:, :LANES], 0.0) for i in n]
    a_rk = [jnp.where(incl[i], sc[i][CHUNK:, LANES:], 0.0) for i in n]
    yield

    xy = [pmm(jnp.concatenate([a_ak[i], a_rk[i]], axis=0), v[i]) for i in n]
    x1 = [xy[i][:CHUNK] for i in n]
    yk = [xy[i][CHUNK:] for i in n]
    t_inv = [eye + jnp.where(levels[0], a_ab[i], 0.0) for i in n]
    yield
    for level in levels[1:]:
        e_t = [pmm(jnp.where(level, a_ab[i], 0.0), t_inv[i]) for i in n]
        yield
        t_inv = [t_inv[i] + pmm(t_inv[i], e_t[i]) for i in n]
        yield
    wu = [_dot(t_inv[i], jnp.concatenate([bd(at[i]), bd(x1[i])], axis=1)) for i in n]
    yield
    m = len(states)
    for first in range(0, len(items), m):
        n = range(first, first + m)
        hs = {i: _dot_nt(jnp.concatenate([wu[i][:, :LANES].astype(BF16), rt[i]], axis=0), states[i - first])
              for i in n}
        yield
        u = {i: hs[i][:CHUNK] + wu[i][:, LANES:] for i in n}
        ys.extend(hs[i][CHUNK:] + pmm(a_rb[i], u[i]) + yk[i] for i in n)
        yield
        upd = {i: _dot_tn(jnp.concatenate([u[i].astype(BF16), v[i]], axis=0),
                          jnp.concatenate([bt[i], kt[i]], axis=0)) for i in n}
        states[:] = [(states[i - first] + jnp.where(bd_mask, upd[i], 0.0)) * p_last[i] for i in n]
        yield


def _scan_attn_kernel(at0_ref, rt0_ref, bt0_ref, kt0_ref, v0_ref, pl0_ref,
                      at1_ref, rt1_ref, bt1_ref, kt1_ref, v1_ref, pl1_ref, *rest, n_cast,
                      chunks_per_block):
    n_attn = ATT_QBLOCKS + 8
    attn_in = rest[:n_attn]
    cast_in = rest[n_attn:n_attn + n_cast]
    yf_ref, yb_ref, o_att_ref = rest[n_attn + n_cast:n_attn + n_cast + 3]
    cast_out = rest[n_attn + n_cast + 3:n_attn + 2 * n_cast + 3]
    s_ref = rest[-1]
    c = pl.program_id(0)
    for src_ref, dst_ref in zip(cast_in, cast_out):
        dst_ref[...] = src_ref[...].astype(BF16)

    @pl.when(c == 0)
    def _():
        s_ref[...] = jnp.zeros_like(s_ref)

    n_batch = v0_ref.shape[0]
    ri = lax.broadcasted_iota(jnp.int32, (CHUNK, LANES), 0)
    ci = lax.broadcasted_iota(jnp.int32, (CHUNK, LANES), 1)
    cj = jnp.where(ci >= CHUNK, ci - CHUNK, ci)
    lane0 = ci < CHUNK
    eye = jnp.where(ri == cj, 1.0, 0.0).astype(F32)
    levels = [((ri // (2 * s)) == (cj // (2 * s))) & ((ri // s) != (cj // s))
              for s in (2 ** e for e in range(CHUNK.bit_length() - 1))]
    r2 = lax.broadcasted_iota(jnp.int32, (LANES, LANES), 0)
    c2 = lax.broadcasted_iota(jnp.int32, (LANES, LANES), 1)
    bd_mask = (r2 >= CHUNK) == (c2 >= CHUNK)
    dirs = ((at0_ref, rt0_ref, bt0_ref, kt0_ref, v0_ref, pl0_ref, ri > cj, ri >= cj),
            (at1_ref, rt1_ref, bt1_ref, kt1_ref, v1_ref, pl1_ref, ri < cj, ri <= cj))
    n_pairs = RWKV_DIM // LANES
    n_chunks = pl.num_programs(0) * SCAN_CHUNKS
    order = [(b, d, p) for b in range(n_batch) for d in range(2) for p in range(n_pairs)]
    items = []
    for j in range(SCAN_CHUNKS):
        first = c * SCAN_CHUNKS + j
        chunk = (first, n_chunks - 1 - first)
        sub = (j, SCAN_CHUNKS - 1 - j)
        p_last = {(b, d): dirs[d][5][b, 0, pl.ds(lax.rem(chunk[d], chunks_per_block), 1), :]
                  for b in range(n_batch) for d in range(2)}
        for b, d, p in order:
            at_ref, rt_ref, bt_ref, kt_ref, v_ref, _, strict, incl = dirs[d]
            rows = slice(sub[d] * CHUNK, (sub[d] + 1) * CHUNK)
            sl = slice(p * LANES, (p + 1) * LANES)
            items.append((at_ref[b, rows, sl], rt_ref[b, rows, sl], bt_ref[b, rows, sl],
                          kt_ref[b, rows, sl], v_ref[b, rows, sl], p_last[b, d][:, sl], strict, incl))
    states = [s_ref[b, d, p] for b, d, p in order]
    ys = []
    scan = _pair_chunks(items, states, levels, eye, lane0, bd_mask, ys)
    attn = _attn_stages(*attn_in, o_att_ref)
    while next(scan, StopIteration) is not StopIteration:
        next(attn, None)
    for _ in attn:
        pass
    for j in range(SCAN_CHUNKS):
        sub = (j, SCAN_CHUNKS - 1 - j)
        for b in range(n_batch):
            for d, y_ref in enumerate((yf_ref, yb_ref)):
                base = j * len(order) + (b * 2 + d) * n_pairs
                y_ref[b, sub[d] * CHUNK:(sub[d] + 1) * CHUNK, :] = jnp.concatenate(
                    ys[base:base + n_pairs], axis=1).astype(BF16)
    for (b, d, p), state in zip(order, states):
        s_ref[b, d, p] = state


def _attn_stages(sink_ref, *refs):
    n_sub = ATT_QBLOCKS
    bias_refs = refs[:n_sub]
    q_ref, kp_ref, kc_ref, kn_ref, vp_ref, vc_ref, vn_ref, o_ref = refs[n_sub:]
    blk = WINDOW
    k_all = jnp.concatenate([kp_ref[0], kc_ref[0], kn_ref[0]], axis=0)
    v_all = jnp.concatenate([vp_ref[0], vc_ref[0], vn_ref[0]], axis=0)

    def swap_halves(x):
        return jnp.concatenate([x[:, HEAD_DIM:], x[:, :HEAD_DIM]], axis=1)

    lane0 = lax.broadcasted_iota(jnp.int32, k_all.shape, 1) < HEAD_DIM

    def variants(x):
        xs = swap_halves(x)
        zero = jnp.zeros_like(x)
        return ((jnp.where(lane0, x, zero), jnp.where(lane0, zero, xs)),
                (jnp.where(lane0, xs, zero), jnp.where(lane0, zero, x)))

    k_var = variants(k_all)
    v_var = variants(v_all)
    yield

    group = ATT_HEADS // (KV_DIM // HEAD_DIM)

    items = [(u, h) for u in range(n_sub) for h in range(ATT_HEADS)]
    keys = lambda var, u, h: var[h // group][h % 2][u * blk:(u + 3) * blk]
    q_pairs = [[q_ref[0, u * blk:(u + 1) * blk, j * LANES:(j + 1) * LANES]
                * jnp.asarray(HEAD_DIM ** -0.5, BF16) for j in range(ATT_DIM // LANES)]
               for u in range(n_sub)]
    s = [_dot_nt(q_pairs[u][h // 2], keys(k_var, u, h)) + bias_refs[u][0, h] for u, h in items]
    yield
    m = [jnp.maximum(jnp.max(s[i], axis=-1, keepdims=True), sink_ref[h]) for i, (u, h) in enumerate(items)]
    yield
    p = [jnp.exp(s[i] - m[i]) for i in range(len(items))]
    yield
    den = [jnp.sum(p[i], axis=-1, keepdims=True) + jnp.exp(sink_ref[h] - m[i])
           for i, (u, h) in enumerate(items)]
    yield
    o = [_dot(p[i], keys(v_var, u, h)) * (1.0 / den[i]) for i, (u, h) in enumerate(items)]
    yield
    for u in range(n_sub):
        ou = o[u * ATT_HEADS:(u + 1) * ATT_HEADS]
        o_ref[0, u * blk:(u + 1) * blk, :] = jnp.concatenate(
            [ou[2 * j] + ou[2 * j + 1] for j in range(ATT_DIM // LANES)], axis=1).astype(o_ref.dtype)


def _attn_bias(blk):
    qi = np.arange(blk)[:, None]
    kj = np.arange(3 * blk)[None, :]
    dist = np.abs(kj - blk - qi)
    slopes = 2.0 ** (-8.0 * np.arange(1, ATT_HEADS + 1, dtype=np.float32) / ATT_HEADS)
    alibi = -slopes[:, None, None].astype(np.float32) * dist[None].astype(np.float32)
    out = []
    for has_prev, has_next in ((False, True), (True, True), (True, False)):
        valid = (dist <= WINDOW) & (has_prev | (kj >= blk)) & (has_next | (kj < 2 * blk))
        out.append(np.where(valid[None], alibi, np.float32(MASK_VALUE)))
    return jnp.asarray(np.stack(out), F32)


def _scan_attention(at0, rt0, bt0, kt0, at1, rt1, bt1, kt1, v, pl0, pl1, q, ka, va, sink, weights):
    B, T, C = v.shape
    nc = T // (CHUNK * SCAN_CHUNKS)
    blk = WINDOW
    n_sub = ATT_QBLOCKS
    per_row = T // (blk * n_sub)
    nb = T // blk
    assert B * per_row == nc, "scan and attention must have the same number of grid steps"
    fwd = pl.BlockSpec((B, SCAN_CHUNKS * CHUNK, C), lambda c: (0, c, 0))
    bwd = pl.BlockSpec((B, SCAN_CHUNKS * CHUNK, C), lambda c: (0, nc - 1 - c, 0))
    cpb = TM_IN // CHUNK
    assert cpb % SCAN_CHUNKS == 0, "a step's chunks must share one block of chunk decays"
    spb = cpb // SCAN_CHUNKS
    pl_f = pl.BlockSpec((B, 1, SUBLANES, C), lambda c: (0, c // spb, 0, 0))
    pl_b = pl.BlockSpec((B, 1, SUBLANES, C), lambda c: (0, (nc - 1 - c) // spb, 0, 0))
    y_shape = jax.ShapeDtypeStruct((B, T, C), BF16)

    row = lambda c: c // per_row
    step = lambda c: lax.rem(c, per_row)
    cur = lambda width: pl.BlockSpec((1, n_sub * blk, width), lambda c: (row(c), step(c), 0))
    prev = pl.BlockSpec((1, blk, KV_DIM), lambda c: (row(c), jnp.maximum(step(c) * n_sub - 1, 0), 0))
    nxt = pl.BlockSpec((1, blk, KV_DIM), lambda c: (row(c), jnp.minimum((step(c) + 1) * n_sub, nb - 1), 0))

    def bias_spec(u):
        def index(c):
            g = step(c) * n_sub + u
            return (jnp.where(g == 0, 0, jnp.where(g == nb - 1, 2, 1)), 0, 0, 0)
        return pl.BlockSpec((1, ATT_HEADS, blk, 3 * blk), index)

    def cast_spec(w):
        rows, cols = w.shape
        per_step = next(r for r in range(BF16_ROWS, rows + 1, BF16_ROWS)
                        if rows % r == 0 and rows // r <= nc)
        last = rows // per_step - 1
        return pl.BlockSpec((per_step, cols), lambda c: (jnp.minimum(c, last), 0))

    cast_specs = [cast_spec(w) for w in weights]
    bias = _attn_bias(blk)
    outs = pl.pallas_call(
        functools.partial(_scan_attn_kernel, n_cast=len(weights), chunks_per_block=cpb), grid=(nc,),
        in_specs=[fwd, fwd, fwd, fwd, fwd, pl_f, bwd, bwd, bwd, bwd, bwd, pl_b,
                  pl.BlockSpec(memory_space=pltpu.SMEM)] + [bias_spec(u) for u in range(n_sub)]
                 + [cur(ATT_DIM), prev, cur(KV_DIM), nxt, prev, cur(KV_DIM), nxt] + cast_specs,
        out_specs=[fwd, bwd, cur(ATT_DIM)] + cast_specs,
        out_shape=[y_shape, y_shape, jax.ShapeDtypeStruct((B, T, ATT_DIM), BF16)]
                  + [jax.ShapeDtypeStruct(w.shape, BF16) for w in weights],
        scratch_shapes=[pltpu.VMEM((B, 2, C // LANES, LANES, LANES), F32)],
        compiler_params=pltpu.CompilerParams(dimension_semantics=("arbitrary",),
                                             vmem_limit_bytes=VMEM_LIMIT),
        name="scan_attn",
    )(at0, rt0, bt0, kt0, v, pl0, at1, rt1, bt1, kt1, v, pl1,
      sink, *([bias] * n_sub), q, ka, ka, ka, va, va, va, *weights)
    return outs[0], outs[1], outs[2], outs[3:]


HALO = BF16_ROWS


def _mix_ffn_kernel(*refs):
    (x_m, x_p, x_n, yf_m, yf_p, yf_n, yb_m, yb_p, yb_n, bo_m, bo_p, bo_n, g_m, g_p, g_n,
     oa_m, oa_p, oa_n, lg_ref, lb_ref, wo_ref, ln2_ref, wg_ref, wu_ref, cw_ref, cb_ref,
     wd_ref, lnf_ref, o_ref) = refs
    i = pl.program_id(1)
    n_tiles = pl.num_programs(1)
    tm = x_m.shape[1]
    rows = tm + 2 * HALO
    core = slice(HALO, tm + HALO)
    ext = lambda m, p, n: jnp.concatenate([p[0], m[0], n[0]], axis=0)

    def seg_mean(t):
        return _head_sums(t) * (1.0 / HEAD_DIM)

    y = ext(yf_m, yf_p, yf_n).astype(F32) + ext(yb_m, yb_p, yb_n).astype(F32)
    d = y - seg_mean(y)
    var = seg_mean(d * d)
    yn = d * lax.rsqrt(var + LNX_EPS) * lg_ref[...] + lb_ref[...]
    o_rwkv = (yn + ext(bo_m, bo_p, bo_n)) * ext(g_m, g_p, g_n)
    mix = _dot(o_rwkv, wo_ref[:RWKV_DIM, :]) + _dot(ext(oa_m, oa_p, oa_n), wo_ref[RWKV_DIM:, :])
    x1 = ext(x_m, x_p, x_n) + mix
    r = lax.broadcasted_iota(jnp.int32, (rows, 1), 0)
    inside = ((r >= HALO) | (i > 0)) & ((r < tm + HALO) | (i < n_tiles - 1))
    x1 = jnp.where(inside, x1, 0.0)

    h = _rms_norm(x1, ln2_ref[...]).astype(BF16)
    h_core = h[core]
    acc = jnp.zeros((tm, D_MODEL), F32)
    for c0, c1 in zip(FF_SPLITS[:-1], FF_SPLITS[1:]):
        cs = slice(c0, c1)
        gp_ext = _dot(h, wg_ref[:, cs])
        prev = pltpu.roll(gp_ext, 1, axis=0)[core]
        nxt = pltpu.roll(gp_ext, rows - 1, axis=0)[core]
        gate = (prev * cw_ref[0:1, cs] + gp_ext[core] * cw_ref[1:2, cs] + nxt * cw_ref[2:3, cs]
                + cb_ref[:, cs])
        act = 0.5 * gate * (1.0 + lax.erf(gate * float(1.0 / np.sqrt(2.0))))
        up = _dot(h_core, wu_ref[:, cs])
        acc = acc + _dot(act * up, wd_ref[cs, :])
    o_ref[0] = _rms_norm(x1[core] + acc, lnf_ref[...])


def _mix_ffn(x, yf, yb, bonus, g, o_att, lnx_g, lnx_b, w_out, ln2_g, wg, wu, conv_w, conv_b,
             wd, lnf_g):
    B, T, D = x.shape
    tm = TM_FFN
    per_tile = tm // HALO
    last = T // HALO - 1

    def tok(width):
        return [pl.BlockSpec((1, tm, width), lambda b, i: (b, i, 0)),
                pl.BlockSpec((1, HALO, width), lambda b, i: (b, jnp.maximum(i * per_tile - 1, 0), 0)),
                pl.BlockSpec((1, HALO, width), lambda b, i: (b, jnp.minimum((i + 1) * per_tile, last), 0))]

    const = lambda shape: pl.BlockSpec(shape, lambda b, i: (0,) * len(shape))
    resident = lambda shape: pl.BlockSpec(shape, lambda b, i: (0,) * len(shape),
                                          pipeline_mode=pl.Buffered(1))
    tokens = (x, yf, yb, bonus, g, o_att)
    return pl.pallas_call(
        _mix_ffn_kernel, grid=(B, T // tm),
        in_specs=[spec for a in tokens for spec in tok(a.shape[-1])]
                 + [const((1, RWKV_DIM)), const((1, RWKV_DIM)), const((D, D)),
                    const((1, D)), resident((D, D_FF)), resident((D, D_FF)),
                    const((CONV_WIDTH, D_FF)), const((1, D_FF)), resident((D_FF, D)), const((1, D))],
        out_specs=pl.BlockSpec((1, tm, D), lambda b, i: (b, i, 0)),
        out_shape=jax.ShapeDtypeStruct((B, T, D), F32),
        compiler_params=pltpu.CompilerParams(dimension_semantics=("parallel", "parallel"),
                                             vmem_limit_bytes=VMEM_LIMIT),
        name="mix_ffn",
    )(*[a for a in tokens for _ in range(3)], lnx_g, lnx_b, w_out, ln2_g, wg, wu, conv_w,
      conv_b, wd, lnf_g)


def _chunk_triangles():
    t = np.arange(CHUNK)
    tri = np.stack([t[:, None] >= t[None, :], t[:, None] <= t[None, :]]).astype(np.float32)
    return jnp.asarray(np.concatenate([tri, tri], axis=2), BF16)


def kernel(x, ln1_g, w_in, shift_mu_prev, shift_mu_next, decay_w0, decay_w2, iclr_a0, iclr_a2,
           gate_g2, k_k, k_a, r_k, lnx_g, lnx_b, attn_sink, w_out, ln2_g, ffn_w_gate, ffn_w_up,
           ffn_conv_w, ffn_conv_b, ffn_w_down, lnf_g):
    B, T, _ = x.shape
    assert w_in.shape[0] == 1, "single-layer block"
    l = 0
    tri = _chunk_triangles()
    row = lambda a: a.reshape(1, -1)
    w2 = decay_w2[l]
    w2_pad = jnp.concatenate([w2, jnp.zeros_like(w2)], axis=1)
    w2_bf = w2_pad.astype(BF16)
    w2_cat = jnp.concatenate([w2_bf, w2_bf], axis=1)
    mu_p, mu_n = shift_mu_prev[l], shift_mu_next[l]
    a2_pad = jnp.concatenate([jnp.zeros_like(iclr_a2[l]), iclr_a2[l]], axis=0).astype(BF16)
    (at0, rt0, bt0, kt0, at1, rt1, bt1, kt1, v, pl0, pl1, g, bonus, q, ka, va) = _in_proj(
        x, row(ln1_g[l]), w_in[l], row(1.0 - mu_p - mu_n), row(mu_p), row(mu_n),
        decay_w0[l], w2_cat, row(iclr_a0[l]), a2_pad, gate_g2[l].astype(BF16),
        row(k_k[l]), row(k_a[l]), row(r_k[l]), tri)
    yf, yb, o_att, (wo, wg, wu, wd) = _scan_attention(
        at0, rt0, bt0, kt0, at1, rt1, bt1, kt1, v, pl0, pl1, q, ka, va, attn_sink[l],
        (w_out[l], ffn_w_gate[l], ffn_w_up[l], ffn_w_down[l]))
    return _mix_ffn(x, yf, yb, bonus, g, o_att, row(lnx_g[l]), row(lnx_b[l]), wo,
                    row(ln2_g[l]), wg, wu, ffn_conv_w[l], row(ffn_conv_b[l]), wd, row(lnf_g))
```

```python
import functools

import numpy as np
import jax
import jax.numpy as jnp
from jax import lax
from jax.experimental import pallas as pl
from jax.experimental.pallas import tpu as pltpu

F32 = jnp.float32
BF16 = jnp.bfloat16

D_MODEL = 1024
HEAD_DIM = 64
RWKV_DIM = 512
ATT_DIM = 512
ATT_HEADS = 8
KV_DIM = 128
LORA_DIM = 256
SHIFT_DIM = 3 * RWKV_DIM + LORA_DIM
PROJ_DIM = SHIFT_DIM + ATT_DIM + 2 * KV_DIM
WINDOW = 128
D_FF = 2816
CONV_WIDTH = 3
NORM_EPS = 1e-6
LNX_EPS = 64e-5
L2_EPS = 1e-12
MASK_VALUE = -1e30
LOG2E = float(np.log2(np.e))
NEG_DECAY_SCALE = float(-np.exp(-0.5) * np.log2(np.e))

LANES = 128
SUBLANES = 8
BF16_ROWS = 16
CHUNK = 64
VMEM_LIMIT = 56 * 1024 * 1024

TM_IN = 512
IN_PARTS = 2
TM_FFN = 512
SCAN_CHUNKS = 2
ATT_QBLOCKS = 4
FF_SPLITS = (0, 1536, D_FF)


def _dot(a, b):
    return jnp.dot(a.astype(BF16), b.astype(BF16), preferred_element_type=F32)


def _dot_nt(a, b):
    return lax.dot_general(a.astype(BF16), b.astype(BF16), (((1,), (1,)), ((), ())),
                           preferred_element_type=F32)


def _dot_tn(a, b):
    return lax.dot_general(a.astype(BF16), b.astype(BF16), (((0,), (0,)), ((), ())),
                           preferred_element_type=F32)


def _split2(x):
    hi = x.astype(BF16)
    lo = (x - hi.astype(F32)).astype(BF16)
    return hi, lo


def _rms_norm(x, g):
    return x * lax.rsqrt(jnp.mean(x * x, axis=-1, keepdims=True) + NORM_EPS) * g


def _head_sums(t):
    first_head = lax.broadcasted_iota(jnp.int32, (1, LANES), 1) < HEAD_DIM
    tiles = []
    for j in range(t.shape[-1] // LANES):
        tj = t[:, j * LANES:(j + 1) * LANES]
        both = jnp.sum(tj, axis=-1, keepdims=True)
        head0 = jnp.sum(jnp.where(first_head, tj, 0.0), axis=-1, keepdims=True)
        tiles.append(jnp.where(first_head, head0, both - head0))
    return jnp.concatenate(tiles, axis=1)


def _in_proj_kernel(x_ref, xp_ref, xn_ref, ln1_ref, w_ref, muc_ref, mup_ref, mun_ref, w0_ref,
                    w2_ref, a0_ref, a2_ref, g2_ref, kk_ref, ka_ref, rk_ref, tri_ref,
                    at0_ref, rt0_ref, bt0_ref, kt0_ref, at1_ref, rt1_ref, bt1_ref, kt1_ref,
                    v_ref, pl0_ref, pl1_ref, g_ref, bonus_ref, q_ref, ka_o_ref, va_o_ref, wbf_ref):
    i = pl.program_id(1)
    n_tiles = pl.num_programs(1)
    tm = x_ref.shape[1]

    @pl.when((pl.program_id(0) == 0) & (i == 0))
    def _():
        wbf_ref[...] = w_ref[...].astype(BF16)

    hm = tm // IN_PARTS
    rows = hm + 2 * SUBLANES
    core = slice(SUBLANES, hm + SUBLANES)
    halo_lo = jnp.where(i > 0, xp_ref[0], 0.0)
    halo_hi = jnp.where(i < n_tiles - 1, xn_ref[0], 0.0)
    dir_outs = ((at0_ref, rt0_ref, bt0_ref, kt0_ref, pl0_ref),
                (at1_ref, rt1_ref, bt1_ref, kt1_ref, pl1_ref))

    def part(lo):
        out = slice(lo, lo + hm)
        before = halo_lo if lo == 0 else x_ref[0, lo - SUBLANES:lo, :]
        after = halo_hi if lo + hm == tm else x_ref[0, lo + hm:lo + hm + SUBLANES, :]
        x_ext = jnp.concatenate([before, x_ref[0, out, :], after], axis=0)
        h = _rms_norm(x_ext, ln1_ref[...]).astype(BF16)
        proj = lambda c0, c1: _dot(h, wbf_ref[:, c0:c1])
        p_codes = proj(3 * RWKV_DIM, SHIFT_DIM)
        p_k = proj(RWKV_DIM, 2 * RWKV_DIM)
        p_r = proj(0, RWKV_DIM)
        p_v = proj(2 * RWKV_DIM, 3 * RWKV_DIM)
        yield

        def shifted(p_ext, c0, c1):
            prev = pltpu.roll(p_ext, 1, axis=0)[core]
            nxt = pltpu.roll(p_ext, rows - 1, axis=0)[core]
            return (p_ext[core] * muc_ref[:, c0:c1] + prev * mup_ref[:, c0:c1]
                    + nxt * mun_ref[:, c0:c1])

        codes = shifted(p_codes, 3 * RWKV_DIM, SHIFT_DIM)
        c_di = codes[:, :LANES]
        th_hi, th_lo = _split2(jnp.tanh(c_di))
        th_cat = jnp.concatenate([th_hi, th_lo], axis=1)
        gate_code = jax.nn.sigmoid(codes[:, LANES:])
        k = shifted(p_k, RWKV_DIM, 2 * RWKV_DIM)
        kkr = k * kk_ref[...]
        kkr_sq = kkr * kkr
        r = shifted(p_r, 0, RWKV_DIM)
        v = shifted(p_v, 2 * RWKV_DIM, 3 * RWKV_DIM)
        v_ref[0, out, :] = v.astype(BF16)
        yield

        a_pre = _dot(c_di, a2_ref[...])
        g_ref[0, out, :] = _dot(gate_code, g2_ref[...]).astype(BF16)
        n2 = _head_sums(kkr_sq)
        z = [w0_ref[d:d + 1, :] + jnp.dot(th_cat, w2_ref[d], preferred_element_type=F32)
             for d in range(2)]
        yield

        a_vec = jax.nn.sigmoid(a0_ref[...] + a_pre)
        kk = kkr * lax.rsqrt(jnp.maximum(n2, L2_EPS * L2_EPS))
        k2 = k * (1.0 + (a_vec - 1.0) * ka_ref[...])
        b_vec = kk * a_vec
        neg_kk = -kk
        rk2 = r * k2 * rk_ref[...]
        lw = [NEG_DECAY_SCALE / (1.0 + jnp.exp2(z[d] * (-LOG2E))) for d in range(2)]
        lw_split = [_split2(lw[d]) for d in range(2)]
        yield

        bonus_sum = _head_sums(rk2)
        att = proj(SHIFT_DIM, PROJ_DIM)[core]
        q_ref[0, out, :] = att[:, :ATT_DIM].astype(BF16)
        ka_o_ref[0, out, :] = att[:, ATT_DIM:ATT_DIM + KV_DIM].astype(BF16)
        va_o_ref[0, out, :] = att[:, ATT_DIM + KV_DIM:].astype(BF16)
        parts = [[jnp.dot(tri_ref[d],
                          jnp.concatenate([lw_split[d][0][cs], lw_split[d][1][cs]], axis=0),
                          preferred_element_type=F32)
                  for cs in (slice(j * CHUNK, (j + 1) * CHUNK) for j in range(hm // CHUNK))]
                 for d in range(2)]
        yield

        bonus_ref[0, out, :] = (bonus_sum * v).astype(BF16)
        for d, (at_ref, rt_ref, bt_ref, kt_ref, pl_ref) in enumerate(dir_outs):
            ci = jnp.concatenate(parts[d], axis=0)
            end = 0 if d else CHUNK - 1
            tot = jnp.concatenate([c[end:end + 1] for c in parts[d]], axis=0)
            pl_ref[0, 0, lo // CHUNK:(lo + hm) // CHUNK, :] = jnp.exp2(tot)
            e_inc = jnp.exp2(ci)
            e_exc = jnp.exp2(ci - lw[d])
            e_inv = 1.0 / e_inc
            at_ref[0, out, :] = (neg_kk * e_exc).astype(BF16)
            rt_ref[0, out, :] = (r * e_inc).astype(BF16)
            bt_ref[0, out, :] = (b_vec * e_inv).astype(BF16)
            kt_ref[0, out, :] = (k2 * e_inv).astype(BF16)

    live = [part(p * hm) for p in range(IN_PARTS)]
    while live:
        live = [g for g in live if next(g, StopIteration) is not StopIteration]
    if tm // CHUNK < SUBLANES:
        for d in range(2):
            dir_outs[d][4][0, 0, tm // CHUNK:, :] = jnp.ones((SUBLANES - tm // CHUNK, RWKV_DIM), F32)


def _in_proj(x, ln1_g, w_in, mu_cur, mu_prev, mu_next, decay_w0, w2_cat, iclr_a0, a2_pad, gate_g2,
             k_k, k_a, r_k, tri):
    B, T, D = x.shape
    tm = TM_IN
    nt = T // tm
    rows8 = tm // SUBLANES
    const = lambda shape: pl.BlockSpec(shape, lambda b, i: (0,) * len(shape))
    tok = lambda width: pl.BlockSpec((1, tm, width), lambda b, i: (b, i, 0))
    in_specs = [
        tok(D),
        pl.BlockSpec((1, SUBLANES, D), lambda b, i: (b, jnp.maximum(i * rows8 - 1, 0), 0)),
        pl.BlockSpec((1, SUBLANES, D), lambda b, i: (b, jnp.minimum((i + 1) * rows8, T // SUBLANES - 1), 0)),
        const((1, D)),
        pl.BlockSpec((D, PROJ_DIM), lambda b, i: (0, 0), pipeline_mode=pl.Buffered(1)),
        const((1, SHIFT_DIM)), const((1, SHIFT_DIM)), const((1, SHIFT_DIM)),
        const((2, RWKV_DIM)), const((2, 2 * LANES, RWKV_DIM)),
        const((1, RWKV_DIM)), const((LANES, RWKV_DIM)), const((LANES, RWKV_DIM)),
        const((1, RWKV_DIM)), const((1, RWKV_DIM)), const((1, RWKV_DIM)),
        const((2, CHUNK, 2 * CHUNK)),
    ]
    tok_bf = jax.ShapeDtypeStruct((B, T, RWKV_DIM), BF16)
    pl_shape = jax.ShapeDtypeStruct((B, nt, SUBLANES, RWKV_DIM), F32)
    pl_spec = pl.BlockSpec((1, 1, SUBLANES, RWKV_DIM), lambda b, i: (b, i, 0, 0))
    out_shape = [tok_bf] * 9 + [pl_shape, pl_shape, tok_bf, tok_bf,
                                jax.ShapeDtypeStruct((B, T, ATT_DIM), BF16),
                                jax.ShapeDtypeStruct((B, T, KV_DIM), BF16),
                                jax.ShapeDtypeStruct((B, T, KV_DIM), BF16)]
    out_specs = [tok(RWKV_DIM)] * 9 + [pl_spec, pl_spec, tok(RWKV_DIM), tok(RWKV_DIM),
                                       tok(ATT_DIM), tok(KV_DIM), tok(KV_DIM)]
    return pl.pallas_call(
        _in_proj_kernel, grid=(B, nt), in_specs=in_specs, out_specs=out_specs, out_shape=out_shape,
        scratch_shapes=[pltpu.VMEM((D, PROJ_DIM), BF16)],
        compiler_params=pltpu.CompilerParams(dimension_semantics=("arbitrary", "arbitrary"),
                                             vmem_limit_bytes=VMEM_LIMIT),
        name="in_proj",
    )(x, x, x, ln1_g, w_in, mu_cur, mu_prev, mu_next, decay_w0, w2_cat, iclr_a0, a2_pad, gate_g2,
      k_k, k_a, r_k, tri)


def _pair_chunks(items, states, levels, eye, lane0, bd_mask, ys):
    n = range(len(items))
    at, rt, bt, kt, v, p_last, strict, incl = zip(*items)

    def bd(x):
        x = x.astype(BF16)
        zero = jnp.zeros_like(x)
        return jnp.concatenate([jnp.where(lane0, x, zero), jnp.where(lane0, zero, x)], axis=0)

    def pmm(x, y):
        return _dot(x, bd(y))

    sc = [_dot_nt(jnp.concatenate([at[i], rt[i]], axis=0),
                  jnp.concatenate([bd(bt[i]), bd(kt[i])], axis=0)) for i in n]
    a_ab = [jnp.where(strict[i], sc[i][:CHUNK, :LANES], 0.0) for i in n]
    a_ak = [jnp.where(strict[i], sc[i][:CHUNK, LANES:], 0.0) for i in n]
    a_rb = [jnp.where(incl[i], sc[i][CHUNK:, :LANES], 0.0) for i in n]
    a_rk = [jnp.where(incl[i], sc[i][CHUNK:, LANES:], 0.0) for i in n]
    yield

    xy = [pmm(jnp.concatenate([a_ak[i], a_rk[i]], axis=0), v[i]) for i in n]
    x1 = [xy[i][:CHUNK] for i in n]
    yk = [xy[i][CHUNK:] for i in n]
    t_inv = [eye + jnp.where(levels[0], a_ab[i], 0.0) for i in n]
    yield
    for level in levels[1:]:
        e_t = [pmm(jnp.where(level, a_ab[i], 0.0), t_inv[i]) for i in n]
        yield
        t_inv = [t_inv[i] + pmm(t_inv[i], e_t[i]) for i in n]
        yield
    wu = [_dot(t_inv[i], jnp.concatenate([bd(at[i]), bd(x1[i])], axis=1)) for i in n]
    yield
    m = len(states)
    for first in range(0, len(items), m):
        n = range(first, first + m)
        hs = {i: _dot_nt(jnp.concatenate([wu[i][:, :LANES].astype(BF16), rt[i]], axis=0), states[i - first])
              for i in n}
        yield
        u = {i: hs[i][:CHUNK] + wu[i][:, LANES:] for i in n}
        ys.extend(hs[i][CHUNK:] + pmm(a_rb[i], u[i]) + yk[i] for i in n)
        yield
        upd = {i: _dot_tn(jnp.concatenate([u[i].astype(BF16), v[i]], axis=0),
                          jnp.concatenate([bt[i], kt[i]], axis=0)) for i in n}
        states[:] = [(states[i - first] + jnp.where(bd_mask, upd[i], 0.0)) * p_last[i] for i in n]
        yield


def _scan_attn_kernel(at0_ref, rt0_ref, bt0_ref, kt0_ref, v0_ref, pl0_ref,
                      at1_ref, rt1_ref, bt1_ref, kt1_ref, v1_ref, pl1_ref, *rest, n_cast,
                      chunks_per_block):
    n_attn = ATT_QBLOCKS + 8
    attn_in = rest[:n_attn]
    cast_in = rest[n_attn:n_attn + n_cast]
    yf_ref, yb_ref, o_att_ref = rest[n_attn + n_cast:n_attn + n_cast + 3]
    cast_out = rest[n_attn + n_cast + 3:n_attn + 2 * n_cast + 3]
    s_ref = rest[-1]
    c = pl.program_id(0)
    for src_ref, dst_ref in zip(cast_in, cast_out):
        dst_ref[...] = src_ref[...].astype(BF16)

    @pl.when(c == 0)
    def _():
        s_ref[...] = jnp.zeros_like(s_ref)

    n_batch = v0_ref.shape[0]
    ri = lax.broadcasted_iota(jnp.int32, (CHUNK, LANES), 0)
    ci = lax.broadcasted_iota(jnp.int32, (CHUNK, LANES), 1)
    cj = jnp.where(ci >= CHUNK, ci - CHUNK, ci)
    lane0 = ci < CHUNK
    eye = jnp.where(ri == cj, 1.0, 0.0).astype(F32)
    levels = [((ri // (2 * s)) == (cj // (2 * s))) & ((ri // s) != (cj // s))
              for s in (2 ** e for e in range(CHUNK.bit_length() - 1))]
    r2 = lax.broadcasted_iota(jnp.int32, (LANES, LANES), 0)
    c2 = lax.broadcasted_iota(jnp.int32, (LANES, LANES), 1)
    bd_mask = (r2 >= CHUNK) == (c2 >= CHUNK)
    dirs = ((at0_ref, rt0_ref, bt0_ref, kt0_ref, v0_ref, pl0_ref, ri > cj, ri >= cj),
            (at1_ref, rt1_ref, bt1_ref, kt1_ref, v1_ref, pl1_ref, ri < cj, ri <= cj))
    n_pairs = RWKV_DIM // LANES
    n_chunks = pl.num_programs(0) * SCAN_CHUNKS
    order = [(b, d, p) for b in range(n_batch) for d in range(2) for p in range(n_pairs)]
    items = []
    for j in range(SCAN_CHUNKS):
        first = c * SCAN_CHUNKS + j
        chunk = (first, n_chunks - 1 - first)
        sub = (j, SCAN_CHUNKS - 1 - j)
        p_last = {(b, d): dirs[d][5][b, 0, pl.ds(lax.rem(chunk[d], chunks_per_block), 1), :]
                  for b in range(n_batch) for d in range(2)}
        for b, d, p in order:
            at_ref, rt_ref, bt_ref, kt_ref, v_ref, _, strict, incl = dirs[d]
            rows = slice(sub[d] * CHUNK, (sub[d] + 1) * CHUNK)
            sl = slice(p * LANES, (p + 1) * LANES)
            items.append((at_ref[b, rows, sl], rt_ref[b, rows, sl], bt_ref[b, rows, sl],
                          kt_ref[b, rows, sl], v_ref[b, rows, sl], p_last[b, d][:, sl], strict, incl))
    states = [s_ref[b, d, p] for b, d, p in order]
    ys = []
    scan = _pair_chunks(items, states, levels, eye, lane0, bd_mask, ys)
    attn = _attn_stages(*attn_in, o_att_ref)
    while next(scan, StopIteration) is not StopIteration:
        next(attn, None)
    for _ in attn:
        pass
    for j in range(SCAN_CHUNKS):
        sub = (j, SCAN_CHUNKS - 1 - j)
        for b in range(n_batch):
            for d, y_ref in enumerate((yf_ref, yb_ref)):
                base = j * len(order) + (b * 2 + d) * n_pairs
                y_ref[b, sub[d] * CHUNK:(sub[d] + 1) * CHUNK, :] = jnp.concatenate(
                    ys[base:base + n_pairs], axis=1).astype(BF16)
    for (b, d, p), state in zip(order, states):
        s_ref[b, d, p] = state


def _attn_stages(sink_ref, *refs):
    n_sub = ATT_QBLOCKS
    bias_refs = refs[:n_sub]
    q_ref, kp_ref, kc_ref, kn_ref, vp_ref, vc_ref, vn_ref, o_ref = refs[n_sub:]
    blk = WINDOW
    k_all = jnp.concatenate([kp_ref[0], kc_ref[0], kn_ref[0]], axis=0)
    v_all = jnp.concatenate([vp_ref[0], vc_ref[0], vn_ref[0]], axis=0)

    def swap_halves(x):
        return jnp.concatenate([x[:, HEAD_DIM:], x[:, :HEAD_DIM]], axis=1)

    lane0 = lax.broadcasted_iota(jnp.int32, k_all.shape, 1) < HEAD_DIM

    def variants(x):
        xs = swap_halves(x)
        zero = jnp.zeros_like(x)
        return ((jnp.where(lane0, x, zero), jnp.where(lane0, zero, xs)),
                (jnp.where(lane0, xs, zero), jnp.where(lane0, zero, x)))

    k_var = variants(k_all)
    v_var = variants(v_all)
    yield

    group = ATT_HEADS // (KV_DIM // HEAD_DIM)

    items = [(u, h) for u in range(n_sub) for h in range(ATT_HEADS)]
    keys = lambda var, u, h: var[h // group][h % 2][u * blk:(u + 3) * blk]
    q_pairs = [[q_ref[0, u * blk:(u + 1) * blk, j * LANES:(j + 1) * LANES]
                * jnp.asarray(HEAD_DIM ** -0.5, BF16) for j in range(ATT_DIM // LANES)]
               for u in range(n_sub)]
    s = [_dot_nt(q_pairs[u][h // 2], keys(k_var, u, h)) + bias_refs[u][0, h] for u, h in items]
    yield
    m = [jnp.maximum(jnp.max(s[i], axis=-1, keepdims=True), sink_ref[h]) for i, (u, h) in enumerate(items)]
    yield
    p = [jnp.exp(s[i] - m[i]) for i in range(len(items))]
    yield
    den = [jnp.sum(p[i], axis=-1, keepdims=True) + jnp.exp(sink_ref[h] - m[i])
           for i, (u, h) in enumerate(items)]
    yield
    o = [_dot(p[i], keys(v_var, u, h)) * (1.0 / den[i]) for i, (u, h) in enumerate(items)]
    yield
    for u in range(n_sub):
        ou = o[u * ATT_HEADS:(u + 1) * ATT_HEADS]
        o_ref[0, u * blk:(u + 1) * blk, :] = jnp.concatenate(
            [ou[2 * j] + ou[2 * j + 1] for j in range(ATT_DIM // LANES)], axis=1).astype(o_ref.dtype)


def _attn_bias(blk):
    qi = np.arange(blk)[:, None]
    kj = np.arange(3 * blk)[None, :]
    dist = np.abs(kj - blk - qi)
    slopes = 2.0 ** (-8.0 * np.arange(1, ATT_HEADS + 1, dtype=np.float32) / ATT_HEADS)
    alibi = -slopes[:, None, None].astype(np.float32) * dist[None].astype(np.float32)
    out = []
    for has_prev, has_next in ((False, True), (True, True), (True, False)):
        valid = (dist <= WINDOW) & (has_prev | (kj >= blk)) & (has_next | (kj < 2 * blk))
        out.append(np.where(valid[None], alibi, np.float32(MASK_VALUE)))
    return jnp.asarray(np.stack(out), F32)


def _scan_attention(at0, rt0, bt0, kt0, at1, rt1, bt1, kt1, v, pl0, pl1, q, ka, va, sink, weights):
    B, T, C = v.shape
    nc = T // (CHUNK * SCAN_CHUNKS)
    blk = WINDOW
    n_sub = ATT_QBLOCKS
    per_row = T // (blk * n_sub)
    nb = T // blk
    assert B * per_row == nc, "scan and attention must have the same number of grid steps"
    fwd = pl.BlockSpec((B, SCAN_CHUNKS * CHUNK, C), lambda c: (0, c, 0))
    bwd = pl.BlockSpec((B, SCAN_CHUNKS * CHUNK, C), lambda c: (0, nc - 1 - c, 0))
    cpb = TM_IN // CHUNK
    assert cpb % SCAN_CHUNKS == 0, "a step's chunks must share one block of chunk decays"
    spb = cpb // SCAN_CHUNKS
    pl_f = pl.BlockSpec((B, 1, SUBLANES, C), lambda c: (0, c // spb, 0, 0))
    pl_b = pl.BlockSpec((B, 1, SUBLANES, C), lambda c: (0, (nc - 1 - c) // spb, 0, 0))
    y_shape = jax.ShapeDtypeStruct((B, T, C), BF16)

    row = lambda c: c // per_row
    step = lambda c: lax.rem(c, per_row)
    cur = lambda width: pl.BlockSpec((1, n_sub * blk, width), lambda c: (row(c), step(c), 0))
    prev = pl.BlockSpec((1, blk, KV_DIM), lambda c: (row(c), jnp.maximum(step(c) * n_sub - 1, 0), 0))
    nxt = pl.BlockSpec((1, blk, KV_DIM), lambda c: (row(c), jnp.minimum((step(c) + 1) * n_sub, nb - 1), 0))

    def bias_spec(u):
        def index(c):
            g = step(c) * n_sub + u
            return (jnp.where(g == 0, 0, jnp.where(g == nb - 1, 2, 1)), 0, 0, 0)
        return pl.BlockSpec((1, ATT_HEADS, blk, 3 * blk), index)

    def cast_spec(w):
        rows, cols = w.shape
        per_step = next(r for r in range(BF16_ROWS, rows + 1, BF16_ROWS)
                        if rows % r == 0 and rows // r <= nc)
        last = rows // per_step - 1
        return pl.BlockSpec((per_step, cols), lambda c: (jnp.minimum(c, last), 0))

    cast_specs = [cast_spec(w) for w in weights]
    bias = _attn_bias(blk)
    outs = pl.pallas_call(
        functools.partial(_scan_attn_kernel, n_cast=len(weights), chunks_per_block=cpb), grid=(nc,),
        in_specs=[fwd, fwd, fwd, fwd, fwd, pl_f, bwd, bwd, bwd, bwd, bwd, pl_b,
                  pl.BlockSpec(memory_space=pltpu.SMEM)] + [bias_spec(u) for u in range(n_sub)]
                 + [cur(ATT_DIM), prev, cur(KV_DIM), nxt, prev, cur(KV_DIM), nxt] + cast_specs,
        out_specs=[fwd, bwd, cur(ATT_DIM)] + cast_specs,
        out_shape=[y_shape, y_shape, jax.ShapeDtypeStruct((B, T, ATT_DIM), BF16)]
                  + [jax.ShapeDtypeStruct(w.shape, BF16) for w in weights],
        scratch_shapes=[pltpu.VMEM((B, 2, C // LANES, LANES, LANES), F32)],
        compiler_params=pltpu.CompilerParams(dimension_semantics=("arbitrary",),
                                             vmem_limit_bytes=VMEM_LIMIT),
        name="scan_attn",
    )(at0, rt0, bt0, kt0, v, pl0, at1, rt1, bt1, kt1, v, pl1,
      sink, *([bias] * n_sub), q, ka, ka, ka, va, va, va, *weights)
    return outs[0], outs[1], outs[2], outs[3:]


HALO = BF16_ROWS


def _mix_ffn_kernel(*refs):
    (x_m, x_p, x_n, yf_m, yf_p, yf_n, yb_m, yb_p, yb_n, bo_m, bo_p, bo_n, g_m, g_p, g_n,
     oa_m, oa_p, oa_n, lg_ref, lb_ref, wo_ref, ln2_ref, wg_ref, wu_ref, cw_ref, cb_ref,
     wd_ref, lnf_ref, o_ref) = refs
    i = pl.program_id(1)
    n_tiles = pl.num_programs(1)
    tm = x_m.shape[1]
    rows = tm + 2 * HALO
    core = slice(HALO, tm + HALO)
    ext = lambda m, p, n: jnp.concatenate([p[0], m[0], n[0]], axis=0)

    def seg_mean(t):
        return _head_sums(t) * (1.0 / HEAD_DIM)

    y = ext(yf_m, yf_p, yf_n).astype(F32) + ext(yb_m, yb_p, yb_n).astype(F32)
    d = y - seg_mean(y)
    var = seg_mean(d * d)
    yn = d * lax.rsqrt(var + LNX_EPS) * lg_ref[...] + lb_ref[...]
    o_rwkv = (yn + ext(bo_m, bo_p, bo_n)) * ext(g_m, g_p, g_n)
    mix = _dot(o_rwkv, wo_ref[:RWKV_DIM, :]) + _dot(ext(oa_m, oa_p, oa_n), wo_ref[RWKV_DIM:, :])
    x1 = ext(x_m, x_p, x_n) + mix
    r = lax.broadcasted_iota(jnp.int32, (rows, 1), 0)
    inside = ((r >= HALO) | (i > 0)) & ((r < tm + HALO) | (i < n_tiles - 1))
    x1 = jnp.where(inside, x1, 0.0)

    h = _rms_norm(x1, ln2_ref[...]).astype(BF16)
    h_core = h[core]
    acc = jnp.zeros((tm, D_MODEL), F32)
    for c0, c1 in zip(FF_SPLITS[:-1], FF_SPLITS[1:]):
        cs = slice(c0, c1)
        gp_ext = _dot(h, wg_ref[:, cs])
        prev = pltpu.roll(gp_ext, 1, axis=0)[core]
        nxt = pltpu.roll(gp_ext, rows - 1, axis=0)[core]
        gate = (prev * cw_ref[0:1, cs] + gp_ext[core] * cw_ref[1:2, cs] + nxt * cw_ref[2:3, cs]
                + cb_ref[:, cs])
        act = 0.5 * gate * (1.0 + lax.erf(gate * float(1.0 / np.sqrt(2.0))))
        up = _dot(h_core, wu_ref[:, cs])
        acc = acc + _dot(act * up, wd_ref[cs, :])
    o_ref[0] = _rms_norm(x1[core] + acc, lnf_ref[...])


def _mix_ffn(x, yf, yb, bonus, g, o_att, lnx_g, lnx_b, w_out, ln2_g, wg, wu, conv_w, conv_b,
             wd, lnf_g):
    B, T, D = x.shape
    tm = TM_FFN
    per_tile = tm // HALO
    last = T // HALO - 1

    def tok(width):
        return [pl.BlockSpec((1, tm, width), lambda b, i: (b, i, 0)),
                pl.BlockSpec((1, HALO, width), lambda b, i: (b, jnp.maximum(i * per_tile - 1, 0), 0)),
                pl.BlockSpec((1, HALO, width), lambda b, i: (b, jnp.minimum((i + 1) * per_tile, last), 0))]

    const = lambda shape: pl.BlockSpec(shape, lambda b, i: (0,) * len(shape))
    resident = lambda shape: pl.BlockSpec(shape, lambda b, i: (0,) * len(shape),
                                          pipeline_mode=pl.Buffered(1))
    tokens = (x, yf, yb, bonus, g, o_att)
    return pl.pallas_call(
        _mix_ffn_kernel, grid=(B, T // tm),
        in_specs=[spec for a in tokens for spec in tok(a.shape[-1])]
                 + [const((1, RWKV_DIM)), const((1, RWKV_DIM)), const((D, D)),
                    const((1, D)), resident((D, D_FF)), resident((D, D_FF)),
                    const((CONV_WIDTH, D_FF)), const((1, D_FF)), resident((D_FF, D)), const((1, D))],
        out_specs=pl.BlockSpec((1, tm, D), lambda b, i: (b, i, 0)),
        out_shape=jax.ShapeDtypeStruct((B, T, D), F32),
        compiler_params=pltpu.CompilerParams(dimension_semantics=("parallel", "parallel"),
                                             vmem_limit_bytes=VMEM_LIMIT),
        name="mix_ffn",
    )(*[a for a in tokens for _ in range(3)], lnx_g, lnx_b, w_out, ln2_g, wg, wu, conv_w,
      conv_b, wd, lnf_g)


def _chunk_triangles():
    t = np.arange(CHUNK)
    tri = np.stack([t[:, None] >= t[None, :], t[:, None] <= t[None, :]]).astype(np.float32)
    return jnp.asarray(np.concatenate([tri, tri], axis=2), BF16)


def kernel(x, ln1_g, w_in, shift_mu_prev, shift_mu_next, decay_w0, decay_w2, iclr_a0, iclr_a2,
           gate_g2, k_k, k_a, r_k, lnx_g, lnx_b, attn_sink, w_out, ln2_g, ffn_w_gate, ffn_w_up,
           ffn_conv_w, ffn_conv_b, ffn_w_down, lnf_g):
    B, T, _ = x.shape
    assert w_in.shape[0] == 1, "single-layer block"
    l = 0
    tri = _chunk_triangles()
    row = lambda a: a.reshape(1, -1)
    w2 = decay_w2[l]
    w2_pad = jnp.concatenate([w2, jnp.zeros_like(w2)], axis=1)
    w2_bf = w2_pad.astype(BF16)
    w2_cat = jnp.concatenate([w2_bf, w2_bf], axis=1)
    mu_p, mu_n = shift_mu_prev[l], shift_mu_next[l]
    a2_pad = jnp.concatenate([jnp.zeros_like(iclr_a2[l]), iclr_a2[l]], axis=0).astype(BF16)
    (at0, rt0, bt0, kt0, at1, rt1, bt1, kt1, v, pl0, pl1, g, bonus, q, ka, va) = _in_proj(
        x, row(ln1_g[l]), w_in[l], row(1.0 - mu_p - mu_n), row(mu_p), row(mu_n),
        decay_w0[l], w2_cat, row(iclr_a0[l]), a2_pad, gate_g2[l].astype(BF16),
        row(k_k[l]), row(k_a[l]), row(r_k[l]), tri)
    yf, yb, o_att, (wo, wg, wu, wd) = _scan_attention(
        at0, rt0, bt0, kt0, at1, rt1, bt1, kt1, v, pl0, pl1, q, ka, va, attn_sink[l],
        (w_out[l], ffn_w_gate[l], ffn_w_up[l], ffn_w_down[l]))
    return _mix_ffn(x, yf, yb, bonus, g, o_att, row(lnx_g[l]), row(lnx_b[l]), wo,
                    row(ln2_g[l]), wg, wu, ffn_conv_w[l], row(ffn_conv_b[l]), wd, row(lnf_g))
```

```python
import functools

import numpy as np
import jax
import jax.numpy as jnp
from jax import lax
from jax.experimental import pallas as pl
from jax.experimental.pallas import tpu as pltpu

F32 = jnp.float32
BF16 = jnp.bfloat16

D_MODEL = 1024
HEAD_DIM = 64
RWKV_DIM = 512
ATT_DIM = 512
ATT_HEADS = 8
KV_DIM = 128
LORA_DIM = 256
SHIFT_DIM = 3 * RWKV_DIM + LORA_DIM
PROJ_DIM = SHIFT_DIM + ATT_DIM + 2 * KV_DIM
WINDOW = 128
D_FF = 2816
CONV_WIDTH = 3
NORM_EPS = 1e-6
LNX_EPS = 64e-5
L2_EPS = 1e-12
MASK_VALUE = -1e30
LOG2E = float(np.log2(np.e))
NEG_DECAY_SCALE = float(-np.exp(-0.5) * np.log2(np.e))

LANES = 128
SUBLANES = 8
BF16_ROWS = 16
CHUNK = 64
VMEM_LIMIT = 56 * 1024 * 1024

TM_IN = 512
IN_PARTS = 2
TM_FFN = 512
SCAN_CHUNKS = 2
ATT_QBLOCKS = 4
FF_SPLITS = (0, 1536, D_FF)


def _dot(a, b):
    return jnp.dot(a.astype(BF16), b.astype(BF16), preferred_element_type=F32)


def _dot_nt(a, b):
    return lax.dot_general(a.astype(BF16), b.astype(BF16), (((1,), (1,)), ((), ())),
                           preferred_element_type=F32)


def _dot_tn(a, b):
    return lax.dot_general(a.astype(BF16), b.astype(BF16), (((0,), (0,)), ((), ())),
                           preferred_element_type=F32)


def _split2(x):
    hi = x.astype(BF16)
    lo = (x - hi.astype(F32)).astype(BF16)
    return hi, lo


def _rms_norm(x, g):
    return x * lax.rsqrt(jnp.mean(x * x, axis=-1, keepdims=True) + NORM_EPS) * g


def _head_sums(t):
    first_head = lax.broadcasted_iota(jnp.int32, (1, LANES), 1) < HEAD_DIM
    tiles = []
    for j in range(t.shape[-1] // LANES):
        tj = t[:, j * LANES:(j + 1) * LANES]
        both = jnp.sum(tj, axis=-1, keepdims=True)
        head0 = jnp.sum(jnp.where(first_head, tj, 0.0), axis=-1, keepdims=True)
        tiles.append(jnp.where(first_head, head0, both - head0))
    return jnp.concatenate(tiles, axis=1)


def _in_proj_kernel(x_ref, xp_ref, xn_ref, ln1_ref, w_ref, muc_ref, mup_ref, mun_ref, w0_ref,
                    w2_ref, a0_ref, a2_ref, g2_ref, kk_ref, ka_ref, rk_ref, tri_ref,
                    at0_ref, rt0_ref, bt0_ref, kt0_ref, at1_ref, rt1_ref, bt1_ref, kt1_ref,
                    v_ref, pl0_ref, pl1_ref, g_ref, bonus_ref, q_ref, ka_o_ref, va_o_ref, wbf_ref):
    i = pl.program_id(1)
    n_tiles = pl.num_programs(1)
    tm = x_ref.shape[1]

    @pl.when((pl.program_id(0) == 0) & (i == 0))
    def _():
        wbf_ref[...] = w_ref[...].astype(BF16)

    hm = tm // IN_PARTS
    rows = hm + 2 * SUBLANES
    core = slice(SUBLANES, hm + SUBLANES)
    halo_lo = jnp.where(i > 0, xp_ref[0], 0.0)
    halo_hi = jnp.where(i < n_tiles - 1, xn_ref[0], 0.0)
    dir_outs = ((at0_ref, rt0_ref, bt0_ref, kt0_ref, pl0_ref),
                (at1_ref, rt1_ref, bt1_ref, kt1_ref, pl1_ref))

    def part(lo):
        out = slice(lo, lo + hm)
        before = halo_lo if lo == 0 else x_ref[0, lo - SUBLANES:lo, :]
        after = halo_hi if lo + hm == tm else x_ref[0, lo + hm:lo + hm + SUBLANES, :]
        x_ext = jnp.concatenate([before, x_ref[0, out, :], after], axis=0)
        h = _rms_norm(x_ext, ln1_ref[...]).astype(BF16)
        proj = lambda c0, c1: _dot(h, wbf_ref[:, c0:c1])
        p_codes = proj(3 * RWKV_DIM, SHIFT_DIM)
        p_k = proj(RWKV_DIM, 2 * RWKV_DIM)
        p_r = proj(0, RWKV_DIM)
        p_v = proj(2 * RWKV_DIM, 3 * RWKV_DIM)
        yield

        def shifted(p_ext, c0, c1):
            prev = pltpu.roll(p_ext, 1, axis=0)[core]
            nxt = pltpu.roll(p_ext, rows - 1, axis=0)[core]
            return (p_ext[core] * muc_ref[:, c0:c1] + prev * mup_ref[:, c0:c1]
                    + nxt * mun_ref[:, c0:c1])

        codes = shifted(p_codes, 3 * RWKV_DIM, SHIFT_DIM)
        c_di = codes[:, :LANES]
        th_hi, th_lo = _split2(jnp.tanh(c_di))
        th_cat = jnp.concatenate([th_hi, th_lo], axis=1)
        gate_code = jax.nn.sigmoid(codes[:, LANES:])
        k = shifted(p_k, RWKV_DIM, 2 * RWKV_DIM)
        kkr = k * kk_ref[...]
        kkr_sq = kkr * kkr
        r = shifted(p_r, 0, RWKV_DIM)
        v = shifted(p_v, 2 * RWKV_DIM, 3 * RWKV_DIM)
        v_ref[0, out, :] = v.astype(BF16)
        yield

        a_pre = _dot(c_di, a2_ref[...])
        g_ref[0, out, :] = _dot(gate_code, g2_ref[...]).astype(BF16)
        n2 = _head_sums(kkr_sq)
        z = [w0_ref[d:d + 1, :] + jnp.dot(th_cat, w2_ref[d], preferred_element_type=F32)
             for d in range(2)]
        yield

        a_vec = jax.nn.sigmoid(a0_ref[...] + a_pre)
        kk = kkr * lax.rsqrt(jnp.maximum(n2, L2_EPS * L2_EPS))
        k2 = k * (1.0 + (a_vec - 1.0) * ka_ref[...])
        b_vec = kk * a_vec
        neg_kk = -kk
        rk2 = r * k2 * rk_ref[...]
        lw = [NEG_DECAY_SCALE / (1.0 + jnp.exp2(z[d] * (-LOG2E))) for d in range(2)]
        lw_split = [_split2(lw[d]) for d in range(2)]
        yield

        bonus_sum = _head_sums(rk2)
        att = proj(SHIFT_DIM, PROJ_DIM)[core]
        q_ref[0, out, :] = att[:, :ATT_DIM].astype(BF16)
        ka_o_ref[0, out, :] = att[:, ATT_DIM:ATT_DIM + KV_DIM].astype(BF16)
        va_o_ref[0, out, :] = att[:, ATT_DIM + KV_DIM:].astype(BF16)
        parts = [[jnp.dot(tri_ref[d],
                          jnp.concatenate([lw_split[d][0][cs], lw_split[d][1][cs]], axis=0),
                          preferred_element_type=F32)
                  for cs in (slice(j * CHUNK, (j + 1) * CHUNK) for j in range(hm // CHUNK))]
                 for d in range(2)]
        yield

        bonus_ref[0, out, :] = (bonus_sum * v).astype(BF16)
        for d, (at_ref, rt_ref, bt_ref, kt_ref, pl_ref) in enumerate(dir_outs):
            ci = jnp.concatenate(parts[d], axis=0)
            end = 0 if d else CHUNK - 1
            tot = jnp.concatenate([c[end:end + 1] for c in parts[d]], axis=0)
            pl_ref[0, 0, lo // CHUNK:(lo + hm) // CHUNK, :] = jnp.exp2(tot)
            e_inc = jnp.exp2(ci)
            e_exc = jnp.exp2(ci - lw[d])
            e_inv = 1.0 / e_inc
            at_ref[0, out, :] = (neg_kk * e_exc).astype(BF16)
            rt_ref[0, out, :] = (r * e_inc).astype(BF16)
            bt_ref[0, out, :] = (b_vec * e_inv).astype(BF16)
            kt_ref[0, out, :] = (k2 * e_inv).astype(BF16)

    live = [part(p * hm) for p in range(IN_PARTS)]
    while live:
        live = [g for g in live if next(g, StopIteration) is not StopIteration]
    if tm // CHUNK < SUBLANES:
        for d in range(2):
            dir_outs[d][4][0, 0, tm // CHUNK:, :] = jnp.ones((SUBLANES - tm // CHUNK, RWKV_DIM), F32)


def _in_proj(x, ln1_g, w_in, mu_cur, mu_prev, mu_next, decay_w0, w2_cat, iclr_a0, a2_pad, gate_g2,
             k_k, k_a, r_k, tri):
    B, T, D = x.shape
    tm = TM_IN
    nt = T // tm
    rows8 = tm // SUBLANES
    const = lambda shape: pl.BlockSpec(shape, lambda b, i: (0,) * len(shape))
    tok = lambda width: pl.BlockSpec((1, tm, width), lambda b, i: (b, i, 0))
    in_specs = [
        tok(D),
        pl.BlockSpec((1, SUBLANES, D), lambda b, i: (b, jnp.maximum(i * rows8 - 1, 0), 0)),
        pl.BlockSpec((1, SUBLANES, D), lambda b, i: (b, jnp.minimum((i + 1) * rows8, T // SUBLANES - 1), 0)),
        const((1, D)),
        pl.BlockSpec((D, PROJ_DIM), lambda b, i: (0, 0), pipeline_mode=pl.Buffered(1)),
        const((1, SHIFT_DIM)), const((1, SHIFT_DIM)), const((1, SHIFT_DIM)),
        const((2, RWKV_DIM)), const((2, 2 * LANES, RWKV_DIM)),
        const((1, RWKV_DIM)), const((LANES, RWKV_DIM)), const((LANES, RWKV_DIM)),
        const((1, RWKV_DIM)), const((1, RWKV_DIM)), const((1, RWKV_DIM)),
        const((2, CHUNK, 2 * CHUNK)),
    ]
    tok_bf = jax.ShapeDtypeStruct((B, T, RWKV_DIM), BF16)
    pl_shape = jax.ShapeDtypeStruct((B, nt, SUBLANES, RWKV_DIM), F32)
    pl_spec = pl.BlockSpec((1, 1, SUBLANES, RWKV_DIM), lambda b, i: (b, i, 0, 0))
    out_shape = [tok_bf] * 9 + [pl_shape, pl_shape, tok_bf, tok_bf,
                                jax.ShapeDtypeStruct((B, T, ATT_DIM), BF16),
                                jax.ShapeDtypeStruct((B, T, KV_DIM), BF16),
                                jax.ShapeDtypeStruct((B, T, KV_DIM), BF16)]
    out_specs = [tok(RWKV_DIM)] * 9 + [pl_spec, pl_spec, tok(RWKV_DIM), tok(RWKV_DIM),
                                       tok(ATT_DIM), tok(KV_DIM), tok(KV_DIM)]
    return pl.pallas_call(
        _in_proj_kernel, grid=(B, nt), in_specs=in_specs, out_specs=out_specs, out_shape=out_shape,
        scratch_shapes=[pltpu.VMEM((D, PROJ_DIM), BF16)],
        compiler_params=pltpu.CompilerParams(dimension_semantics=("arbitrary", "arbitrary"),
                                             vmem_limit_bytes=VMEM_LIMIT),
        name="in_proj",
    )(x, x, x, ln1_g, w_in, mu_cur, mu_prev, mu_next, decay_w0, w2_cat, iclr_a0, a2_pad, gate_g2,
      k_k, k_a, r_k, tri)


def _pair_chunks(items, states, levels, eye, lane0, bd_mask, ys):
    n = range(len(items))
    at, rt, bt, kt, v, p_last, strict, incl = zip(*items)

    def bd(x):
        x = x.astype(BF16)
        zero = jnp.zeros_like(x)
        return jnp.concatenate([jnp.where(lane0, x, zero), jnp.where(lane0, zero, x)], axis=0)

    def pmm(x, y):
        return _dot(x, bd(y))

    sc = [_dot_nt(jnp.concatenate([at[i], rt[i]], axis=0),
                  jnp.concatenate([bd(bt[i]), bd(kt[i])], axis=0)) for i in n]
    a_ab = [jnp.where(strict[i], sc[i][:CHUNK, :LANES], 0.0) for i in n]
    a_ak = [jnp.where(strict[i], sc[i][:CHUNK, LANES:], 0.0) for i in n]
    a_rb = [jnp.where(incl[i], sc[i][CHUNK:, :LANES], 0.0) for i in n]
    a_rk = [jnp.where(incl[i], sc[i][CHUNK:, LANES:], 0.0) for i in n]
    yield

    xy = [pmm(jnp.concatenate([a_ak[i], a_rk[i]], axis=0), v[i]) for i in n]
    x1 = [xy[i][:CHUNK] for i in n]
    yk = [xy[i][CHUNK:] for i in n]
    t_inv = [eye + jnp.where(levels[0], a_ab[i], 0.0) for i in n]
    yield
    def pmm_two(x, y, i):
        both = _dot(jnp.concatenate([x[i], x[i + 1]], axis=0),
                    jnp.concatenate([bd(y[i]), bd(y[i + 1])], axis=1))
        return both[:CHUNK, :LANES], both[CHUNK:, LANES:]

    def pmm_all(x, y):
        return [r for i in range(0, len(x), 2) for r in pmm_two(x, y, i)]

    for level in levels[1:]:
        e_t = pmm_all([jnp.where(level, a_ab[i], 0.0) for i in n], t_inv)
        yield
        step = pmm_all(t_inv, e_t)
        t_inv = [t_inv[i] + step[i] for i in n]
        yield
    wu = [_dot(t_inv[i], jnp.concatenate([bd(at[i]), bd(x1[i])], axis=1)) for i in n]
    yield
    m = len(states)
    for first in range(0, len(items), m):
        n = range(first, first + m)
        hs = {i: _dot_nt(jnp.concatenate([wu[i][:, :LANES].astype(BF16), rt[i]], axis=0), states[i - first])
              for i in n}
        yield
        u = {i: hs[i][:CHUNK] + wu[i][:, LANES:] for i in n}
        ys.extend(hs[i][CHUNK:] + pmm(a_rb[i], u[i]) + yk[i] for i in n)
        yield
        upd = {i: _dot_tn(jnp.concatenate([u[i].astype(BF16), v[i]], axis=0),
                          jnp.concatenate([bt[i], kt[i]], axis=0)) for i in n}
        states[:] = [(states[i - first] + jnp.where(bd_mask, upd[i], 0.0)) * p_last[i] for i in n]
        yield


def _scan_attn_kernel(at0_ref, rt0_ref, bt0_ref, kt0_ref, v0_ref, pl0_ref,
                      at1_ref, rt1_ref, bt1_ref, kt1_ref, v1_ref, pl1_ref, *rest, n_cast,
                      chunks_per_block):
    n_attn = ATT_QBLOCKS + 8
    attn_in = rest[:n_attn]
    cast_in = rest[n_attn:n_attn + n_cast]
    yf_ref, yb_ref, o_att_ref = rest[n_attn + n_cast:n_attn + n_cast + 3]
    cast_out = rest[n_attn + n_cast + 3:n_attn + 2 * n_cast + 3]
    s_ref = rest[-1]
    c = pl.program_id(0)
    for src_ref, dst_ref in zip(cast_in, cast_out):
        dst_ref[...] = src_ref[...].astype(BF16)

    @pl.when(c == 0)
    def _():
        s_ref[...] = jnp.zeros_like(s_ref)

    n_batch = v0_ref.shape[0]
    ri = lax.broadcasted_iota(jnp.int32, (CHUNK, LANES), 0)
    ci = lax.broadcasted_iota(jnp.int32, (CHUNK, LANES), 1)
    cj = jnp.where(ci >= CHUNK, ci - CHUNK, ci)
    lane0 = ci < CHUNK
    eye = jnp.where(ri == cj, 1.0, 0.0).astype(F32)
    levels = [((ri // (2 * s)) == (cj // (2 * s))) & ((ri // s) != (cj // s))
              for s in (2 ** e for e in range(CHUNK.bit_length() - 1))]
    r2 = lax.broadcasted_iota(jnp.int32, (LANES, LANES), 0)
    c2 = lax.broadcasted_iota(jnp.int32, (LANES, LANES), 1)
    bd_mask = (r2 >= CHUNK) == (c2 >= CHUNK)
    dirs = ((at0_ref, rt0_ref, bt0_ref, kt0_ref, v0_ref, pl0_ref, ri > cj, ri >= cj),
            (at1_ref, rt1_ref, bt1_ref, kt1_ref, v1_ref, pl1_ref, ri < cj, ri <= cj))
    n_pairs = RWKV_DIM // LANES
    n_chunks = pl.num_programs(0) * SCAN_CHUNKS
    order = [(b, d, p) for b in range(n_batch) for d in range(2) for p in range(n_pairs)]
    items = []
    for j in range(SCAN_CHUNKS):
        first = c * SCAN_CHUNKS + j
        chunk = (first, n_chunks - 1 - first)
        sub = (j, SCAN_CHUNKS - 1 - j)
        p_last = {(b, d): dirs[d][5][b, 0, pl.ds(lax.rem(chunk[d], chunks_per_block), 1), :]
                  for b in range(n_batch) for d in range(2)}
        for b, d, p in order:
            at_ref, rt_ref, bt_ref, kt_ref, v_ref, _, strict, incl = dirs[d]
            rows = slice(sub[d] * CHUNK, (sub[d] + 1) * CHUNK)
            sl = slice(p * LANES, (p + 1) * LANES)
            items.append((at_ref[b, rows, sl], rt_ref[b, rows, sl], bt_ref[b, rows, sl],
                          kt_ref[b, rows, sl], v_ref[b, rows, sl], p_last[b, d][:, sl], strict, incl))
    states = [s_ref[b, d, p] for b, d, p in order]
    ys = []
    scan = _pair_chunks(items, states, levels, eye, lane0, bd_mask, ys)
    attn = _attn_stages(*attn_in, o_att_ref)
    while next(scan, StopIteration) is not StopIteration:
        next(attn, None)
    for _ in attn:
        pass
    for j in range(SCAN_CHUNKS):
        sub = (j, SCAN_CHUNKS - 1 - j)
        for b in range(n_batch):
            for d, y_ref in enumerate((yf_ref, yb_ref)):
                base = j * len(order) + (b * 2 + d) * n_pairs
                y_ref[b, sub[d] * CHUNK:(sub[d] + 1) * CHUNK, :] = jnp.concatenate(
                    ys[base:base + n_pairs], axis=1).astype(BF16)
    for (b, d, p), state in zip(order, states):
        s_ref[b, d, p] = state


def _attn_stages(sink_ref, *refs):
    n_sub = ATT_QBLOCKS
    bias_refs = refs[:n_sub]
    q_ref, kp_ref, kc_ref, kn_ref, vp_ref, vc_ref, vn_ref, o_ref = refs[n_sub:]
    blk = WINDOW
    k_all = jnp.concatenate([kp_ref[0], kc_ref[0], kn_ref[0]], axis=0)
    v_all = jnp.concatenate([vp_ref[0], vc_ref[0], vn_ref[0]], axis=0)

    def swap_halves(x):
        return jnp.concatenate([x[:, HEAD_DIM:], x[:, :HEAD_DIM]], axis=1)

    lane0 = lax.broadcasted_iota(jnp.int32, k_all.shape, 1) < HEAD_DIM

    def variants(x):
        xs = swap_halves(x)
        zero = jnp.zeros_like(x)
        return ((jnp.where(lane0, x, zero), jnp.where(lane0, zero, xs)),
                (jnp.where(lane0, xs, zero), jnp.where(lane0, zero, x)))

    k_var = variants(k_all)
    v_var = variants(v_all)
    yield

    group = ATT_HEADS // (KV_DIM // HEAD_DIM)

    items = [(u, h) for u in range(n_sub) for h in range(ATT_HEADS)]
    keys = lambda var, u, h: var[h // group][h % 2][u * blk:(u + 3) * blk]
    q_pairs = [[q_ref[0, u * blk:(u + 1) * blk, j * LANES:(j + 1) * LANES]
                * jnp.asarray(HEAD_DIM ** -0.5, BF16) for j in range(ATT_DIM // LANES)]
               for u in range(n_sub)]
    s = [_dot_nt(q_pairs[u][h // 2], keys(k_var, u, h)) + bias_refs[u][0, h] for u, h in items]
    yield
    m = [jnp.maximum(jnp.max(s[i], axis=-1, keepdims=True), sink_ref[h]) for i, (u, h) in enumerate(items)]
    yield
    p = [jnp.exp(s[i] - m[i]) for i in range(len(items))]
    yield
    den = [jnp.sum(p[i], axis=-1, keepdims=True) + jnp.exp(sink_ref[h] - m[i])
           for i, (u, h) in enumerate(items)]
    yield
    o = [_dot(p[i], keys(v_var, u, h)) * (1.0 / den[i]) for i, (u, h) in enumerate(items)]
    yield
    for u in range(n_sub):
        ou = o[u * ATT_HEADS:(u + 1) * ATT_HEADS]
        o_ref[0, u * blk:(u + 1) * blk, :] = jnp.concatenate(
            [ou[2 * j] + ou[2 * j + 1] for j in range(ATT_DIM // LANES)], axis=1).astype(o_ref.dtype)


def _attn_bias(blk):
    qi = np.arange(blk)[:, None]
    kj = np.arange(3 * blk)[None, :]
    dist = np.abs(kj - blk - qi)
    slopes = 2.0 ** (-8.0 * np.arange(1, ATT_HEADS + 1, dtype=np.float32) / ATT_HEADS)
    alibi = -slopes[:, None, None].astype(np.float32) * dist[None].astype(np.float32)
    out = []
    for has_prev, has_next in ((False, True), (True, True), (True, False)):
        valid = (dist <= WINDOW) & (has_prev | (kj >= blk)) & (has_next | (kj < 2 * blk))
        out.append(np.where(valid[None], alibi, np.float32(MASK_VALUE)))
    return jnp.asarray(np.stack(out), F32)


def _scan_attention(at0, rt0, bt0, kt0, at1, rt1, bt1, kt1, v, pl0, pl1, q, ka, va, sink, weights):
    B, T, C = v.shape
    nc = T // (CHUNK * SCAN_CHUNKS)
    blk = WINDOW
    n_sub = ATT_QBLOCKS
    per_row = T // (blk * n_sub)
    nb = T // blk
    assert B * per_row == nc, "scan and attention must have the same number of grid steps"
    fwd = pl.BlockSpec((B, SCAN_CHUNKS * CHUNK, C), lambda c: (0, c, 0))
    bwd = pl.BlockSpec((B, SCAN_CHUNKS * CHUNK, C), lambda c: (0, nc - 1 - c, 0))
    cpb = TM_IN // CHUNK
    assert cpb % SCAN_CHUNKS == 0, "a step's chunks must share one block of chunk decays"
    spb = cpb // SCAN_CHUNKS
    pl_f = pl.BlockSpec((B, 1, SUBLANES, C), lambda c: (0, c // spb, 0, 0))
    pl_b = pl.BlockSpec((B, 1, SUBLANES, C), lambda c: (0, (nc - 1 - c) // spb, 0, 0))
    y_shape = jax.ShapeDtypeStruct((B, T, C), BF16)

    row = lambda c: c // per_row
    step = lambda c: lax.rem(c, per_row)
    cur = lambda width: pl.BlockSpec((1, n_sub * blk, width), lambda c: (row(c), step(c), 0))
    prev = pl.BlockSpec((1, blk, KV_DIM), lambda c: (row(c), jnp.maximum(step(c) * n_sub - 1, 0), 0))
    nxt = pl.BlockSpec((1, blk, KV_DIM), lambda c: (row(c), jnp.minimum((step(c) + 1) * n_sub, nb - 1), 0))

    def bias_spec(u):
        def index(c):
            g = step(c) * n_sub + u
            return (jnp.where(g == 0, 0, jnp.where(g == nb - 1, 2, 1)), 0, 0, 0)
        return pl.BlockSpec((1, ATT_HEADS, blk, 3 * blk), index)

    def cast_spec(w):
        rows, cols = w.shape
        per_step = next(r for r in range(BF16_ROWS, rows + 1, BF16_ROWS)
                        if rows % r == 0 and rows // r <= nc)
        last = rows // per_step - 1
        return pl.BlockSpec((per_step, cols), lambda c: (jnp.minimum(c, last), 0))

    cast_specs = [cast_spec(w) for w in weights]
    bias = _attn_bias(blk)
    outs = pl.pallas_call(
        functools.partial(_scan_attn_kernel, n_cast=len(weights), chunks_per_block=cpb), grid=(nc,),
        in_specs=[fwd, fwd, fwd, fwd, fwd, pl_f, bwd, bwd, bwd, bwd, bwd, pl_b,
                  pl.BlockSpec(memory_space=pltpu.SMEM)] + [bias_spec(u) for u in range(n_sub)]
                 + [cur(ATT_DIM), prev, cur(KV_DIM), nxt, prev, cur(KV_DIM), nxt] + cast_specs,
        out_specs=[fwd, bwd, cur(ATT_DIM)] + cast_specs,
        out_shape=[y_shape, y_shape, jax.ShapeDtypeStruct((B, T, ATT_DIM), BF16)]
                  + [jax.ShapeDtypeStruct(w.shape, BF16) for w in weights],
        scratch_shapes=[pltpu.VMEM((B, 2, C // LANES, LANES, LANES), F32)],
        compiler_params=pltpu.CompilerParams(dimension_semantics=("arbitrary",),
                                             vmem_limit_bytes=VMEM_LIMIT),
        name="scan_attn",
    )(at0, rt0, bt0, kt0, v, pl0, at1, rt1, bt1, kt1, v, pl1,
      sink, *([bias] * n_sub), q, ka, ka, ka, va, va, va, *weights)
    return outs[0], outs[1], outs[2], outs[3:]


HALO = BF16_ROWS


def _mix_ffn_kernel(*refs):
    (x_m, x_p, x_n, yf_m, yf_p, yf_n, yb_m, yb_p, yb_n, bo_m, bo_p, bo_n, g_m, g_p, g_n,
     oa_m, oa_p, oa_n, lg_ref, lb_ref, wo_ref, ln2_ref, wg_ref, wu_ref, cw_ref, cb_ref,
     wd_ref, lnf_ref, o_ref) = refs
    i = pl.program_id(1)
    n_tiles = pl.num_programs(1)
    tm = x_m.shape[1]
    rows = tm + 2 * HALO
    core = slice(HALO, tm + HALO)
    ext = lambda m, p, n: jnp.concatenate([p[0], m[0], n[0]], axis=0)

    def seg_mean(t):
        return _head_sums(t) * (1.0 / HEAD_DIM)

    y = ext(yf_m, yf_p, yf_n).astype(F32) + ext(yb_m, yb_p, yb_n).astype(F32)
    d = y - seg_mean(y)
    var = seg_mean(d * d)
    yn = d * lax.rsqrt(var + LNX_EPS) * lg_ref[...] + lb_ref[...]
    o_rwkv = (yn + ext(bo_m, bo_p, bo_n)) * ext(g_m, g_p, g_n)
    mix = _dot(o_rwkv, wo_ref[:RWKV_DIM, :]) + _dot(ext(oa_m, oa_p, oa_n), wo_ref[RWKV_DIM:, :])
    x1 = ext(x_m, x_p, x_n) + mix
    r = lax.broadcasted_iota(jnp.int32, (rows, 1), 0)
    inside = ((r >= HALO) | (i > 0)) & ((r < tm + HALO) | (i < n_tiles - 1))
    x1 = jnp.where(inside, x1, 0.0)

    h = _rms_norm(x1, ln2_ref[...]).astype(BF16)
    h_core = h[core]
    acc = jnp.zeros((tm, D_MODEL), F32)
    for c0, c1 in zip(FF_SPLITS[:-1], FF_SPLITS[1:]):
        cs = slice(c0, c1)
        gp_ext = _dot(h, wg_ref[:, cs])
        prev = pltpu.roll(gp_ext, 1, axis=0)[core]
        nxt = pltpu.roll(gp_ext, rows - 1, axis=0)[core]
        gate = (prev * cw_ref[0:1, cs] + gp_ext[core] * cw_ref[1:2, cs] + nxt * cw_ref[2:3, cs]
                + cb_ref[:, cs])
        act = 0.5 * gate * (1.0 + lax.erf(gate * float(1.0 / np.sqrt(2.0))))
        up = _dot(h_core, wu_ref[:, cs])
        acc = acc + _dot(act * up, wd_ref[cs, :])
    o_ref[0] = _rms_norm(x1[core] + acc, lnf_ref[...])


def _mix_ffn(x, yf, yb, bonus, g, o_att, lnx_g, lnx_b, w_out, ln2_g, wg, wu, conv_w, conv_b,
             wd, lnf_g):
    B, T, D = x.shape
    tm = TM_FFN
    per_tile = tm // HALO
    last = T // HALO - 1

    def tok(width):
        return [pl.BlockSpec((1, tm, width), lambda b, i: (b, i, 0)),
                pl.BlockSpec((1, HALO, width), lambda b, i: (b, jnp.maximum(i * per_tile - 1, 0), 0)),
                pl.BlockSpec((1, HALO, width), lambda b, i: (b, jnp.minimum((i + 1) * per_tile, last), 0))]

    const = lambda shape: pl.BlockSpec(shape, lambda b, i: (0,) * len(shape))
    resident = lambda shape: pl.BlockSpec(shape, lambda b, i: (0,) * len(shape),
                                          pipeline_mode=pl.Buffered(1))
    tokens = (x, yf, yb, bonus, g, o_att)
    return pl.pallas_call(
        _mix_ffn_kernel, grid=(B, T // tm),
        in_specs=[spec for a in tokens for spec in tok(a.shape[-1])]
                 + [const((1, RWKV_DIM)), const((1, RWKV_DIM)), const((D, D)),
                    const((1, D)), resident((D, D_FF)), resident((D, D_FF)),
                    const((CONV_WIDTH, D_FF)), const((1, D_FF)), resident((D_FF, D)), const((1, D))],
        out_specs=pl.BlockSpec((1, tm, D), lambda b, i: (b, i, 0)),
        out_shape=jax.ShapeDtypeStruct((B, T, D), F32),
        compiler_params=pltpu.CompilerParams(dimension_semantics=("parallel", "parallel"),
                                             vmem_limit_bytes=VMEM_LIMIT),
        name="mix_ffn",
    )(*[a for a in tokens for _ in range(3)], lnx_g, lnx_b, w_out, ln2_g, wg, wu, conv_w,
      conv_b, wd, lnf_g)


def _chunk_triangles():
    t = np.arange(CHUNK)
    tri = np.stack([t[:, None] >= t[None, :], t[:, None] <= t[None, :]]).astype(np.float32)
    return jnp.asarray(np.concatenate([tri, tri], axis=2), BF16)


def kernel(x, ln1_g, w_in, shift_mu_prev, shift_mu_next, decay_w0, decay_w2, iclr_a0, iclr_a2,
           gate_g2, k_k, k_a, r_k, lnx_g, lnx_b, attn_sink, w_out, ln2_g, ffn_w_gate, ffn_w_up,
           ffn_conv_w, ffn_conv_b, ffn_w_down, lnf_g):
    B, T, _ = x.shape
    assert w_in.shape[0] == 1, "single-layer block"
    l = 0
    tri = _chunk_triangles()
    row = lambda a: a.reshape(1, -1)
    w2 = decay_w2[l]
    w2_pad = jnp.concatenate([w2, jnp.zeros_like(w2)], axis=1)
    w2_bf = w2_pad.astype(BF16)
    w2_cat = jnp.concatenate([w2_bf, w2_bf], axis=1)
    mu_p, mu_n = shift_mu_prev[l], shift_mu_next[l]
    a2_pad = jnp.concatenate([jnp.zeros_like(iclr_a2[l]), iclr_a2[l]], axis=0).astype(BF16)
    (at0, rt0, bt0, kt0, at1, rt1, bt1, kt1, v, pl0, pl1, g, bonus, q, ka, va) = _in_proj(
        x, row(ln1_g[l]), w_in[l], row(1.0 - mu_p - mu_n), row(mu_p), row(mu_n),
        decay_w0[l], w2_cat, row(iclr_a0[l]), a2_pad, gate_g2[l].astype(BF16),
        row(k_k[l]), row(k_a[l]), row(r_k[l]), tri)
    yf, yb, o_att, (wo, wg, wu, wd) = _scan_attention(
        at0, rt0, bt0, kt0, at1, rt1, bt1, kt1, v, pl0, pl1, q, ka, va, attn_sink[l],
        (w_out[l], ffn_w_gate[l], ffn_w_up[l], ffn_w_down[l]))
    return _mix_ffn(x, yf, yb, bonus, g, o_att, row(lnx_g[l]), row(lnx_b[l]), wo,
                    row(ln2_g[l]), wg, wu, ffn_conv_w[l], row(ffn_conv_b[l]), wd, row(lnf_g))
```

```python
import functools

import numpy as np
import jax
import jax.numpy as jnp
from jax import lax
from jax.experimental import pallas as pl
from jax.experimental.pallas import tpu as pltpu

F32 = jnp.float32
BF16 = jnp.bfloat16

D_MODEL = 1024
HEAD_DIM = 64
RWKV_DIM = 512
ATT_DIM = 512
ATT_HEADS = 8
KV_DIM = 128
LORA_DIM = 256
SHIFT_DIM = 3 * RWKV_DIM + LORA_DIM
PROJ_DIM = SHIFT_DIM + ATT_DIM + 2 * KV_DIM
WINDOW = 128
D_FF = 2816
CONV_WIDTH = 3
NORM_EPS = 1e-6
LNX_EPS = 64e-5
L2_EPS = 1e-12
MASK_VALUE = -1e30
LOG2E = float(np.log2(np.e))
NEG_DECAY_SCALE = float(-np.exp(-0.5) * np.log2(np.e))

LANES = 128
SUBLANES = 8
BF16_ROWS = 16
CHUNK = 64
VMEM_LIMIT = 56 * 1024 * 1024

TM_IN = 512
IN_PARTS = 2
TM_FFN = 512
SCAN_CHUNKS = 2
ATT_QBLOCKS = 4
FF_SPLITS = (0, 1536, D_FF)


def _dot(a, b):
    return jnp.dot(a.astype(BF16), b.astype(BF16), preferred_element_type=F32)


def _dot_nt(a, b):
    return lax.dot_general(a.astype(BF16), b.astype(BF16), (((1,), (1,)), ((), ())),
                           preferred_element_type=F32)


def _dot_tn(a, b):
    return lax.dot_general(a.astype(BF16), b.astype(BF16), (((0,), (0,)), ((), ())),
                           preferred_element_type=F32)


def _split2(x):
    hi = x.astype(BF16)
    lo = (x - hi.astype(F32)).astype(BF16)
    return hi, lo


def _rms_norm(x, g):
    return x * lax.rsqrt(jnp.mean(x * x, axis=-1, keepdims=True) + NORM_EPS) * g


def _head_sums(t):
    first_head = lax.broadcasted_iota(jnp.int32, (1, LANES), 1) < HEAD_DIM
    tiles = []
    for j in range(t.shape[-1] // LANES):
        tj = t[:, j * LANES:(j + 1) * LANES]
        both = jnp.sum(tj, axis=-1, keepdims=True)
        head0 = jnp.sum(jnp.where(first_head, tj, 0.0), axis=-1, keepdims=True)
        tiles.append(jnp.where(first_head, head0, both - head0))
    return jnp.concatenate(tiles, axis=1)


def _in_proj_kernel(x_ref, xp_ref, xn_ref, ln1_ref, w_ref, muc_ref, mup_ref, mun_ref, w0_ref,
                    w2_ref, a0_ref, a2_ref, g2_ref, kk_ref, ka_ref, rk_ref, tri_ref,
                    pk0_ref, pk1_ref, v_ref, pl0_ref, pl1_ref, g_ref, bonus_ref, q_ref, ka_o_ref, va_o_ref,
                    wbf_ref):
    i = pl.program_id(1)
    n_tiles = pl.num_programs(1)
    tm = x_ref.shape[1]

    @pl.when((pl.program_id(0) == 0) & (i == 0))
    def _():
        wbf_ref[...] = w_ref[...].astype(BF16)

    hm = tm // IN_PARTS
    rows = hm + 2 * SUBLANES
    core = slice(SUBLANES, hm + SUBLANES)
    halo_lo = jnp.where(i > 0, xp_ref[0], 0.0)
    halo_hi = jnp.where(i < n_tiles - 1, xn_ref[0], 0.0)
    dir_outs = ((pk0_ref, pl0_ref), (pk1_ref, pl1_ref))

    def part(lo):
        out = slice(lo, lo + hm)
        before = halo_lo if lo == 0 else x_ref[0, lo - SUBLANES:lo, :]
        after = halo_hi if lo + hm == tm else x_ref[0, lo + hm:lo + hm + SUBLANES, :]
        x_ext = jnp.concatenate([before, x_ref[0, out, :], after], axis=0)
        h = _rms_norm(x_ext, ln1_ref[...]).astype(BF16)
        proj = lambda c0, c1: _dot(h, wbf_ref[:, c0:c1])
        p_codes = proj(3 * RWKV_DIM, SHIFT_DIM)
        p_k = proj(RWKV_DIM, 2 * RWKV_DIM)
        p_r = proj(0, RWKV_DIM)
        p_v = proj(2 * RWKV_DIM, 3 * RWKV_DIM)
        yield

        def shifted(p_ext, c0, c1):
            prev = pltpu.roll(p_ext, 1, axis=0)[core]
            nxt = pltpu.roll(p_ext, rows - 1, axis=0)[core]
            return (p_ext[core] * muc_ref[:, c0:c1] + prev * mup_ref[:, c0:c1]
                    + nxt * mun_ref[:, c0:c1])

        codes = shifted(p_codes, 3 * RWKV_DIM, SHIFT_DIM)
        c_di = codes[:, :LANES]
        th_hi, th_lo = _split2(jnp.tanh(c_di))
        th_cat = jnp.concatenate([th_hi, th_lo], axis=1)
        gate_code = jax.nn.sigmoid(codes[:, LANES:])
        k = shifted(p_k, RWKV_DIM, 2 * RWKV_DIM)
        kkr = k * kk_ref[...]
        kkr_sq = kkr * kkr
        r = shifted(p_r, 0, RWKV_DIM)
        v = shifted(p_v, 2 * RWKV_DIM, 3 * RWKV_DIM)
        v_ref[0, out, :] = v.astype(BF16)
        yield

        a_pre = _dot(c_di, a2_ref[...])
        g_ref[0, out, :] = _dot(gate_code, g2_ref[...]).astype(BF16)
        n2 = _head_sums(kkr_sq)
        z = [w0_ref[d:d + 1, :] + jnp.dot(th_cat, w2_ref[d], preferred_element_type=F32)
             for d in range(2)]
        yield

        a_vec = jax.nn.sigmoid(a0_ref[...] + a_pre)
        kk = kkr * lax.rsqrt(jnp.maximum(n2, L2_EPS * L2_EPS))
        k2 = k * (1.0 + (a_vec - 1.0) * ka_ref[...])
        b_vec = kk * a_vec
        neg_kk = -kk
        rk2 = r * k2 * rk_ref[...]
        lw = [NEG_DECAY_SCALE / (1.0 + jnp.exp2(z[d] * (-LOG2E))) for d in range(2)]
        lw_split = [_split2(lw[d]) for d in range(2)]
        yield

        bonus_sum = _head_sums(rk2)
        att = proj(SHIFT_DIM, PROJ_DIM)[core]
        q_ref[0, out, :] = att[:, :ATT_DIM].astype(BF16)
        ka_o_ref[0, out, :] = att[:, ATT_DIM:ATT_DIM + KV_DIM].astype(BF16)
        va_o_ref[0, out, :] = att[:, ATT_DIM + KV_DIM:].astype(BF16)
        parts = [[jnp.dot(tri_ref[d],
                          jnp.concatenate([lw_split[d][0][cs], lw_split[d][1][cs]], axis=0),
                          preferred_element_type=F32)
                  for cs in (slice(j * CHUNK, (j + 1) * CHUNK) for j in range(hm // CHUNK))]
                 for d in range(2)]
        yield

        bonus_ref[0, out, :] = (bonus_sum * v).astype(BF16)
        for d, (pk_ref, pl_ref) in enumerate(dir_outs):
            ci = jnp.concatenate(parts[d], axis=0)
            end = 0 if d else CHUNK - 1
            tot = jnp.concatenate([c[end:end + 1] for c in parts[d]], axis=0)
            pl_ref[0, 0, lo // CHUNK:(lo + hm) // CHUNK, :] = jnp.exp2(tot)
            e_inc = jnp.exp2(ci)
            e_exc = jnp.exp2(ci - lw[d])
            e_inv = 1.0 / e_inc
            packed = (neg_kk * e_exc, r * e_inc, b_vec * e_inv, k2 * e_inv)
            for f, value in enumerate(packed):
                pk_ref[0, out, f * RWKV_DIM:(f + 1) * RWKV_DIM] = value.astype(BF16)

    live = [part(p * hm) for p in range(IN_PARTS)]
    while live:
        live = [g for g in live if next(g, StopIteration) is not StopIteration]
    if tm // CHUNK < SUBLANES:
        for d in range(2):
            dir_outs[d][1][0, 0, tm // CHUNK:, :] = jnp.ones((SUBLANES - tm // CHUNK, RWKV_DIM), F32)


def _in_proj(x, ln1_g, w_in, mu_cur, mu_prev, mu_next, decay_w0, w2_cat, iclr_a0, a2_pad, gate_g2,
             k_k, k_a, r_k, tri):
    B, T, D = x.shape
    tm = TM_IN
    nt = T // tm
    rows8 = tm // SUBLANES
    const = lambda shape: pl.BlockSpec(shape, lambda b, i: (0,) * len(shape))
    tok = lambda width: pl.BlockSpec((1, tm, width), lambda b, i: (b, i, 0))
    in_specs = [
        tok(D),
        pl.BlockSpec((1, SUBLANES, D), lambda b, i: (b, jnp.maximum(i * rows8 - 1, 0), 0)),
        pl.BlockSpec((1, SUBLANES, D), lambda b, i: (b, jnp.minimum((i + 1) * rows8, T // SUBLANES - 1), 0)),
        const((1, D)),
        pl.BlockSpec((D, PROJ_DIM), lambda b, i: (0, 0), pipeline_mode=pl.Buffered(1)),
        const((1, SHIFT_DIM)), const((1, SHIFT_DIM)), const((1, SHIFT_DIM)),
        const((2, RWKV_DIM)), const((2, 2 * LANES, RWKV_DIM)),
        const((1, RWKV_DIM)), const((LANES, RWKV_DIM)), const((LANES, RWKV_DIM)),
        const((1, RWKV_DIM)), const((1, RWKV_DIM)), const((1, RWKV_DIM)),
        const((2, CHUNK, 2 * CHUNK)),
    ]
    tok_bf = jax.ShapeDtypeStruct((B, T, RWKV_DIM), BF16)
    pl_shape = jax.ShapeDtypeStruct((B, nt, SUBLANES, RWKV_DIM), F32)
    pl_spec = pl.BlockSpec((1, 1, SUBLANES, RWKV_DIM), lambda b, i: (b, i, 0, 0))
    packed = jax.ShapeDtypeStruct((B, T, 4 * RWKV_DIM), BF16)
    out_shape = [packed, packed, tok_bf, pl_shape, pl_shape, tok_bf, tok_bf,
                                jax.ShapeDtypeStruct((B, T, ATT_DIM), BF16),
                                jax.ShapeDtypeStruct((B, T, KV_DIM), BF16),
                                jax.ShapeDtypeStruct((B, T, KV_DIM), BF16)]
    out_specs = [tok(4 * RWKV_DIM), tok(4 * RWKV_DIM), tok(RWKV_DIM), pl_spec, pl_spec, tok(RWKV_DIM), tok(RWKV_DIM),
                                       tok(ATT_DIM), tok(KV_DIM), tok(KV_DIM)]
    return pl.pallas_call(
        _in_proj_kernel, grid=(B, nt), in_specs=in_specs, out_specs=out_specs, out_shape=out_shape,
        scratch_shapes=[pltpu.VMEM((D, PROJ_DIM), BF16)],
        compiler_params=pltpu.CompilerParams(dimension_semantics=("arbitrary", "arbitrary"),
                                             vmem_limit_bytes=VMEM_LIMIT),
        name="in_proj",
    )(x, x, x, ln1_g, w_in, mu_cur, mu_prev, mu_next, decay_w0, w2_cat, iclr_a0, a2_pad, gate_g2,
      k_k, k_a, r_k, tri)


def _pair_chunks(items, states, levels, eye, lane0, bd_mask, ys):
    n = range(len(items))
    at, rt, bt, kt, v, p_last, strict, incl = zip(*items)

    def bd(x):
        x = x.astype(BF16)
        zero = jnp.zeros_like(x)
        return jnp.concatenate([jnp.where(lane0, x, zero), jnp.where(lane0, zero, x)], axis=0)

    def pmm(x, y):
        return _dot(x, bd(y))

    sc = [_dot_nt(jnp.concatenate([at[i], rt[i]], axis=0),
                  jnp.concatenate([bd(bt[i]), bd(kt[i])], axis=0)) for i in n]
    a_ab = [jnp.where(strict[i], sc[i][:CHUNK, :LANES], 0.0) for i in n]
    a_ak = [jnp.where(strict[i], sc[i][:CHUNK, LANES:], 0.0) for i in n]
    a_rb = [jnp.where(incl[i], sc[i][CHUNK:, :LANES], 0.0) for i in n]
    a_rk = [jnp.where(incl[i], sc[i][CHUNK:, LANES:], 0.0) for i in n]
    yield

    xy = [pmm(jnp.concatenate([a_ak[i], a_rk[i]], axis=0), v[i]) for i in n]
    x1 = [xy[i][:CHUNK] for i in n]
    yk = [xy[i][CHUNK:] for i in n]
    t_inv = [eye + jnp.where(levels[0], a_ab[i], 0.0) for i in n]
    yield
    for level in levels[1:]:
        e_t = [pmm(jnp.where(level, a_ab[i], 0.0), t_inv[i]) for i in n]
        yield
        t_inv = [t_inv[i] + pmm(t_inv[i], e_t[i]) for i in n]
        yield
    wu = [_dot(t_inv[i], jnp.concatenate([bd(at[i]), bd(x1[i])], axis=1)) for i in n]
    yield
    m = len(states)
    for first in range(0, len(items), m):
        n = range(first, first + m)
        hs = {i: _dot_nt(jnp.concatenate([wu[i][:, :LANES].astype(BF16), rt[i]], axis=0), states[i - first])
              for i in n}
        yield
        u = {i: hs[i][:CHUNK] + wu[i][:, LANES:] for i in n}
        ys.extend(hs[i][CHUNK:] + pmm(a_rb[i], u[i]) + yk[i] for i in n)
        yield
        upd = {i: _dot_tn(jnp.concatenate([u[i].astype(BF16), v[i]], axis=0),
                          jnp.concatenate([bt[i], kt[i]], axis=0)) for i in n}
        states[:] = [(states[i - first] + jnp.where(bd_mask, upd[i], 0.0)) * p_last[i] for i in n]
        yield


def _scan_attn_kernel(pk0_ref, v0_ref, pl0_ref, pk1_ref, v1_ref, pl1_ref, *rest, n_cast,
                      chunks_per_block):
    n_attn = ATT_QBLOCKS + 8
    attn_in = rest[:n_attn]
    cast_in = rest[n_attn:n_attn + n_cast]
    yf_ref, yb_ref, o_att_ref = rest[n_attn + n_cast:n_attn + n_cast + 3]
    cast_out = rest[n_attn + n_cast + 3:n_attn + 2 * n_cast + 3]
    s_ref = rest[-1]
    c = pl.program_id(0)
    for src_ref, dst_ref in zip(cast_in, cast_out):
        dst_ref[...] = src_ref[...].astype(BF16)

    @pl.when(c == 0)
    def _():
        s_ref[...] = jnp.zeros_like(s_ref)

    n_batch = v0_ref.shape[0]
    ri = lax.broadcasted_iota(jnp.int32, (CHUNK, LANES), 0)
    ci = lax.broadcasted_iota(jnp.int32, (CHUNK, LANES), 1)
    cj = jnp.where(ci >= CHUNK, ci - CHUNK, ci)
    lane0 = ci < CHUNK
    eye = jnp.where(ri == cj, 1.0, 0.0).astype(F32)
    levels = [((ri // (2 * s)) == (cj // (2 * s))) & ((ri // s) != (cj // s))
              for s in (2 ** e for e in range(CHUNK.bit_length() - 1))]
    r2 = lax.broadcasted_iota(jnp.int32, (LANES, LANES), 0)
    c2 = lax.broadcasted_iota(jnp.int32, (LANES, LANES), 1)
    bd_mask = (r2 >= CHUNK) == (c2 >= CHUNK)
    dirs = ((pk0_ref, v0_ref, pl0_ref, ri > cj, ri >= cj),
            (pk1_ref, v1_ref, pl1_ref, ri < cj, ri <= cj))
    n_pairs = RWKV_DIM // LANES
    n_chunks = pl.num_programs(0) * SCAN_CHUNKS
    order = [(b, d, p) for b in range(n_batch) for d in range(2) for p in range(n_pairs)]
    items = []
    for j in range(SCAN_CHUNKS):
        first = c * SCAN_CHUNKS + j
        chunk = (first, n_chunks - 1 - first)
        sub = (j, SCAN_CHUNKS - 1 - j)
        p_last = {(b, d): dirs[d][2][b, 0, pl.ds(lax.rem(chunk[d], chunks_per_block), 1), :]
                  for b in range(n_batch) for d in range(2)}
        for b, d, p in order:
            pk_ref, v_ref, _, strict, incl = dirs[d]
            rows = slice(sub[d] * CHUNK, (sub[d] + 1) * CHUNK)
            sl = slice(p * LANES, (p + 1) * LANES)
            field = lambda f: pk_ref[b, rows, f * RWKV_DIM + p * LANES:f * RWKV_DIM + (p + 1) * LANES]
            items.append((field(0), field(1), field(2), field(3), v_ref[b, rows, sl],
                          p_last[b, d][:, sl], strict, incl))
    states = [s_ref[b, d, p] for b, d, p in order]
    ys = []
    scan = _pair_chunks(items, states, levels, eye, lane0, bd_mask, ys)
    attn = _attn_stages(*attn_in, o_att_ref)
    while next(scan, StopIteration) is not StopIteration:
        next(attn, None)
    for _ in attn:
        pass
    for j in range(SCAN_CHUNKS):
        sub = (j, SCAN_CHUNKS - 1 - j)
        for b in range(n_batch):
            for d, y_ref in enumerate((yf_ref, yb_ref)):
                base = j * len(order) + (b * 2 + d) * n_pairs
                y_ref[b, sub[d] * CHUNK:(sub[d] + 1) * CHUNK, :] = jnp.concatenate(
                    ys[base:base + n_pairs], axis=1).astype(BF16)
    for (b, d, p), state in zip(order, states):
        s_ref[b, d, p] = state


def _attn_stages(sink_ref, *refs):
    n_sub = ATT_QBLOCKS
    bias_refs = refs[:n_sub]
    q_ref, kp_ref, kc_ref, kn_ref, vp_ref, vc_ref, vn_ref, o_ref = refs[n_sub:]
    blk = WINDOW
    k_all = jnp.concatenate([kp_ref[0], kc_ref[0], kn_ref[0]], axis=0)
    v_all = jnp.concatenate([vp_ref[0], vc_ref[0], vn_ref[0]], axis=0)

    def swap_halves(x):
        return jnp.concatenate([x[:, HEAD_DIM:], x[:, :HEAD_DIM]], axis=1)

    lane0 = lax.broadcasted_iota(jnp.int32, k_all.shape, 1) < HEAD_DIM

    def variants(x):
        xs = swap_halves(x)
        zero = jnp.zeros_like(x)
        return ((jnp.where(lane0, x, zero), jnp.where(lane0, zero, xs)),
                (jnp.where(lane0, xs, zero), jnp.where(lane0, zero, x)))

    k_var = variants(k_all)
    v_var = variants(v_all)
    yield

    group = ATT_HEADS // (KV_DIM // HEAD_DIM)

    items = [(u, h) for u in range(n_sub) for h in range(ATT_HEADS)]
    keys = lambda var, u, h: var[h // group][h % 2][u * blk:(u + 3) * blk]
    q_pairs = [[q_ref[0, u * blk:(u + 1) * blk, j * LANES:(j + 1) * LANES]
                * jnp.asarray(HEAD_DIM ** -0.5, BF16) for j in range(ATT_DIM // LANES)]
               for u in range(n_sub)]
    s = [_dot_nt(q_pairs[u][h // 2], keys(k_var, u, h)) + bias_refs[u][0, h] for u, h in items]
    yield
    m = [jnp.maximum(jnp.max(s[i], axis=-1, keepdims=True), sink_ref[h]) for i, (u, h) in enumerate(items)]
    yield
    p = [jnp.exp(s[i] - m[i]) for i in range(len(items))]
    yield
    den = [jnp.sum(p[i], axis=-1, keepdims=True) + jnp.exp(sink_ref[h] - m[i])
           for i, (u, h) in enumerate(items)]
    yield
    o = [_dot(p[i], keys(v_var, u, h)) * (1.0 / den[i]) for i, (u, h) in enumerate(items)]
    yield
    for u in range(n_sub):
        ou = o[u * ATT_HEADS:(u + 1) * ATT_HEADS]
        o_ref[0, u * blk:(u + 1) * blk, :] = jnp.concatenate(
            [ou[2 * j] + ou[2 * j + 1] for j in range(ATT_DIM // LANES)], axis=1).astype(o_ref.dtype)


def _attn_bias(blk):
    qi = np.arange(blk)[:, None]
    kj = np.arange(3 * blk)[None, :]
    dist = np.abs(kj - blk - qi)
    slopes = 2.0 ** (-8.0 * np.arange(1, ATT_HEADS + 1, dtype=np.float32) / ATT_HEADS)
    alibi = -slopes[:, None, None].astype(np.float32) * dist[None].astype(np.float32)
    out = []
    for has_prev, has_next in ((False, True), (True, True), (True, False)):
        valid = (dist <= WINDOW) & (has_prev | (kj >= blk)) & (has_next | (kj < 2 * blk))
        out.append(np.where(valid[None], alibi, np.float32(MASK_VALUE)))
    return jnp.asarray(np.stack(out), F32)


def _scan_attention(pk0, pk1, v, pl0, pl1, q, ka, va, sink, weights):
    B, T, C = v.shape
    nc = T // (CHUNK * SCAN_CHUNKS)
    blk = WINDOW
    n_sub = ATT_QBLOCKS
    per_row = T // (blk * n_sub)
    nb = T // blk
    assert B * per_row == nc, "scan and attention must have the same number of grid steps"
    fwd = pl.BlockSpec((B, SCAN_CHUNKS * CHUNK, C), lambda c: (0, c, 0))
    bwd = pl.BlockSpec((B, SCAN_CHUNKS * CHUNK, C), lambda c: (0, nc - 1 - c, 0))
    fwd_pk = pl.BlockSpec((B, SCAN_CHUNKS * CHUNK, 4 * C), lambda c: (0, c, 0))
    bwd_pk = pl.BlockSpec((B, SCAN_CHUNKS * CHUNK, 4 * C), lambda c: (0, nc - 1 - c, 0))
    cpb = TM_IN // CHUNK
    assert cpb % SCAN_CHUNKS == 0, "a step's chunks must share one block of chunk decays"
    spb = cpb // SCAN_CHUNKS
    pl_f = pl.BlockSpec((B, 1, SUBLANES, C), lambda c: (0, c // spb, 0, 0))
    pl_b = pl.BlockSpec((B, 1, SUBLANES, C), lambda c: (0, (nc - 1 - c) // spb, 0, 0))
    y_shape = jax.ShapeDtypeStruct((B, T, C), BF16)

    row = lambda c: c // per_row
    step = lambda c: lax.rem(c, per_row)
    cur = lambda width: pl.BlockSpec((1, n_sub * blk, width), lambda c: (row(c), step(c), 0))
    prev = pl.BlockSpec((1, blk, KV_DIM), lambda c: (row(c), jnp.maximum(step(c) * n_sub - 1, 0), 0))
    nxt = pl.BlockSpec((1, blk, KV_DIM), lambda c: (row(c), jnp.minimum((step(c) + 1) * n_sub, nb - 1), 0))

    def bias_spec(u):
        def index(c):
            g = step(c) * n_sub + u
            return (jnp.where(g == 0, 0, jnp.where(g == nb - 1, 2, 1)), 0, 0, 0)
        return pl.BlockSpec((1, ATT_HEADS, blk, 3 * blk), index)

    def cast_spec(w):
        rows, cols = w.shape
        per_step = next(r for r in range(BF16_ROWS, rows + 1, BF16_ROWS)
                        if rows % r == 0 and rows // r <= nc)
        last = rows // per_step - 1
        return pl.BlockSpec((per_step, cols), lambda c: (jnp.minimum(c, last), 0))

    cast_specs = [cast_spec(w) for w in weights]
    bias = _attn_bias(blk)
    outs = pl.pallas_call(
        functools.partial(_scan_attn_kernel, n_cast=len(weights), chunks_per_block=cpb), grid=(nc,),
        in_specs=[fwd_pk, fwd, pl_f, bwd_pk, bwd, pl_b,
                  pl.BlockSpec(memory_space=pltpu.SMEM)] + [bias_spec(u) for u in range(n_sub)]
                 + [cur(ATT_DIM), prev, cur(KV_DIM), nxt, prev, cur(KV_DIM), nxt] + cast_specs,
        out_specs=[fwd, bwd, cur(ATT_DIM)] + cast_specs,
        out_shape=[y_shape, y_shape, jax.ShapeDtypeStruct((B, T, ATT_DIM), BF16)]
                  + [jax.ShapeDtypeStruct(w.shape, BF16) for w in weights],
        scratch_shapes=[pltpu.VMEM((B, 2, C // LANES, LANES, LANES), F32)],
        compiler_params=pltpu.CompilerParams(dimension_semantics=("arbitrary",),
                                             vmem_limit_bytes=VMEM_LIMIT),
        name="scan_attn",
    )(pk0, v, pl0, pk1, v, pl1,
      sink, *([bias] * n_sub), q, ka, ka, ka, va, va, va, *weights)
    return outs[0], outs[1], outs[2], outs[3:]


HALO = BF16_ROWS


def _mix_ffn_kernel(*refs):
    (x_m, x_p, x_n, yf_m, yf_p, yf_n, yb_m, yb_p, yb_n, bo_m, bo_p, bo_n, g_m, g_p, g_n,
     oa_m, oa_p, oa_n, lg_ref, lb_ref, wo_ref, ln2_ref, wg_ref, wu_ref, cw_ref, cb_ref,
     wd_ref, lnf_ref, o_ref) = refs
    i = pl.program_id(1)
    n_tiles = pl.num_programs(1)
    tm = x_m.shape[1]
    rows = tm + 2 * HALO
    core = slice(HALO, tm + HALO)
    ext = lambda m, p, n: jnp.concatenate([p[0], m[0], n[0]], axis=0)

    def seg_mean(t):
        return _head_sums(t) * (1.0 / HEAD_DIM)

    y = ext(yf_m, yf_p, yf_n).astype(F32) + ext(yb_m, yb_p, yb_n).astype(F32)
    d = y - seg_mean(y)
    var = seg_mean(d * d)
    yn = d * lax.rsqrt(var + LNX_EPS) * lg_ref[...] + lb_ref[...]
    o_rwkv = (yn + ext(bo_m, bo_p, bo_n)) * ext(g_m, g_p, g_n)
    mix = _dot(o_rwkv, wo_ref[:RWKV_DIM, :]) + _dot(ext(oa_m, oa_p, oa_n), wo_ref[RWKV_DIM:, :])
    x1 = ext(x_m, x_p, x_n) + mix
    r = lax.broadcasted_iota(jnp.int32, (rows, 1), 0)
    inside = ((r >= HALO) | (i > 0)) & ((r < tm + HALO) | (i < n_tiles - 1))
    x1 = jnp.where(inside, x1, 0.0)

    h = _rms_norm(x1, ln2_ref[...]).astype(BF16)
    h_core = h[core]
    acc = jnp.zeros((tm, D_MODEL), F32)
    for c0, c1 in zip(FF_SPLITS[:-1], FF_SPLITS[1:]):
        cs = slice(c0, c1)
        gp_ext = _dot(h, wg_ref[:, cs])
        prev = pltpu.roll(gp_ext, 1, axis=0)[core]
        nxt = pltpu.roll(gp_ext, rows - 1, axis=0)[core]
        gate = (prev * cw_ref[0:1, cs] + gp_ext[core] * cw_ref[1:2, cs] + nxt * cw_ref[2:3, cs]
                + cb_ref[:, cs])
        act = 0.5 * gate * (1.0 + lax.erf(gate * float(1.0 / np.sqrt(2.0))))
        up = _dot(h_core, wu_ref[:, cs])
        acc = acc + _dot(act * up, wd_ref[cs, :])
    o_ref[0] = _rms_norm(x1[core] + acc, lnf_ref[...])


def _mix_ffn(x, yf, yb, bonus, g, o_att, lnx_g, lnx_b, w_out, ln2_g, wg, wu, conv_w, conv_b,
             wd, lnf_g):
    B, T, D = x.shape
    tm = TM_FFN
    per_tile = tm // HALO
    last = T // HALO - 1

    def tok(width):
        return [pl.BlockSpec((1, tm, width), lambda b, i: (b, i, 0)),
                pl.BlockSpec((1, HALO, width), lambda b, i: (b, jnp.maximum(i * per_tile - 1, 0), 0)),
                pl.BlockSpec((1, HALO, width), lambda b, i: (b, jnp.minimum((i + 1) * per_tile, last), 0))]

    const = lambda shape: pl.BlockSpec(shape, lambda b, i: (0,) * len(shape))
    resident = lambda shape: pl.BlockSpec(shape, lambda b, i: (0,) * len(shape),
                                          pipeline_mode=pl.Buffered(1))
    tokens = (x, yf, yb, bonus, g, o_att)
    return pl.pallas_call(
        _mix_ffn_kernel, grid=(B, T // tm),
        in_specs=[spec for a in tokens for spec in tok(a.shape[-1])]
                 + [const((1, RWKV_DIM)), const((1, RWKV_DIM)), const((D, D)),
                    const((1, D)), resident((D, D_FF)), resident((D, D_FF)),
                    const((CONV_WIDTH, D_FF)), const((1, D_FF)), resident((D_FF, D)), const((1, D))],
        out_specs=pl.BlockSpec((1, tm, D), lambda b, i: (b, i, 0)),
        out_shape=jax.ShapeDtypeStruct((B, T, D), F32),
        compiler_params=pltpu.CompilerParams(dimension_semantics=("parallel", "parallel"),
                                             vmem_limit_bytes=VMEM_LIMIT),
        name="mix_ffn",
    )(*[a for a in tokens for _ in range(3)], lnx_g, lnx_b, w_out, ln2_g, wg, wu, conv_w,
      conv_b, wd, lnf_g)


def _chunk_triangles():
    t = np.arange(CHUNK)
    tri = np.stack([t[:, None] >= t[None, :], t[:, None] <= t[None, :]]).astype(np.float32)
    return jnp.asarray(np.concatenate([tri, tri], axis=2), BF16)


def kernel(x, ln1_g, w_in, shift_mu_prev, shift_mu_next, decay_w0, decay_w2, iclr_a0, iclr_a2,
           gate_g2, k_k, k_a, r_k, lnx_g, lnx_b, attn_sink, w_out, ln2_g, ffn_w_gate, ffn_w_up,
           ffn_conv_w, ffn_conv_b, ffn_w_down, lnf_g):
    B, T, _ = x.shape
    assert w_in.shape[0] == 1, "single-layer block"
    l = 0
    tri = _chunk_triangles()
    row = lambda a: a.reshape(1, -1)
    w2 = decay_w2[l]
    w2_pad = jnp.concatenate([w2, jnp.zeros_like(w2)], axis=1)
    w2_bf = w2_pad.astype(BF16)
    w2_cat = jnp.concatenate([w2_bf, w2_bf], axis=1)
    mu_p, mu_n = shift_mu_prev[l], shift_mu_next[l]
    a2_pad = jnp.concatenate([jnp.zeros_like(iclr_a2[l]), iclr_a2[l]], axis=0).astype(BF16)
    (pk0, pk1, v, pl0, pl1, g, bonus, q, ka, va) = _in_proj(
        x, row(ln1_g[l]), w_in[l], row(1.0 - mu_p - mu_n), row(mu_p), row(mu_n),
        decay_w0[l], w2_cat, row(iclr_a0[l]), a2_pad, gate_g2[l].astype(BF16),
        row(k_k[l]), row(k_a[l]), row(r_k[l]), tri)
    yf, yb, o_att, (wo, wg, wu, wd) = _scan_attention(
        pk0, pk1, v, pl0, pl1, q, ka, va, attn_sink[l],
        (w_out[l], ffn_w_gate[l], ffn_w_up[l], ffn_w_down[l]))
    return _mix_ffn(x, yf, yb, bonus, g, o_att, row(lnx_g[l]), row(lnx_b[l]), wo,
                    row(ln2_g[l]), wg, wu, ffn_conv_w[l], row(ffn_conv_b[l]), wd, row(lnf_g))
```

```python
import functools

import numpy as np
import jax
import jax.numpy as jnp
from jax import lax
from jax.experimental import pallas as pl
from jax.experimental.pallas import tpu as pltpu

F32 = jnp.float32
BF16 = jnp.bfloat16

D_MODEL = 1024
HEAD_DIM = 64
RWKV_DIM = 512
ATT_DIM = 512
ATT_HEADS = 8
KV_DIM = 128
LORA_DIM = 256
SHIFT_DIM = 3 * RWKV_DIM + LORA_DIM
PROJ_DIM = SHIFT_DIM + ATT_DIM + 2 * KV_DIM
WINDOW = 128
D_FF = 2816
CONV_WIDTH = 3
NORM_EPS = 1e-6
LNX_EPS = 64e-5
L2_EPS = 1e-12
MASK_VALUE = -1e30
LOG2E = float(np.log2(np.e))
NEG_DECAY_SCALE = float(-np.exp(-0.5) * np.log2(np.e))

LANES = 128
SUBLANES = 8
BF16_ROWS = 16
CHUNK = 64
VMEM_LIMIT = 56 * 1024 * 1024

TM_IN = 512
IN_PARTS = 2
TM_FFN = 512
SCAN_CHUNKS = 2
ATT_QBLOCKS = 4
FF_SPLITS = (0, 1536, D_FF)


def _dot(a, b):
    return jnp.dot(a.astype(BF16), b.astype(BF16), preferred_element_type=F32)


def _dot_nt(a, b):
    return lax.dot_general(a.astype(BF16), b.astype(BF16), (((1,), (1,)), ((), ())),
                           preferred_element_type=F32)


def _dot_tn(a, b):
    return lax.dot_general(a.astype(BF16), b.astype(BF16), (((0,), (0,)), ((), ())),
                           preferred_element_type=F32)


def _split2(x):
    hi = x.astype(BF16)
    lo = (x - hi.astype(F32)).astype(BF16)
    return hi, lo


def _rms_norm(x, g):
    return x * lax.rsqrt(jnp.mean(x * x, axis=-1, keepdims=True) + NORM_EPS) * g


def _head_sums(t):
    first_head = lax.broadcasted_iota(jnp.int32, (1, LANES), 1) < HEAD_DIM
    tiles = []
    for j in range(t.shape[-1] // LANES):
        tj = t[:, j * LANES:(j + 1) * LANES]
        both = jnp.sum(tj, axis=-1, keepdims=True)
        head0 = jnp.sum(jnp.where(first_head, tj, 0.0), axis=-1, keepdims=True)
        tiles.append(jnp.where(first_head, head0, both - head0))
    return jnp.concatenate(tiles, axis=1)


def _in_proj_kernel(x_ref, xp_ref, xn_ref, ln1_ref, w_ref, muc_ref, mup_ref, mun_ref, w0_ref,
                    w2_ref, a0_ref, a2_ref, g2_ref, kk_ref, ka_ref, rk_ref, tri_ref,
                    pk0_ref, pk1_ref, v_ref, pl0_ref, pl1_ref, g_ref, bonus_ref, q_ref, ka_o_ref, va_o_ref,
                    wbf_ref):
    i = pl.program_id(1)
    n_tiles = pl.num_programs(1)
    tm = x_ref.shape[1]

    @pl.when((pl.program_id(0) == 0) & (i == 0))
    def _():
        wbf_ref[...] = w_ref[...].astype(BF16)

    hm = tm // IN_PARTS
    rows = hm + 2 * SUBLANES
    core = slice(SUBLANES, hm + SUBLANES)
    halo_lo = jnp.where(i > 0, xp_ref[0], 0.0)
    halo_hi = jnp.where(i < n_tiles - 1, xn_ref[0], 0.0)
    dir_outs = ((pk0_ref, pl0_ref), (pk1_ref, pl1_ref))

    def part(lo):
        out = slice(lo, lo + hm)
        before = halo_lo if lo == 0 else x_ref[0, lo - SUBLANES:lo, :]
        after = halo_hi if lo + hm == tm else x_ref[0, lo + hm:lo + hm + SUBLANES, :]
        x_ext = jnp.concatenate([before, x_ref[0, out, :], after], axis=0)
        h = _rms_norm(x_ext, ln1_ref[...]).astype(BF16)
        proj = lambda c0, c1: _dot(h, wbf_ref[:, c0:c1])
        p_codes = proj(3 * RWKV_DIM, SHIFT_DIM)
        p_k = proj(RWKV_DIM, 2 * RWKV_DIM)
        yield

        def shifted(p_ext, c0, c1):
            prev = pltpu.roll(p_ext, 1, axis=0)[core]
            nxt = pltpu.roll(p_ext, rows - 1, axis=0)[core]
            return (p_ext[core] * muc_ref[:, c0:c1] + prev * mup_ref[:, c0:c1]
                    + nxt * mun_ref[:, c0:c1])

        codes = shifted(p_codes, 3 * RWKV_DIM, SHIFT_DIM)
        c_di = codes[:, :LANES]
        th_hi, th_lo = _split2(jnp.tanh(c_di))
        th_cat = jnp.concatenate([th_hi, th_lo], axis=1)
        gate_code = jax.nn.sigmoid(codes[:, LANES:])
        k = shifted(p_k, RWKV_DIM, 2 * RWKV_DIM)
        kkr = k * kk_ref[...]
        kkr_sq = kkr * kkr
        yield

        p_r = proj(0, RWKV_DIM)
        p_v = proj(2 * RWKV_DIM, 3 * RWKV_DIM)
        a_pre = _dot(c_di, a2_ref[...])
        g_ref[0, out, :] = _dot(gate_code, g2_ref[...]).astype(BF16)
        n2 = _head_sums(kkr_sq)
        z = [w0_ref[d:d + 1, :] + jnp.dot(th_cat, w2_ref[d], preferred_element_type=F32)
             for d in range(2)]
        yield

        r = shifted(p_r, 0, RWKV_DIM)
        v = shifted(p_v, 2 * RWKV_DIM, 3 * RWKV_DIM)
        v_ref[0, out, :] = v.astype(BF16)
        a_vec = jax.nn.sigmoid(a0_ref[...] + a_pre)
        kk = kkr * lax.rsqrt(jnp.maximum(n2, L2_EPS * L2_EPS))
        k2 = k * (1.0 + (a_vec - 1.0) * ka_ref[...])
        b_vec = kk * a_vec
        neg_kk = -kk
        rk2 = r * k2 * rk_ref[...]
        lw = [NEG_DECAY_SCALE / (1.0 + jnp.exp2(z[d] * (-LOG2E))) for d in range(2)]
        lw_split = [_split2(lw[d]) for d in range(2)]
        yield

        bonus_sum = _head_sums(rk2)
        att = proj(SHIFT_DIM, PROJ_DIM)[core]
        q_ref[0, out, :] = att[:, :ATT_DIM].astype(BF16)
        ka_o_ref[0, out, :] = att[:, ATT_DIM:ATT_DIM + KV_DIM].astype(BF16)
        va_o_ref[0, out, :] = att[:, ATT_DIM + KV_DIM:].astype(BF16)
        parts = [[jnp.dot(tri_ref[d],
                          jnp.concatenate([lw_split[d][0][cs], lw_split[d][1][cs]], axis=0),
                          preferred_element_type=F32)
                  for cs in (slice(j * CHUNK, (j + 1) * CHUNK) for j in range(hm // CHUNK))]
                 for d in range(2)]
        yield

        bonus_ref[0, out, :] = (bonus_sum * v).astype(BF16)
        for d, (pk_ref, pl_ref) in enumerate(dir_outs):
            ci = jnp.concatenate(parts[d], axis=0)
            end = 0 if d else CHUNK - 1
            tot = jnp.concatenate([c[end:end + 1] for c in parts[d]], axis=0)
            pl_ref[0, 0, lo // CHUNK:(lo + hm) // CHUNK, :] = jnp.exp2(tot)
            e_inc = jnp.exp2(ci)
            e_exc = jnp.exp2(ci - lw[d])
            e_inv = 1.0 / e_inc
            packed = (neg_kk * e_exc, r * e_inc, b_vec * e_inv, k2 * e_inv)
            for f, value in enumerate(packed):
                pk_ref[0, out, f * RWKV_DIM:(f + 1) * RWKV_DIM] = value.astype(BF16)

    live = [part(p * hm) for p in range(IN_PARTS)]
    while live:
        live = [g for g in live if next(g, StopIteration) is not StopIteration]
    if tm // CHUNK < SUBLANES:
        for d in range(2):
            dir_outs[d][1][0, 0, tm // CHUNK:, :] = jnp.ones((SUBLANES - tm // CHUNK, RWKV_DIM), F32)


def _in_proj(x, ln1_g, w_in, mu_cur, mu_prev, mu_next, decay_w0, w2_cat, iclr_a0, a2_pad, gate_g2,
             k_k, k_a, r_k, tri):
    B, T, D = x.shape
    tm = TM_IN
    nt = T // tm
    rows8 = tm // SUBLANES
    const = lambda shape: pl.BlockSpec(shape, lambda b, i: (0,) * len(shape))
    tok = lambda width: pl.BlockSpec((1, tm, width), lambda b, i: (b, i, 0))
    in_specs = [
        tok(D),
        pl.BlockSpec((1, SUBLANES, D), lambda b, i: (b, jnp.maximum(i * rows8 - 1, 0), 0)),
        pl.BlockSpec((1, SUBLANES, D), lambda b, i: (b, jnp.minimum((i + 1) * rows8, T // SUBLANES - 1), 0)),
        const((1, D)),
        pl.BlockSpec((D, PROJ_DIM), lambda b, i: (0, 0), pipeline_mode=pl.Buffered(1)),
        const((1, SHIFT_DIM)), const((1, SHIFT_DIM)), const((1, SHIFT_DIM)),
        const((2, RWKV_DIM)), const((2, 2 * LANES, RWKV_DIM)),
        const((1, RWKV_DIM)), const((LANES, RWKV_DIM)), const((LANES, RWKV_DIM)),
        const((1, RWKV_DIM)), const((1, RWKV_DIM)), const((1, RWKV_DIM)),
        const((2, CHUNK, 2 * CHUNK)),
    ]
    tok_bf = jax.ShapeDtypeStruct((B, T, RWKV_DIM), BF16)
    pl_shape = jax.ShapeDtypeStruct((B, nt, SUBLANES, RWKV_DIM), F32)
    pl_spec = pl.BlockSpec((1, 1, SUBLANES, RWKV_DIM), lambda b, i: (b, i, 0, 0))
    packed = jax.ShapeDtypeStruct((B, T, 4 * RWKV_DIM), BF16)
    out_shape = [packed, packed, tok_bf, pl_shape, pl_shape, tok_bf, tok_bf,
                                jax.ShapeDtypeStruct((B, T, ATT_DIM), BF16),
                                jax.ShapeDtypeStruct((B, T, KV_DIM), BF16),
                                jax.ShapeDtypeStruct((B, T, KV_DIM), BF16)]
    out_specs = [tok(4 * RWKV_DIM), tok(4 * RWKV_DIM), tok(RWKV_DIM), pl_spec, pl_spec, tok(RWKV_DIM), tok(RWKV_DIM),
                                       tok(ATT_DIM), tok(KV_DIM), tok(KV_DIM)]
    return pl.pallas_call(
        _in_proj_kernel, grid=(B, nt), in_specs=in_specs, out_specs=out_specs, out_shape=out_shape,
        scratch_shapes=[pltpu.VMEM((D, PROJ_DIM), BF16)],
        compiler_params=pltpu.CompilerParams(dimension_semantics=("arbitrary", "arbitrary"),
                                             vmem_limit_bytes=VMEM_LIMIT),
        name="in_proj",
    )(x, x, x, ln1_g, w_in, mu_cur, mu_prev, mu_next, decay_w0, w2_cat, iclr_a0, a2_pad, gate_g2,
      k_k, k_a, r_k, tri)


def _pair_chunks(items, states, levels, eye, lane0, bd_mask, ys):
    n = range(len(items))
    at, rt, bt, kt, v, p_last, strict, incl = zip(*items)

    def bd(x):
        x = x.astype(BF16)
        zero = jnp.zeros_like(x)
        return jnp.concatenate([jnp.where(lane0, x, zero), jnp.where(lane0, zero, x)], axis=0)

    def pmm(x, y):
        return _dot(x, bd(y))

    sc = [_dot_nt(jnp.concatenate([at[i], rt[i]], axis=0),
                  jnp.concatenate([bd(bt[i]), bd(kt[i])], axis=0)) for i in n]
    a_ab = [jnp.where(strict[i], sc[i][:CHUNK, :LANES], 0.0) for i in n]
    a_ak = [jnp.where(strict[i], sc[i][:CHUNK, LANES:], 0.0) for i in n]
    a_rb = [jnp.where(incl[i], sc[i][CHUNK:, :LANES], 0.0) for i in n]
    a_rk = [jnp.where(incl[i], sc[i][CHUNK:, LANES:], 0.0) for i in n]
    yield

    xy = [pmm(jnp.concatenate([a_ak[i], a_rk[i]], axis=0), v[i]) for i in n]
    x1 = [xy[i][:CHUNK] for i in n]
    yk = [xy[i][CHUNK:] for i in n]
    t_inv = [eye + jnp.where(levels[0], a_ab[i], 0.0) for i in n]
    yield
    for level in levels[1:]:
        e_t = [pmm(jnp.where(level, a_ab[i], 0.0), t_inv[i]) for i in n]
        yield
        t_inv = [t_inv[i] + pmm(t_inv[i], e_t[i]) for i in n]
        yield
    wu = [_dot(t_inv[i], jnp.concatenate([bd(at[i]), bd(x1[i])], axis=1)) for i in n]
    yield
    m = len(states)
    for first in range(0, len(items), m):
        n = range(first, first + m)
        hs = {i: _dot_nt(jnp.concatenate([wu[i][:, :LANES].astype(BF16), rt[i]], axis=0), states[i - first])
              for i in n}
        yield
        u = {i: hs[i][:CHUNK] + wu[i][:, LANES:] for i in n}
        ys.extend(hs[i][CHUNK:] + pmm(a_rb[i], u[i]) + yk[i] for i in n)
        yield
        upd = {i: _dot_tn(jnp.concatenate([u[i].astype(BF16), v[i]], axis=0),
                          jnp.concatenate([bt[i], kt[i]], axis=0)) for i in n}
        states[:] = [(states[i - first] + jnp.where(bd_mask, upd[i], 0.0)) * p_last[i] for i in n]
        yield


def _scan_attn_kernel(pk0_ref, v0_ref, pl0_ref, pk1_ref, v1_ref, pl1_ref, *rest, n_cast,
                      chunks_per_block):
    n_attn = ATT_QBLOCKS + 8
    attn_in = rest[:n_attn]
    cast_in = rest[n_attn:n_attn + n_cast]
    yf_ref, yb_ref, o_att_ref = rest[n_attn + n_cast:n_attn + n_cast + 3]
    cast_out = rest[n_attn + n_cast + 3:n_attn + 2 * n_cast + 3]
    s_ref = rest[-1]
    c = pl.program_id(0)
    for src_ref, dst_ref in zip(cast_in, cast_out):
        dst_ref[...] = src_ref[...].astype(BF16)

    @pl.when(c == 0)
    def _():
        s_ref[...] = jnp.zeros_like(s_ref)

    n_batch = v0_ref.shape[0]
    ri = lax.broadcasted_iota(jnp.int32, (CHUNK, LANES), 0)
    ci = lax.broadcasted_iota(jnp.int32, (CHUNK, LANES), 1)
    cj = jnp.where(ci >= CHUNK, ci - CHUNK, ci)
    lane0 = ci < CHUNK
    eye = jnp.where(ri == cj, 1.0, 0.0).astype(F32)
    levels = [((ri // (2 * s)) == (cj // (2 * s))) & ((ri // s) != (cj // s))
              for s in (2 ** e for e in range(CHUNK.bit_length() - 1))]
    r2 = lax.broadcasted_iota(jnp.int32, (LANES, LANES), 0)
    c2 = lax.broadcasted_iota(jnp.int32, (LANES, LANES), 1)
    bd_mask = (r2 >= CHUNK) == (c2 >= CHUNK)
    dirs = ((pk0_ref, v0_ref, pl0_ref, ri > cj, ri >= cj),
            (pk1_ref, v1_ref, pl1_ref, ri < cj, ri <= cj))
    n_pairs = RWKV_DIM // LANES
    n_chunks = pl.num_programs(0) * SCAN_CHUNKS
    order = [(b, d, p) for b in range(n_batch) for d in range(2) for p in range(n_pairs)]
    items = []
    for j in range(SCAN_CHUNKS):
        first = c * SCAN_CHUNKS + j
        chunk = (first, n_chunks - 1 - first)
        sub = (j, SCAN_CHUNKS - 1 - j)
        p_last = {(b, d): dirs[d][2][b, 0, pl.ds(lax.rem(chunk[d], chunks_per_block), 1), :]
                  for b in range(n_batch) for d in range(2)}
        for b, d, p in order:
            pk_ref, v_ref, _, strict, incl = dirs[d]
            rows = slice(sub[d] * CHUNK, (sub[d] + 1) * CHUNK)
            sl = slice(p * LANES, (p + 1) * LANES)
            field = lambda f: pk_ref[b, rows, f * RWKV_DIM + p * LANES:f * RWKV_DIM + (p + 1) * LANES]
            items.append((field(0), field(1), field(2), field(3), v_ref[b, rows, sl],
                          p_last[b, d][:, sl], strict, incl))
    states = [s_ref[b, d, p] for b, d, p in order]
    ys = []
    scan = _pair_chunks(items, states, levels, eye, lane0, bd_mask, ys)
    attn = _attn_stages(*attn_in, o_att_ref)
    while next(scan, StopIteration) is not StopIteration:
        next(attn, None)
    for _ in attn:
        pass
    for j in range(SCAN_CHUNKS):
        sub = (j, SCAN_CHUNKS - 1 - j)
        for b in range(n_batch):
            for d, y_ref in enumerate((yf_ref, yb_ref)):
                base = j * len(order) + (b * 2 + d) * n_pairs
                y_ref[b, sub[d] * CHUNK:(sub[d] + 1) * CHUNK, :] = jnp.concatenate(
                    ys[base:base + n_pairs], axis=1).astype(BF16)
    for (b, d, p), state in zip(order, states):
        s_ref[b, d, p] = state


def _attn_stages(sink_ref, *refs):
    n_sub = ATT_QBLOCKS
    bias_refs = refs[:n_sub]
    q_ref, kp_ref, kc_ref, kn_ref, vp_ref, vc_ref, vn_ref, o_ref = refs[n_sub:]
    blk = WINDOW
    k_all = jnp.concatenate([kp_ref[0], kc_ref[0], kn_ref[0]], axis=0)
    v_all = jnp.concatenate([vp_ref[0], vc_ref[0], vn_ref[0]], axis=0)

    def swap_halves(x):
        return jnp.concatenate([x[:, HEAD_DIM:], x[:, :HEAD_DIM]], axis=1)

    lane0 = lax.broadcasted_iota(jnp.int32, k_all.shape, 1) < HEAD_DIM

    def variants(x):
        xs = swap_halves(x)
        zero = jnp.zeros_like(x)
        return ((jnp.where(lane0, x, zero), jnp.where(lane0, zero, xs)),
                (jnp.where(lane0, xs, zero), jnp.where(lane0, zero, x)))

    k_var = variants(k_all)
    v_var = variants(v_all)
    yield

    group = ATT_HEADS // (KV_DIM // HEAD_DIM)

    items = [(u, h) for u in range(n_sub) for h in range(ATT_HEADS)]
    keys = lambda var, u, h: var[h // group][h % 2][u * blk:(u + 3) * blk]
    q_pairs = [[q_ref[0, u * blk:(u + 1) * blk, j * LANES:(j + 1) * LANES]
                * jnp.asarray(HEAD_DIM ** -0.5, BF16) for j in range(ATT_DIM // LANES)]
               for u in range(n_sub)]
    s = [_dot_nt(q_pairs[u][h // 2], keys(k_var, u, h)) + bias_refs[u][0, h] for u, h in items]
    yield
    m = [jnp.maximum(jnp.max(s[i], axis=-1, keepdims=True), sink_ref[h]) for i, (u, h) in enumerate(items)]
    yield
    p = [jnp.exp(s[i] - m[i]) for i in range(len(items))]
    yield
    den = [jnp.sum(p[i], axis=-1, keepdims=True) + jnp.exp(sink_ref[h] - m[i])
           for i, (u, h) in enumerate(items)]
    yield
    o = [_dot(p[i], keys(v_var, u, h)) * (1.0 / den[i]) for i, (u, h) in enumerate(items)]
    yield
    for u in range(n_sub):
        ou = o[u * ATT_HEADS:(u + 1) * ATT_HEADS]
        o_ref[0, u * blk:(u + 1) * blk, :] = jnp.concatenate(
            [ou[2 * j] + ou[2 * j + 1] for j in range(ATT_DIM // LANES)], axis=1).astype(o_ref.dtype)


def _attn_bias(blk):
    qi = np.arange(blk)[:, None]
    kj = np.arange(3 * blk)[None, :]
    dist = np.abs(kj - blk - qi)
    slopes = 2.0 ** (-8.0 * np.arange(1, ATT_HEADS + 1, dtype=np.float32) / ATT_HEADS)
    alibi = -slopes[:, None, None].astype(np.float32) * dist[None].astype(np.float32)
    out = []
    for has_prev, has_next in ((False, True), (True, True), (True, False)):
        valid = (dist <= WINDOW) & (has_prev | (kj >= blk)) & (has_next | (kj < 2 * blk))
        out.append(np.where(valid[None], alibi, np.float32(MASK_VALUE)))
    return jnp.asarray(np.stack(out), F32)


def _scan_attention(pk0, pk1, v, pl0, pl1, q, ka, va, sink, weights):
    B, T, C = v.shape
    nc = T // (CHUNK * SCAN_CHUNKS)
    blk = WINDOW
    n_sub = ATT_QBLOCKS
    per_row = T // (blk * n_sub)
    nb = T // blk
    assert B * per_row == nc, "scan and attention must have the same number of grid steps"
    fwd = pl.BlockSpec((B, SCAN_CHUNKS * CHUNK, C), lambda c: (0, c, 0))
    bwd = pl.BlockSpec((B, SCAN_CHUNKS * CHUNK, C), lambda c: (0, nc - 1 - c, 0))
    fwd_pk = pl.BlockSpec((B, SCAN_CHUNKS * CHUNK, 4 * C), lambda c: (0, c, 0))
    bwd_pk = pl.BlockSpec((B, SCAN_CHUNKS * CHUNK, 4 * C), lambda c: (0, nc - 1 - c, 0))
    cpb = TM_IN // CHUNK
    assert cpb % SCAN_CHUNKS == 0, "a step's chunks must share one block of chunk decays"
    spb = cpb // SCAN_CHUNKS
    pl_f = pl.BlockSpec((B, 1, SUBLANES, C), lambda c: (0, c // spb, 0, 0))
    pl_b = pl.BlockSpec((B, 1, SUBLANES, C), lambda c: (0, (nc - 1 - c) // spb, 0, 0))
    y_shape = jax.ShapeDtypeStruct((B, T, C), BF16)

    row = lambda c: c // per_row
    step = lambda c: lax.rem(c, per_row)
    cur = lambda width: pl.BlockSpec((1, n_sub * blk, width), lambda c: (row(c), step(c), 0))
    prev = pl.BlockSpec((1, blk, KV_DIM), lambda c: (row(c), jnp.maximum(step(c) * n_sub - 1, 0), 0))
    nxt = pl.BlockSpec((1, blk, KV_DIM), lambda c: (row(c), jnp.minimum((step(c) + 1) * n_sub, nb - 1), 0))

    def bias_spec(u):
        def index(c):
            g = step(c) * n_sub + u
            return (jnp.where(g == 0, 0, jnp.where(g == nb - 1, 2, 1)), 0, 0, 0)
        return pl.BlockSpec((1, ATT_HEADS, blk, 3 * blk), index)

    def cast_spec(w):
        rows, cols = w.shape
        per_step = next(r for r in range(BF16_ROWS, rows + 1, BF16_ROWS)
                        if rows % r == 0 and rows // r <= nc)
        last = rows // per_step - 1
        return pl.BlockSpec((per_step, cols), lambda c: (jnp.minimum(c, last), 0))

    cast_specs = [cast_spec(w) for w in weights]
    bias = _attn_bias(blk)
    outs = pl.pallas_call(
        functools.partial(_scan_attn_kernel, n_cast=len(weights), chunks_per_block=cpb), grid=(nc,),
        in_specs=[fwd_pk, fwd, pl_f, bwd_pk, bwd, pl_b,
                  pl.BlockSpec(memory_space=pltpu.SMEM)] + [bias_spec(u) for u in range(n_sub)]
                 + [cur(ATT_DIM), prev, cur(KV_DIM), nxt, prev, cur(KV_DIM), nxt] + cast_specs,
        out_specs=[fwd, bwd, cur(ATT_DIM)] + cast_specs,
        out_shape=[y_shape, y_shape, jax.ShapeDtypeStruct((B, T, ATT_DIM), BF16)]
                  + [jax.ShapeDtypeStruct(w.shape, BF16) for w in weights],
        scratch_shapes=[pltpu.VMEM((B, 2, C // LANES, LANES, LANES), F32)],
        compiler_params=pltpu.CompilerParams(dimension_semantics=("arbitrary",),
                                             vmem_limit_bytes=VMEM_LIMIT),
        name="scan_attn",
    )(pk0, v, pl0, pk1, v, pl1,
      sink, *([bias] * n_sub), q, ka, ka, ka, va, va, va, *weights)
    return outs[0], outs[1], outs[2], outs[3:]


HALO = BF16_ROWS


def _mix_ffn_kernel(*refs):
    (x_m, x_p, x_n, yf_m, yf_p, yf_n, yb_m, yb_p, yb_n, bo_m, bo_p, bo_n, g_m, g_p, g_n,
     oa_m, oa_p, oa_n, lg_ref, lb_ref, wo_ref, ln2_ref, wg_ref, wu_ref, cw_ref, cb_ref,
     wd_ref, lnf_ref, o_ref) = refs
    i = pl.program_id(1)
    n_tiles = pl.num_programs(1)
    tm = x_m.shape[1]
    rows = tm + 2 * HALO
    core = slice(HALO, tm + HALO)
    ext = lambda m, p, n: jnp.concatenate([p[0], m[0], n[0]], axis=0)

    def seg_mean(t):
        return _head_sums(t) * (1.0 / HEAD_DIM)

    y = ext(yf_m, yf_p, yf_n).astype(F32) + ext(yb_m, yb_p, yb_n).astype(F32)
    d = y - seg_mean(y)
    var = seg_mean(d * d)
    yn = d * lax.rsqrt(var + LNX_EPS) * lg_ref[...] + lb_ref[...]
    o_rwkv = (yn + ext(bo_m, bo_p, bo_n)) * ext(g_m, g_p, g_n)
    mix = _dot(o_rwkv, wo_ref[:RWKV_DIM, :]) + _dot(ext(oa_m, oa_p, oa_n), wo_ref[RWKV_DIM:, :])
    x1 = ext(x_m, x_p, x_n) + mix
    r = lax.broadcasted_iota(jnp.int32, (rows, 1), 0)
    inside = ((r >= HALO) | (i > 0)) & ((r < tm + HALO) | (i < n_tiles - 1))
    x1 = jnp.where(inside, x1, 0.0)

    h = _rms_norm(x1, ln2_ref[...]).astype(BF16)
    h_core = h[core]
    acc = jnp.zeros((tm, D_MODEL), F32)
    for c0, c1 in zip(FF_SPLITS[:-1], FF_SPLITS[1:]):
        cs = slice(c0, c1)
        gp_ext = _dot(h, wg_ref[:, cs])
        prev = pltpu.roll(gp_ext, 1, axis=0)[core]
        nxt = pltpu.roll(gp_ext, rows - 1, axis=0)[core]
        gate = (prev * cw_ref[0:1, cs] + gp_ext[core] * cw_ref[1:2, cs] + nxt * cw_ref[2:3, cs]
                + cb_ref[:, cs])
        act = 0.5 * gate * (1.0 + lax.erf(gate * float(1.0 / np.sqrt(2.0))))
        up = _dot(h_core, wu_ref[:, cs])
        acc = acc + _dot(act * up, wd_ref[cs, :])
    o_ref[0] = _rms_norm(x1[core] + acc, lnf_ref[...])


def _mix_ffn(x, yf, yb, bonus, g, o_att, lnx_g, lnx_b, w_out, ln2_g, wg, wu, conv_w, conv_b,
             wd, lnf_g):
    B, T, D = x.shape
    tm = TM_FFN
    per_tile = tm // HALO
    last = T // HALO - 1

    def tok(width):
        return [pl.BlockSpec((1, tm, width), lambda b, i: (b, i, 0)),
                pl.BlockSpec((1, HALO, width), lambda b, i: (b, jnp.maximum(i * per_tile - 1, 0), 0)),
                pl.BlockSpec((1, HALO, width), lambda b, i: (b, jnp.minimum((i + 1) * per_tile, last), 0))]

    const = lambda shape: pl.BlockSpec(shape, lambda b, i: (0,) * len(shape))
    resident = lambda shape: pl.BlockSpec(shape, lambda b, i: (0,) * len(shape),
                                          pipeline_mode=pl.Buffered(1))
    tokens = (x, yf, yb, bonus, g, o_att)
    return pl.pallas_call(
        _mix_ffn_kernel, grid=(B, T // tm),
        in_specs=[spec for a in tokens for spec in tok(a.shape[-1])]
                 + [const((1, RWKV_DIM)), const((1, RWKV_DIM)), const((D, D)),
                    const((1, D)), resident((D, D_FF)), resident((D, D_FF)),
                    const((CONV_WIDTH, D_FF)), const((1, D_FF)), resident((D_FF, D)), const((1, D))],
        out_specs=pl.BlockSpec((1, tm, D), lambda b, i: (b, i, 0)),
        out_shape=jax.ShapeDtypeStruct((B, T, D), F32),
        compiler_params=pltpu.CompilerParams(dimension_semantics=("parallel", "parallel"),
                                             vmem_limit_bytes=VMEM_LIMIT),
        name="mix_ffn",
    )(*[a for a in tokens for _ in range(3)], lnx_g, lnx_b, w_out, ln2_g, wg, wu, conv_w,
      conv_b, wd, lnf_g)


def _chunk_triangles():
    t = np.arange(CHUNK)
    tri = np.stack([t[:, None] >= t[None, :], t[:, None] <= t[None, :]]).astype(np.float32)
    return jnp.asarray(np.concatenate([tri, tri], axis=2), BF16)


def kernel(x, ln1_g, w_in, shift_mu_prev, shift_mu_next, decay_w0, decay_w2, iclr_a0, iclr_a2,
           gate_g2, k_k, k_a, r_k, lnx_g, lnx_b, attn_sink, w_out, ln2_g, ffn_w_gate, ffn_w_up,
           ffn_conv_w, ffn_conv_b, ffn_w_down, lnf_g):
    B, T, _ = x.shape
    assert w_in.shape[0] == 1, "single-layer block"
    l = 0
    tri = _chunk_triangles()
    row = lambda a: a.reshape(1, -1)
    w2 = decay_w2[l]
    w2_pad = jnp.concatenate([w2, jnp.zeros_like(w2)], axis=1)
    w2_bf = w2_pad.astype(BF16)
    w2_cat = jnp.concatenate([w2_bf, w2_bf], axis=1)
    mu_p, mu_n = shift_mu_prev[l], shift_mu_next[l]
    a2_pad = jnp.concatenate([jnp.zeros_like(iclr_a2[l]), iclr_a2[l]], axis=0).astype(BF16)
    (pk0, pk1, v, pl0, pl1, g, bonus, q, ka, va) = _in_proj(
        x, row(ln1_g[l]), w_in[l], row(1.0 - mu_p - mu_n), row(mu_p), row(mu_n),
        decay_w0[l], w2_cat, row(iclr_a0[l]), a2_pad, gate_g2[l].astype(BF16),
        row(k_k[l]), row(k_a[l]), row(r_k[l]), tri)
    yf, yb, o_att, (wo, wg, wu, wd) = _scan_attention(
        pk0, pk1, v, pl0, pl1, q, ka, va, attn_sink[l],
        (w_out[l], ffn_w_gate[l], ffn_w_up[l], ffn_w_down[l]))
    return _mix_ffn(x, yf, yb, bonus, g, o_att, row(lnx_g[l]), row(lnx_b[l]), wo,
                    row(ln2_g[l]), wg, wu, ffn_conv_w[l], row(ffn_conv_b[l]), wd, row(lnf_g))
```
